```python
import jax, jax.numpy as jnp
from jax import lax
import numpy as np

D_MODEL = 2048
BATCH = 4
SEQ = 4096
DEPTH = 2

D_LRU = 1024
LRU_BLOCKS = 16
LRU_BW = D_LRU // LRU_BLOCKS
CONV_WIDTH = 4
LRU_C = 8.0
N_HEADS = 8
HEAD_DIM = 128
D_ATTN = N_HEADS * HEAD_DIM
MOBA_BLOCK = 256
MOBA_TOPK = 3
Q_CHUNK = 32
ROPE_THETA = 500000.0
ROT_DIM = HEAD_DIM // 4
D_FF = 5632
NORM_EPS = 1e-6
IN_COLS = D_LRU + 3 * D_ATTN + 2 * D_MODEL

kernel_name = "hybrid_rglru_moba_macaron"


def rms_norm(x, g):
    xf = x.astype(jnp.float32)
    y = xf * lax.rsqrt(jnp.mean(xf * xf, axis=-1, keepdims=True) + NORM_EPS)
    return (y * g.astype(jnp.float32)).astype(x.dtype)


def swiglu(x, w13, w2):
    a, b = jnp.split(x @ w13, 2, axis=-1)
    return (jax.nn.silu(a) * b) @ w2


def causal_conv(u, w, b):
    s = u.shape[1]
    up = jnp.pad(u, ((0, 0), (CONV_WIDTH - 1, 0), (0, 0)))
    y = b
    for tap in range(CONV_WIDTH):
        y = y + up[:, tap:tap + s] * w[tap]
    return y


def rg_lru(u, wa, ba, wx, bx, lam):
    bsz, s, _ = u.shape
    uf = u.astype(jnp.float32)
    ub = uf.reshape(bsz, s, LRU_BLOCKS, LRU_BW)
    r = jax.nn.sigmoid(jnp.einsum('bsni,nij->bsnj', ub, wa.astype(jnp.float32)).reshape(bsz, s, D_LRU) + ba.astype(jnp.float32))
    i = jax.nn.sigmoid(jnp.einsum('bsni,nij->bsnj', ub, wx.astype(jnp.float32)).reshape(bsz, s, D_LRU) + bx.astype(jnp.float32))
    log_a = -LRU_C * r * jax.nn.softplus(-lam.astype(jnp.float32))
    a = jnp.exp(log_a)
    b = jnp.sqrt(-jnp.expm1(2.0 * log_a)) * (i * uf)

    def combine(left, right):
        a_l, b_l = left
        a_r, b_r = right
        return a_l * a_r, a_r * b_l + b_r

    _, h = lax.associative_scan(combine, (a, b), axis=1)
    return h.astype(u.dtype)


def rope_tables(s):
    pos = jnp.arange(s, dtype=jnp.float32)
    inv = ROPE_THETA ** (-jnp.arange(0, ROT_DIM, 2, dtype=jnp.float32) / ROT_DIM)
    ang = pos[:, None] * inv[None, :]
    return jnp.cos(ang), jnp.sin(ang)


def partial_rotary(x, cos, sin):
    cos = cos.astype(x.dtype)
    sin = sin.astype(x.dtype)
    x1 = x[..., :ROT_DIM // 2]
    x2 = x[..., ROT_DIM // 2:ROT_DIM]
    return jnp.concatenate([x1 * cos - x2 * sin, x2 * cos + x1 * sin, x[..., ROT_DIM:]], axis=-1)


def moba_attention(q, k, v):
    bsz, nh, s, dh = q.shape
    nb = -(-s // MOBA_BLOCK)
    sp = nb * MOBA_BLOCK
    padw = ((0, 0), (0, 0), (0, sp - s), (0, 0))
    q = jnp.pad(q, padw)
    k = jnp.pad(k, padw)
    v = jnp.pad(v, padw)
    kb = k.reshape(bsz, nh, nb, MOBA_BLOCK, dh)
    vb = v.reshape(bsz, nh, nb, MOBA_BLOCK, dh)
    n_sel = min(MOBA_TOPK, nb - 1)
    scale = dh ** -0.5
    nc = sp // Q_CHUNK
    qc = q.reshape(bsz, nh, nc, Q_CHUNK, dh).transpose(2, 0, 1, 3, 4)
    chunk_ids = jnp.arange(nc)

    if n_sel > 0:
        kmean = jnp.mean(kb.astype(jnp.float32), axis=3)
        gate = jnp.einsum('bhsd,bhnd->bhsn', q.astype(jnp.float32), kmean)
        q_blk = jnp.arange(sp) // MOBA_BLOCK
        past = jnp.arange(nb)[None, :] < q_blk[:, None]
        gate = jnp.where(past, gate, -jnp.inf)
        _, sel = lax.top_k(gate, n_sel)
        selc = sel.reshape(bsz, nh, nc, Q_CHUNK, n_sel).transpose(2, 0, 1, 3, 4)
    else:
        selc = jnp.zeros((nc, bsz, nh, Q_CHUNK, 0), jnp.int32)
    bi = jnp.arange(bsz)[:, None, None, None]
    hi = jnp.arange(nh)[None, :, None, None]

    def chunk_attn(args):
        c, qi, si = args
        qpos = c * Q_CHUNK + jnp.arange(Q_CHUNK)
        j = (c * Q_CHUNK) // MOBA_BLOCK
        k_own = lax.dynamic_index_in_dim(kb, j, axis=2, keepdims=False)
        v_own = lax.dynamic_index_in_dim(vb, j, axis=2, keepdims=False)
        kpos = j * MOBA_BLOCK + jnp.arange(MOBA_BLOCK)
        s_own = jnp.einsum('bhqd,bhtd->bhqt', qi, k_own).astype(jnp.float32) * scale
        s_own = jnp.where(kpos[None, :] <= qpos[:, None], s_own, -jnp.inf)
        if n_sel > 0:
            k_sel = kb[bi, hi, si]
            v_sel = vb[bi, hi, si]
            s_sel = jnp.einsum('bhqd,bhqntd->bhqnt', qi, k_sel).astype(jnp.float32) * scale
            valid = jnp.arange(n_sel) < j
            s_sel = jnp.where(valid[:, None], s_sel, -jnp.inf)
            s_all = jnp.concatenate([s_sel.reshape(bsz, nh, Q_CHUNK, n_sel * MOBA_BLOCK), s_own], axis=-1)
            p = jax.nn.softmax(s_all, axis=-1).astype(v.dtype)
            p_sel = p[..., :n_sel * MOBA_BLOCK].reshape(bsz, nh, Q_CHUNK, n_sel, MOBA_BLOCK)
            p_own = p[..., n_sel * MOBA_BLOCK:]
            return (jnp.einsum('bhqnt,bhqntd->bhqd', p_sel, v_sel)
                    + jnp.einsum('bhqt,bhtd->bhqd', p_own, v_own))
        p_own = jax.nn.softmax(s_own, axis=-1).astype(v.dtype)
        return jnp.einsum('bhqt,bhtd->bhqd', p_own, v_own)

    out = lax.map(chunk_attn, (chunk_ids, qc, selc))
    out = out.transpose(1, 2, 0, 3, 4).reshape(bsz, nh, sp, dh)
    return out[:, :, :s]


def hybrid_mixer(h, w_in, b_gate, conv_w, conv_b, wa, ba, wx, bx, lam,
                 w_lru_up, w_attn_up, w_out, cos, sin):
    bsz, s, _ = h.shape
    z = h @ w_in
    u, q, k, v, g = jnp.split(z, [D_LRU, D_LRU + D_ATTN, D_LRU + 2 * D_ATTN, D_LRU + 3 * D_ATTN], axis=-1)
    y_a = rg_lru(causal_conv(u, conv_w, conv_b), wa, ba, wx, bx, lam) @ w_lru_up
    heads = lambda t: t.reshape(bsz, s, N_HEADS, HEAD_DIM).transpose(0, 2, 1, 3)
    qh = partial_rotary(heads(q), cos, sin)
    kh = partial_rotary(heads(k), cos, sin)
    o = moba_attention(qh, kh, heads(v)).transpose(0, 2, 1, 3).reshape(bsz, s, D_ATTN)
    y_b = o @ w_attn_up
    g_a, g_b = jnp.split(jax.nn.sigmoid(g + b_gate), 2, axis=-1)
    return (g_a * y_a + g_b * y_b) @ w_out


def setup_inputs(seed: int = 0) -> dict:
    key = jax.random.key(seed)
    ks = jax.random.split(key, 18)
    f32 = jnp.float32
    nrm = lambda k, shape, scale: jax.random.normal(k, shape, f32) * scale
    x = nrm(ks[0], (BATCH, SEQ, D_MODEL), 1.0)
    norm_gains = 1.0 + nrm(ks[1], (DEPTH, 6, D_MODEL), 0.02)
    ffn_w13 = nrm(ks[2], (DEPTH, 2, D_MODEL, 2 * D_FF), D_MODEL ** -0.5)
    ffn_w2 = nrm(ks[3], (DEPTH, 2, D_FF, D_MODEL), D_FF ** -0.5)
    w_in = nrm(ks[4], (DEPTH, D_MODEL, IN_COLS), D_MODEL ** -0.5)
    b_gate = nrm(ks[5], (DEPTH, 2 * D_MODEL), 0.02)
    conv_w = nrm(ks[6], (DEPTH, CONV_WIDTH, D_LRU), CONV_WIDTH ** -0.5)
    conv_b = nrm(ks[7], (DEPTH, D_LRU), 0.02)
    lru_wa = nrm(ks[8], (DEPTH, LRU_BLOCKS, LRU_BW, LRU_BW), LRU_BW ** -0.5)
    lru_ba = nrm(ks[9], (DEPTH, D_LRU), 0.02)
    lru_wx = nrm(ks[10], (DEPTH, LRU_BLOCKS, LRU_BW, LRU_BW), LRU_BW ** -0.5)
    lru_bx = nrm(ks[11], (DEPTH, D_LRU), 0.02)
    a_c = jax.random.uniform(ks[12], (DEPTH, D_LRU), f32, 0.9, 0.999)
    sig = a_c ** (1.0 / LRU_C)
    lru_lambda = jnp.log(sig) - jnp.log1p(-sig)
    w_lru_up = nrm(ks[13], (DEPTH, D_LRU, D_MODEL), D_LRU ** -0.5)
    w_attn_up = nrm(ks[14], (DEPTH, D_ATTN, D_MODEL), D_ATTN ** -0.5)
    w_out = nrm(ks[15], (DEPTH, D_MODEL, D_MODEL), D_MODEL ** -0.5)
    return {"x": x, "norm_gains": norm_gains, "ffn_w13": ffn_w13, "ffn_w2": ffn_w2,
            "w_in": w_in, "b_gate": b_gate, "conv_w": conv_w, "conv_b": conv_b,
            "lru_wa": lru_wa, "lru_ba": lru_ba, "lru_wx": lru_wx, "lru_bx": lru_bx,
            "lru_lambda": lru_lambda, "w_lru_up": w_lru_up, "w_attn_up": w_attn_up,
            "w_out": w_out}


def reference(x, norm_gains, ffn_w13, ffn_w2, w_in, b_gate, conv_w, conv_b,
              lru_wa, lru_ba, lru_wx, lru_bx, lru_lambda, w_lru_up, w_attn_up, w_out):
    cos, sin = rope_tables(x.shape[1])
    for l in range(DEPTH):
        ng = norm_gains[l]
        x = x + 0.5 * rms_norm(swiglu(rms_norm(x, ng[0]), ffn_w13[l, 0], ffn_w2[l, 0]), ng[1])
        m = hybrid_mixer(rms_norm(x, ng[2]), w_in[l], b_gate[l], conv_w[l], conv_b[l],
                         lru_wa[l], lru_ba[l], lru_wx[l], lru_bx[l], lru_lambda[l],
                         w_lru_up[l], w_attn_up[l], w_out[l], cos, sin)
        x = x + rms_norm(m, ng[3])
        x = x + 0.5 * rms_norm(swiglu(rms_norm(x, ng[4]), ffn_w13[l, 1], ffn_w2[l, 1]), ng[5])
    return x
```

```python
import functools

import jax
import jax.numpy as jnp
from jax import lax
from jax.experimental import pallas as pl
from jax.experimental.pallas import tpu as pltpu

F32 = jnp.float32
BF16 = jnp.bfloat16

D_MODEL = 2048
D_LRU = 1024
LRU_BLOCKS = 16
LRU_BW = D_LRU // LRU_BLOCKS
CONV_WIDTH = 4
LRU_C = 8.0
N_HEADS = 8
HEAD_DIM = 128
D_ATTN = N_HEADS * HEAD_DIM
MOBA_BLOCK = 256
MOBA_TOPK = 3
ROPE_THETA = 500000.0
ROT_DIM = HEAD_DIM // 4
ROT_HALF = ROT_DIM // 2
D_FF = 5632
NORM_EPS = 1e-6
IN_COLS = D_LRU + 3 * D_ATTN + 2 * D_MODEL

LANES = 128
SUBLANES = 8
VMEM_LIMIT_BYTES = 56 * 1024 * 1024

TM = 512
TF = 512
TN_IN = 1024
TC_MIX = 512
TS_LRU = 512
GATE_GROUP = 2 * LRU_BW
N_GATE_GROUPS = D_LRU // GATE_GROUP


def _rms_scale(x):
    return lax.rsqrt(jnp.mean(x * x, axis=-1, keepdims=True) + NORM_EPS)


def _params(semantics):
    return pltpu.CompilerParams(dimension_semantics=semantics,
                                vmem_limit_bytes=VMEM_LIMIT_BYTES)


def _ffn_kernel(x_ref, gin_ref, gout_ref, wa_ref, wb_ref, w2_ref, o_ref, xn_ref, acc_ref):
    j = pl.program_id(1)

    @pl.when(j == 0)
    def _():
        x = x_ref[...]
        xn_ref[...] = (x * _rms_scale(x) * gin_ref[...]).astype(BF16)
        acc_ref[...] = jnp.zeros_like(acc_ref)

    xn = xn_ref[...]
    a = jnp.dot(xn, wa_ref[...], preferred_element_type=F32)
    b = jnp.dot(xn, wb_ref[...], preferred_element_type=F32)
    h = (a * jax.nn.sigmoid(a) * b).astype(BF16)
    acc_ref[...] += jnp.dot(h, w2_ref[...], preferred_element_type=F32)

    @pl.when(j == pl.num_programs(1) - 1)
    def _():
        y = acc_ref[...]
        o_ref[...] = x_ref[...] + 0.5 * (y * _rms_scale(y) * gout_ref[...])


def _ffn(x, gin, gout, w13, w2, layer, half):
    t = x.shape[0]
    nf = D_FF // TF
    return pl.pallas_call(
        _ffn_kernel,
        grid=(t // TM, nf),
        in_specs=[
            pl.BlockSpec((TM, D_MODEL), lambda i, j: (i, 0)),
            pl.BlockSpec((1, D_MODEL), lambda i, j: (0, 0)),
            pl.BlockSpec((1, D_MODEL), lambda i, j: (0, 0)),
            pl.BlockSpec((None, None, D_MODEL, TF), lambda i, j: (layer, half, 0, j)),
            pl.BlockSpec((None, None, D_MODEL, TF), lambda i, j: (layer, half, 0, j + nf)),
            pl.BlockSpec((None, None, TF, D_MODEL), lambda i, j: (layer, half, j, 0)),
        ],
        out_specs=pl.BlockSpec((TM, D_MODEL), lambda i, j: (i, 0)),
        out_shape=jax.ShapeDtypeStruct((t, D_MODEL), F32),
        scratch_shapes=[pltpu.VMEM((TM, D_MODEL), BF16), pltpu.VMEM((TM, D_MODEL), F32)],
        compiler_params=_params(("parallel", "arbitrary")),
        name="ffn_half_step",
    )(x, gin, gout, w13, w13, w2)


def _rotary(acc, cos_ref, sin_lo_ref, sin_hi_ref):
    cos = cos_ref[...]
    sin_lo = sin_lo_ref[...]
    sin_hi = sin_hi_ref[...]
    heads = []
    for hd in range(N_HEADS):
        xh = acc[:, hd * HEAD_DIM:(hd + 1) * HEAD_DIM]
        up = pltpu.roll(xh, HEAD_DIM - ROT_HALF, 1)
        down = pltpu.roll(xh, ROT_HALF, 1)
        heads.append(xh * cos + up * sin_lo + down * sin_hi)
    return jnp.concatenate(heads, axis=1)


def _inproj_kernel(x_ref, g_ref, w_ref, bg_ref, cos_ref, sin_lo_ref, sin_hi_ref,
                   z_ref, xn_ref):
    j = pl.program_id(1)

    @pl.when(j == 0)
    def _():
        x = x_ref[...]
        xn_ref[...] = (x * _rms_scale(x) * g_ref[...]).astype(BF16)

    acc = jnp.dot(xn_ref[...], w_ref[...], preferred_element_type=F32)
    q_blk = D_LRU // TN_IN
    k_blk = (D_LRU + D_ATTN) // TN_IN
    v_blk = (D_LRU + 2 * D_ATTN) // TN_IN
    gate_blk = (D_LRU + 3 * D_ATTN) // TN_IN

    @pl.when((j < q_blk) | (j == v_blk))
    def _():
        z_ref[...] = acc.astype(BF16)

    @pl.when(j == q_blk)
    def _():
        rot = _rotary(acc, cos_ref, sin_lo_ref, sin_hi_ref)
        z_ref[...] = (rot * (HEAD_DIM ** -0.5)).astype(BF16)

    @pl.when(j == k_blk)
    def _():
        z_ref[...] = _rotary(acc, cos_ref, sin_lo_ref, sin_hi_ref).astype(BF16)

    @pl.when(j >= gate_blk)
    def _():
        z_ref[...] = jax.nn.sigmoid(acc + bg_ref[...]).astype(BF16)


def _inproj(x, g, w_in, b_gate, cos_t, sin_lo_t, sin_hi_t, layer, seq):
    t = x.shape[0]
    gate_blk = (D_LRU + 3 * D_ATTN) // TN_IN
    seq_tiles = seq // TM
    return pl.pallas_call(
        _inproj_kernel,
        grid=(t // TM, IN_COLS // TN_IN),
        in_specs=[
            pl.BlockSpec((TM, D_MODEL), lambda i, j: (i, 0)),
            pl.BlockSpec((1, D_MODEL), lambda i, j: (0, 0)),
            pl.BlockSpec((None, D_MODEL, TN_IN), lambda i, j: (layer, 0, j)),
            pl.BlockSpec((None, 1, TN_IN), lambda i, j: (layer, 0, jnp.maximum(j - gate_blk, 0))),
            pl.BlockSpec((TM, HEAD_DIM), lambda i, j: (i % seq_tiles, 0)),
            pl.BlockSpec((TM, HEAD_DIM), lambda i, j: (i % seq_tiles, 0)),
            pl.BlockSpec((TM, HEAD_DIM), lambda i, j: (i % seq_tiles, 0)),
        ],
        out_specs=pl.BlockSpec((TM, TN_IN), lambda i, j: (i, j)),
        out_shape=jax.ShapeDtypeStruct((t, IN_COLS), BF16),
        scratch_shapes=[pltpu.VMEM((TM, D_MODEL), BF16)],
        compiler_params=_params(("parallel", "arbitrary")),
        name="mixer_in_projection",
    )(x, g, w_in, b_gate, cos_t, sin_lo_t, sin_hi_t)


def _lru_kernel(u_ref, cw_ref, cb_ref, wg_ref, ba_ref, bx_ref, lam_ref, h_ref,
                ubuf_ref, a_ref, b_ref, hout_ref, carry_ref):
    ts = TS_LRU
    pad = SUBLANES

    @pl.when(pl.program_id(1) == 0)
    def _():
        ubuf_ref[0:pad, :] = jnp.zeros((pad, D_LRU), F32)
        carry_ref[...] = jnp.zeros_like(carry_ref)

    ubuf_ref[pad:pad + ts, :] = u_ref[0].astype(F32)
    xc = cb_ref[...]
    for tap in range(CONV_WIDTH):
        off = pad - (CONV_WIDTH - 1) + tap
        xc = xc + ubuf_ref[off:off + ts, :] * cw_ref[tap:tap + 1, :]
    ubuf_ref[0:pad, :] = ubuf_ref[ts:ts + pad, :]

    neg_c_softplus = -LRU_C * jax.nn.softplus(-lam_ref[...])
    for g in range(N_GATE_GROUPS):
        cols = slice(g * GATE_GROUP, (g + 1) * GATE_GROUP)
        xg = xc[:, cols]
        lin = jnp.dot(xg.astype(BF16), wg_ref[g], preferred_element_type=F32)
        r = jax.nn.sigmoid(lin[:, :GATE_GROUP] + ba_ref[:, cols])
        i = jax.nn.sigmoid(lin[:, GATE_GROUP:] + bx_ref[:, cols])
        log_a = r * neg_c_softplus[:, cols]
        a = jnp.exp(log_a)
        a_ref[:, cols] = a
        b_ref[:, cols] = jnp.sqrt(1.0 - a * a) * (i * xg)

    def step(t, h):
        h = a_ref[pl.ds(t, 1), :] * h + b_ref[pl.ds(t, 1), :]
        hout_ref[pl.ds(t, 1), :] = h
        return h

    carry_ref[...] = lax.fori_loop(0, ts, step, carry_ref[...], unroll=8)
    h_ref[0] = hout_ref[...].astype(BF16)


def _lru(z3, conv_w, conv_b, w_gate, ba, bx, lam, layer):
    bsz, seq, _ = z3.shape
    vec = lambda: pl.BlockSpec((None, 1, D_LRU), lambda b, t: (layer, 0, 0))
    return pl.pallas_call(
        _lru_kernel,
        grid=(bsz, seq // TS_LRU),
        in_specs=[
            pl.BlockSpec((1, TS_LRU, D_LRU), lambda b, t: (b, t, 0)),
            pl.BlockSpec((None, CONV_WIDTH, D_LRU), lambda b, t: (layer, 0, 0)),
            vec(),
            pl.BlockSpec((None, N_GATE_GROUPS, GATE_GROUP, 2 * GATE_GROUP),
                         lambda b, t: (layer, 0, 0, 0)),
            vec(), vec(), vec(),
        ],
        out_specs=pl.BlockSpec((1, TS_LRU, D_LRU), lambda b, t: (b, t, 0)),
        out_shape=jax.ShapeDtypeStruct((bsz, seq, D_LRU), BF16),
        scratch_shapes=[
            pltpu.VMEM((SUBLANES + TS_LRU, D_LRU), F32),
            pltpu.VMEM((TS_LRU, D_LRU), F32),
            pltpu.VMEM((TS_LRU, D_LRU), F32),
            pltpu.VMEM((TS_LRU, D_LRU), F32),
            pltpu.VMEM((1, D_LRU), F32),
        ],
        compiler_params=_params(("parallel", "arbitrary")),
        name="rg_lru_branch",
    )(z3, conv_w, conv_b, w_gate, ba, bx, lam)


def _nt_dot(a, b):
    return lax.dot_general(a, b, (((1,), (1,)), ((), ())), preferred_element_type=F32)


def _moba_kernel(q_ref, k_ref, v_ref, o_ref, kmean_ref, vt_ref, sel_ref, m_ref, l_ref, acc_ref):
    qi = pl.program_id(2)
    nb = vt_ref.shape[0]
    blk = MOBA_BLOCK

    @pl.when(qi == 0)
    def _():
        kf = k_ref[0].astype(F32).reshape(nb, blk, HEAD_DIM)
        kmean = jnp.mean(kf, axis=1)
        hi = kmean.astype(BF16)
        kmean_ref[0:nb, :] = hi
        kmean_ref[nb:2 * nb, :] = (kmean - hi.astype(F32)).astype(BF16)
        for n in range(nb):
            vt_ref[n] = v_ref[0, n * blk:(n + 1) * blk, :].astype(F32).T.astype(BF16)

    q = q_ref[0]

    g2 = _nt_dot(kmean_ref[...], q)
    blk_id = lax.broadcasted_iota(jnp.int32, (nb, blk), 0)
    gate = jnp.where(blk_id < qi, g2[:nb] + g2[nb:], -jnp.inf)
    sel = jnp.zeros((nb, blk), F32)
    for _ in range(MOBA_TOPK):
        top = jnp.max(gate, axis=0, keepdims=True)
        first = jnp.min(jnp.where(gate == top, blk_id, nb), axis=0, keepdims=True)
        pick = (blk_id == first) & (top > -jnp.inf)
        sel = jnp.where(pick, 1.0, sel)
        gate = jnp.where(pick, -jnp.inf, gate)
    sel_ref[...] = sel

    start = pl.multiple_of(qi * blk, blk)
    s = _nt_dot(k_ref[0, pl.ds(start, blk), :], q)
    key_pos = lax.broadcasted_iota(jnp.int32, (blk, blk), 0)
    qry_pos = lax.broadcasted_iota(jnp.int32, (blk, blk), 1)
    s = jnp.where(key_pos <= qry_pos, s, -jnp.inf)
    m = jnp.max(s, axis=0, keepdims=True)
    p = jnp.exp(s - m)
    m_ref[...] = m
    l_ref[...] = jnp.sum(p, axis=0, keepdims=True)
    acc_ref[...] = jnp.dot(vt_ref[qi], p.astype(BF16), preferred_element_type=F32)

    def past(n, carry):
        off = pl.multiple_of(n * blk, blk)
        sn = _nt_dot(k_ref[0, pl.ds(off, blk), :], q)
        sn = jnp.where(sel_ref[pl.ds(n, 1), :] > 0.0, sn, -jnp.inf)
        m_old = m_ref[...]
        m_new = jnp.maximum(m_old, jnp.max(sn, axis=0, keepdims=True))
        alpha = jnp.exp(m_old - m_new)
        pn = jnp.exp(sn - m_new)
        m_ref[...] = m_new
        l_ref[...] = alpha * l_ref[...] + jnp.sum(pn, axis=0, keepdims=True)
        acc_ref[...] = alpha * acc_ref[...] + jnp.dot(vt_ref[n], pn.astype(BF16),
                                                     preferred_element_type=F32)
        return carry

    lax.fori_loop(0, qi, past, 0)
    o_ref[0] = (acc_ref[...] / l_ref[...]).T.astype(BF16)


def _moba(z3):
    bsz, seq, _ = z3.shape
    nb = seq // MOBA_BLOCK
    q_col = D_LRU // HEAD_DIM
    k_col = (D_LRU + D_ATTN) // HEAD_DIM
    v_col = (D_LRU + 2 * D_ATTN) // HEAD_DIM
    return pl.pallas_call(
        _moba_kernel,
        grid=(bsz, N_HEADS, nb),
        in_specs=[
            pl.BlockSpec((1, MOBA_BLOCK, HEAD_DIM), lambda b, h, i: (b, i, q_col + h)),
            pl.BlockSpec((1, seq, HEAD_DIM), lambda b, h, i: (b, 0, k_col + h)),
            pl.BlockSpec((1, seq, HEAD_DIM), lambda b, h, i: (b, 0, v_col + h)),
        ],
        out_specs=pl.BlockSpec((1, MOBA_BLOCK, HEAD_DIM), lambda b, h, i: (b, i, h)),
        out_shape=jax.ShapeDtypeStruct((bsz, seq, D_ATTN), BF16),
        scratch_shapes=[
            pltpu.VMEM((2 * nb, HEAD_DIM), BF16),
            pltpu.VMEM((nb, HEAD_DIM, MOBA_BLOCK), BF16),
            pltpu.VMEM((nb, MOBA_BLOCK), F32),
            pltpu.VMEM((1, MOBA_BLOCK), F32),
            pltpu.VMEM((1, MOBA_BLOCK), F32),
            pltpu.VMEM((HEAD_DIM, MOBA_BLOCK), F32),
        ],
        compiler_params=_params(("parallel", "parallel", "arbitrary")),
        name="moba_attention",
    )(z3, z3, z3)


def _mix_kernel(x_ref, g_ref, h_ref, o_ref, ga_ref, gb_ref, wl_ref, wa_ref, wo_ref,
                out_ref, acc_ref):
    j = pl.program_id(1)

    @pl.when(j == 0)
    def _():
        acc_ref[...] = jnp.zeros_like(acc_ref)

    ya = jnp.dot(h_ref[...], wl_ref[...], preferred_element_type=F32)
    yb = jnp.dot(o_ref[...], wa_ref[...], preferred_element_type=F32)
    mix = ga_ref[...].astype(F32) * ya + gb_ref[...].astype(F32) * yb
    acc_ref[...] += jnp.dot(mix.astype(BF16), wo_ref[...], preferred_element_type=F32)

    @pl.when(j == pl.num_programs(1) - 1)
    def _():
        y = acc_ref[...]
        out_ref[...] = x_ref[...] + y * _rms_scale(y) * g_ref[...]


def _mix(x, g, h, o, z, w_lru_up, w_attn_up, w_out, layer):
    t = x.shape[0]
    ga_col = (D_LRU + 3 * D_ATTN) // TC_MIX
    gb_col = ga_col + D_MODEL // TC_MIX
    return pl.pallas_call(
        _mix_kernel,
        grid=(t // TM, D_MODEL // TC_MIX),
        in_specs=[
            pl.BlockSpec((TM, D_MODEL), lambda i, j: (i, 0)),
            pl.BlockSpec((1, D_MODEL), lambda i, j: (0, 0)),
            pl.BlockSpec((TM, D_LRU), lambda i, j: (i, 0)),
            pl.BlockSpec((TM, D_ATTN), lambda i, j: (i, 0)),
            pl.BlockSpec((TM, TC_MIX), lambda i, j: (i, ga_col + j)),
            pl.BlockSpec((TM, TC_MIX), lambda i, j: (i, gb_col + j)),
            pl.BlockSpec((None, D_LRU, TC_MIX), lambda i, j: (layer, 0, j)),
            pl.BlockSpec((None, D_ATTN, TC_MIX), lambda i, j: (layer, 0, j)),
            pl.BlockSpec((None, TC_MIX, D_MODEL), lambda i, j: (layer, j, 0)),
        ],
        out_specs=pl.BlockSpec((TM, D_MODEL), lambda i, j: (i, 0)),
        out_shape=jax.ShapeDtypeStruct((t, D_MODEL), F32),
        scratch_shapes=[pltpu.VMEM((TM, D_MODEL), F32)],
        compiler_params=_params(("parallel", "arbitrary")),
        name="mixer_merge_out_projection",
    )(x, g, h, o, z, z, w_lru_up, w_attn_up, w_out)


def _rope_tables(seq):
    pos = jnp.arange(seq, dtype=F32)
    inv = ROPE_THETA ** (-jnp.arange(0, ROT_DIM, 2, dtype=F32) / ROT_DIM)
    ang = pos[:, None] * inv[None, :]
    cos, sin = jnp.cos(ang), jnp.sin(ang)
    rest = HEAD_DIM - ROT_DIM
    zeros = jnp.zeros((seq, ROT_HALF), F32)
    tail0 = jnp.zeros((seq, rest), F32)
    cos_t = jnp.concatenate([cos, cos, jnp.ones((seq, rest), F32)], axis=1)
    sin_lo_t = jnp.concatenate([-sin, zeros, tail0], axis=1)
    sin_hi_t = jnp.concatenate([zeros, sin, tail0], axis=1)
    return cos_t, sin_lo_t, sin_hi_t


def _gate_weights(wa, wx):
    def pair_diag(w):
        depth = w.shape[0]
        w = w.reshape(depth, N_GATE_GROUPS, 2, LRU_BW, LRU_BW)
        z = jnp.zeros_like(w[:, :, 0])
        top = jnp.concatenate([w[:, :, 0], z], axis=-1)
        bot = jnp.concatenate([z, w[:, :, 1]], axis=-1)
        return jnp.concatenate([top, bot], axis=-2)
    return jnp.concatenate([pair_diag(wa), pair_diag(wx)], axis=-1).astype(BF16)


def kernel(x, norm_gains, ffn_w13, ffn_w2, w_in, b_gate, conv_w, conv_b, lru_wa, lru_ba,
           lru_wx, lru_bx, lru_lambda, w_lru_up, w_attn_up, w_out):
    bsz, seq, d = x.shape
    depth = norm_gains.shape[0]
    t = bsz * seq
    assert d == D_MODEL and seq % MOBA_BLOCK == 0 and seq % TM == 0 and seq % TS_LRU == 0

    w13 = ffn_w13.astype(BF16)
    w2 = ffn_w2.astype(BF16)
    w_in_b = w_in.astype(BF16)
    w_lru_up_b = w_lru_up.astype(BF16)
    w_attn_up_b = w_attn_up.astype(BF16)
    w_out_b = w_out.astype(BF16)
    w_gate = _gate_weights(lru_wa, lru_wx)
    b_gate3 = b_gate.reshape(depth, 1, 2 * D_MODEL)
    conv_b3 = conv_b.reshape(depth, 1, D_LRU)
    ba3 = lru_ba.reshape(depth, 1, D_LRU)
    bx3 = lru_bx.reshape(depth, 1, D_LRU)
    lam3 = lru_lambda.reshape(depth, 1, D_LRU)
    cos_t, sin_lo_t, sin_hi_t = _rope_tables(seq)

    xs = x.reshape(t, d)
    for l in range(depth):
        gain = lambda k: norm_gains[l, k].reshape(1, d)
        xs = _ffn(xs, gain(0), gain(1), w13, w2, l, 0)
        z = _inproj(xs, gain(2), w_in_b, b_gate3, cos_t, sin_lo_t, sin_hi_t, l, seq)
        z3 = z.reshape(bsz, seq, IN_COLS)
        h = _lru(z3, conv_w, conv_b3, w_gate, ba3, bx3, lam3, l)
        o = _moba(z3)
        xs = _mix(xs, gain(3), h.reshape(t, D_LRU), o.reshape(t, D_ATTN), z,
                  w_lru_up_b, w_attn_up_b, w_out_b, l)
        xs = _ffn(xs, gain(4), gain(5), w13, w2, l, 1)
    return xs.reshape(bsz, seq, d)
```

```python
import functools

import jax
import jax.numpy as jnp
from jax import lax
from jax.experimental import pallas as pl
from jax.experimental.pallas import tpu as pltpu

F32 = jnp.float32
BF16 = jnp.bfloat16

D_MODEL = 2048
D_LRU = 1024
LRU_BLOCKS = 16
LRU_BW = D_LRU // LRU_BLOCKS
CONV_WIDTH = 4
LRU_C = 8.0
N_HEADS = 8
HEAD_DIM = 128
D_ATTN = N_HEADS * HEAD_DIM
MOBA_BLOCK = 256
MOBA_TOPK = 3
ROPE_THETA = 500000.0
ROT_DIM = HEAD_DIM // 4
ROT_HALF = ROT_DIM // 2
D_FF = 5632
NORM_EPS = 1e-6
IN_COLS = D_LRU + 3 * D_ATTN + 2 * D_MODEL

LANES = 128
SUBLANES = 8
VMEM_LIMIT_BYTES = 56 * 1024 * 1024

TM = 512
TF = 512
TN_IN = 1024
TC_MIX = 512
TS_LRU = 512
GATE_GROUP = 2 * LRU_BW
N_GATE_GROUPS = D_LRU // GATE_GROUP


def _rms_scale(x):
    return lax.rsqrt(jnp.mean(x * x, axis=-1, keepdims=True) + NORM_EPS)


def _params(semantics):
    return pltpu.CompilerParams(dimension_semantics=semantics,
                                vmem_limit_bytes=VMEM_LIMIT_BYTES)


def _ffn_kernel(x_ref, gin_ref, gout_ref, wa_ref, wb_ref, w2_ref, o_ref, xn_ref, acc_ref):
    j = pl.program_id(1)

    @pl.when(j == 0)
    def _():
        x = x_ref[...]
        xn_ref[...] = (x * _rms_scale(x) * gin_ref[...]).astype(BF16)
        acc_ref[...] = jnp.zeros_like(acc_ref)

    xn = xn_ref[...]
    a = jnp.dot(xn, wa_ref[...], preferred_element_type=F32)
    b = jnp.dot(xn, wb_ref[...], preferred_element_type=F32)
    h = (a * jax.nn.sigmoid(a) * b).astype(BF16)
    acc_ref[...] += jnp.dot(h, w2_ref[...], preferred_element_type=F32)

    @pl.when(j == pl.num_programs(1) - 1)
    def _():
        y = acc_ref[...]
        o_ref[...] = x_ref[...] + 0.5 * (y * _rms_scale(y) * gout_ref[...])


def _ffn(x, gin, gout, w13, w2, layer, half):
    t = x.shape[0]
    nf = D_FF // TF
    return pl.pallas_call(
        _ffn_kernel,
        grid=(t // TM, nf),
        in_specs=[
            pl.BlockSpec((TM, D_MODEL), lambda i, j: (i, 0)),
            pl.BlockSpec((1, D_MODEL), lambda i, j: (0, 0)),
            pl.BlockSpec((1, D_MODEL), lambda i, j: (0, 0)),
            pl.BlockSpec((None, None, D_MODEL, TF), lambda i, j: (layer, half, 0, j)),
            pl.BlockSpec((None, None, D_MODEL, TF), lambda i, j: (layer, half, 0, j + nf)),
            pl.BlockSpec((None, None, TF, D_MODEL), lambda i, j: (layer, half, j, 0)),
        ],
        out_specs=pl.BlockSpec((TM, D_MODEL), lambda i, j: (i, 0)),
        out_shape=jax.ShapeDtypeStruct((t, D_MODEL), F32),
        scratch_shapes=[pltpu.VMEM((TM, D_MODEL), BF16), pltpu.VMEM((TM, D_MODEL), F32)],
        compiler_params=_params(("parallel", "arbitrary")),
        name="ffn_half_step",
    )(x, gin, gout, w13, w13, w2)


def _rotary(acc, cos_ref, sin_lo_ref, sin_hi_ref):
    cos = cos_ref[...]
    sin_lo = sin_lo_ref[...]
    sin_hi = sin_hi_ref[...]
    heads = []
    for hd in range(N_HEADS):
        xh = acc[:, hd * HEAD_DIM:(hd + 1) * HEAD_DIM]
        up = pltpu.roll(xh, HEAD_DIM - ROT_HALF, 1)
        down = pltpu.roll(xh, ROT_HALF, 1)
        heads.append(xh * cos + up * sin_lo + down * sin_hi)
    return jnp.concatenate(heads, axis=1)


def _inproj_kernel(x_ref, g_ref, w_ref, bg_ref, cos_ref, sin_lo_ref, sin_hi_ref,
                   z_ref, xn_ref):
    j = pl.program_id(1)

    @pl.when(j == 0)
    def _():
        x = x_ref[...]
        xn_ref[...] = (x * _rms_scale(x) * g_ref[...]).astype(BF16)

    acc = jnp.dot(xn_ref[...], w_ref[...], preferred_element_type=F32)
    q_blk = D_LRU // TN_IN
    k_blk = (D_LRU + D_ATTN) // TN_IN
    v_blk = (D_LRU + 2 * D_ATTN) // TN_IN
    gate_blk = (D_LRU + 3 * D_ATTN) // TN_IN

    @pl.when((j < q_blk) | (j == v_blk))
    def _():
        z_ref[...] = acc.astype(BF16)

    @pl.when(j == q_blk)
    def _():
        rot = _rotary(acc, cos_ref, sin_lo_ref, sin_hi_ref)
        z_ref[...] = (rot * (HEAD_DIM ** -0.5)).astype(BF16)

    @pl.when(j == k_blk)
    def _():
        z_ref[...] = _rotary(acc, cos_ref, sin_lo_ref, sin_hi_ref).astype(BF16)

    @pl.when(j >= gate_blk)
    def _():
        z_ref[...] = jax.nn.sigmoid(acc + bg_ref[...]).astype(BF16)


def _inproj(x, g, w_in, b_gate, cos_t, sin_lo_t, sin_hi_t, layer, seq):
    t = x.shape[0]
    gate_blk = (D_LRU + 3 * D_ATTN) // TN_IN
    seq_tiles = seq // TM
    return pl.pallas_call(
        _inproj_kernel,
        grid=(t // TM, IN_COLS // TN_IN),
        in_specs=[
            pl.BlockSpec((TM, D_MODEL), lambda i, j: (i, 0)),
            pl.BlockSpec((1, D_MODEL), lambda i, j: (0, 0)),
            pl.BlockSpec((None, D_MODEL, TN_IN), lambda i, j: (layer, 0, j)),
            pl.BlockSpec((None, 1, TN_IN), lambda i, j: (layer, 0, jnp.maximum(j - gate_blk, 0))),
            pl.BlockSpec((TM, HEAD_DIM), lambda i, j: (i % seq_tiles, 0)),
            pl.BlockSpec((TM, HEAD_DIM), lambda i, j: (i % seq_tiles, 0)),
            pl.BlockSpec((TM, HEAD_DIM), lambda i, j: (i % seq_tiles, 0)),
        ],
        out_specs=pl.BlockSpec((TM, TN_IN), lambda i, j: (i, j)),
        out_shape=jax.ShapeDtypeStruct((t, IN_COLS), BF16),
        scratch_shapes=[pltpu.VMEM((TM, D_MODEL), BF16)],
        compiler_params=_params(("parallel", "arbitrary")),
        name="mixer_in_projection",
    )(x, g, w_in, b_gate, cos_t, sin_lo_t, sin_hi_t)


def _lru_kernel(u_ref, cw_ref, cb_ref, wg_ref, ba_ref, bx_ref, lam_ref, h_ref,
                ubuf_ref, a_ref, b_ref, hout_ref, carry_ref):
    ts = TS_LRU
    pad = SUBLANES

    @pl.when(pl.program_id(1) == 0)
    def _():
        ubuf_ref[0:pad, :] = jnp.zeros((pad, D_LRU), F32)
        carry_ref[...] = jnp.zeros_like(carry_ref)

    ubuf_ref[pad:pad + ts, :] = u_ref[0].astype(F32)
    xc = cb_ref[...]
    for tap in range(CONV_WIDTH):
        off = pad - (CONV_WIDTH - 1) + tap
        xc = xc + ubuf_ref[off:off + ts, :] * cw_ref[tap:tap + 1, :]
    ubuf_ref[0:pad, :] = ubuf_ref[ts:ts + pad, :]

    neg_c_softplus = -LRU_C * jax.nn.softplus(-lam_ref[...])
    for g in range(N_GATE_GROUPS):
        cols = slice(g * GATE_GROUP, (g + 1) * GATE_GROUP)
        xg = xc[:, cols]
        lin = jnp.dot(xg.astype(BF16), wg_ref[g], preferred_element_type=F32)
        r = jax.nn.sigmoid(lin[:, :GATE_GROUP] + ba_ref[:, cols])
        i = jax.nn.sigmoid(lin[:, GATE_GROUP:] + bx_ref[:, cols])
        log_a = r * neg_c_softplus[:, cols]
        a = jnp.exp(log_a)
        a_ref[:, cols] = a
        b_ref[:, cols] = jnp.sqrt(1.0 - a * a) * (i * xg)

    def step(t, h):
        h = a_ref[pl.ds(t, 1), :] * h + b_ref[pl.ds(t, 1), :]
        hout_ref[pl.ds(t, 1), :] = h
        return h

    carry_ref[...] = lax.fori_loop(0, ts, step, carry_ref[...], unroll=8)
    h_ref[0] = hout_ref[...].astype(BF16)


def _lru(z3, conv_w, conv_b, w_gate, ba, bx, lam, layer):
    bsz, seq, _ = z3.shape
    vec = lambda: pl.BlockSpec((None, 1, D_LRU), lambda b, t: (layer, 0, 0))
    return pl.pallas_call(
        _lru_kernel,
        grid=(bsz, seq // TS_LRU),
        in_specs=[
            pl.BlockSpec((1, TS_LRU, D_LRU), lambda b, t: (b, t, 0)),
            pl.BlockSpec((None, CONV_WIDTH, D_LRU), lambda b, t: (layer, 0, 0)),
            vec(),
            pl.BlockSpec((None, N_GATE_GROUPS, GATE_GROUP, 2 * GATE_GROUP),
                         lambda b, t: (layer, 0, 0, 0)),
            vec(), vec(), vec(),
        ],
        out_specs=pl.BlockSpec((1, TS_LRU, D_LRU), lambda b, t: (b, t, 0)),
        out_shape=jax.ShapeDtypeStruct((bsz, seq, D_LRU), BF16),
        scratch_shapes=[
            pltpu.VMEM((SUBLANES + TS_LRU, D_LRU), F32),
            pltpu.VMEM((TS_LRU, D_LRU), F32),
            pltpu.VMEM((TS_LRU, D_LRU), F32),
            pltpu.VMEM((TS_LRU, D_LRU), F32),
            pltpu.VMEM((1, D_LRU), F32),
        ],
        compiler_params=_params(("parallel", "arbitrary")),
        name="rg_lru_branch",
    )(z3, conv_w, conv_b, w_gate, ba, bx, lam)


def _nt_dot(a, b):
    return lax.dot_general(a, b, (((1,), (1,)), ((), ())), preferred_element_type=F32)


def _tn_dot(a, b):
    return lax.dot_general(a, b, (((0,), (0,)), ((), ())), preferred_element_type=F32)


def _moba_prepare(q_ref, k_ref, bias_ref, causal_ref):
    nb, seq = bias_ref.shape
    blk = MOBA_BLOCK
    neg = -jnp.inf
    kmean = jnp.mean(k_ref[0].astype(F32).reshape(nb, blk, HEAD_DIM), axis=1)
    hi = kmean.astype(BF16)
    lo = (kmean - hi.astype(F32)).astype(BF16)
    q_all = q_ref[0]
    gate = _nt_dot(hi, q_all) + _nt_dot(lo, q_all)
    blk_id = lax.broadcasted_iota(jnp.int32, (nb, seq), 0)
    qry_blk = lax.broadcasted_iota(jnp.int32, (nb, seq), 1) // blk
    gate = jnp.where(blk_id < qry_blk, gate, neg)
    bias = jnp.full((nb, seq), neg, F32)
    for _ in range(MOBA_TOPK):
        top = jnp.max(gate, axis=0, keepdims=True)
        first = jnp.min(jnp.where(gate == top, blk_id, nb), axis=0, keepdims=True)
        pick = (blk_id == first) & (top > neg)
        bias = jnp.where(pick, 0.0, bias)
        gate = jnp.where(pick, neg, gate)
    bias_ref[...] = bias
    key_pos = lax.broadcasted_iota(jnp.int32, (blk, blk), 0)
    qry_pos = lax.broadcasted_iota(jnp.int32, (blk, blk), 1)
    causal_ref[...] = jnp.where(key_pos <= qry_pos, 0.0, neg)


def _moba_query_block(i, q_ref, k_ref, v_ref, o_ref, bias_ref, causal_ref):
    blk = MOBA_BLOCK
    lo, hi = i * blk, (i + 1) * blk
    q = q_ref[0, lo:hi, :]
    s_own = _nt_dot(k_ref[0, lo:hi, :], q) + causal_ref[...]
    m = jnp.max(s_own, axis=0, keepdims=True)
    if i > 0:
        s_past = _nt_dot(k_ref[0, 0:lo, :], q).reshape(i, blk, blk)
        s_past = (s_past + bias_ref[0:i, lo:hi][:, None, :]).reshape(lo, blk)
        m = jnp.maximum(m, jnp.max(s_past, axis=0, keepdims=True))
    p_own = jnp.exp(s_own - m)
    l = jnp.sum(p_own, axis=0, keepdims=True)
    acc = _tn_dot(v_ref[0, lo:hi, :], p_own.astype(BF16))
    if i > 0:
        p_past = jnp.exp(s_past - m)
        l = l + jnp.sum(p_past, axis=0, keepdims=True)
        acc = acc + _tn_dot(v_ref[0, 0:lo, :], p_past.astype(BF16))
    o_ref[0] = (acc * (1.0 / l)).T.astype(BF16)


def _moba_kernel(q_ref, k_ref, v_ref, o_ref, bias_ref, causal_ref):
    qi = pl.program_id(2)

    @pl.when(qi == 0)
    def _():
        _moba_prepare(q_ref, k_ref, bias_ref, causal_ref)

    for i in range(bias_ref.shape[0]):
        pl.when(qi == i)(functools.partial(_moba_query_block, i, q_ref, k_ref, v_ref, o_ref,
                                           bias_ref, causal_ref))


def _moba(z3):
    bsz, seq, _ = z3.shape
    nb = seq // MOBA_BLOCK
    q_col = D_LRU // HEAD_DIM
    k_col = (D_LRU + D_ATTN) // HEAD_DIM
    v_col = (D_LRU + 2 * D_ATTN) // HEAD_DIM
    return pl.pallas_call(
        _moba_kernel,
        grid=(bsz, N_HEADS, nb),
        in_specs=[
            pl.BlockSpec((1, seq, HEAD_DIM), lambda b, h, i: (b, 0, q_col + h)),
            pl.BlockSpec((1, seq, HEAD_DIM), lambda b, h, i: (b, 0, k_col + h)),
            pl.BlockSpec((1, seq, HEAD_DIM), lambda b, h, i: (b, 0, v_col + h)),
        ],
        out_specs=pl.BlockSpec((1, MOBA_BLOCK, HEAD_DIM), lambda b, h, i: (b, i, h)),
        out_shape=jax.ShapeDtypeStruct((bsz, seq, D_ATTN), BF16),
        scratch_shapes=[
            pltpu.VMEM((nb, seq), F32),
            pltpu.VMEM((MOBA_BLOCK, MOBA_BLOCK), F32),
        ],
        compiler_params=_params(("parallel", "parallel", "arbitrary")),
        name="moba_attention",
    )(z3, z3, z3)


def _mix_kernel(x_ref, g_ref, h_ref, o_ref, ga_ref, gb_ref, wl_ref, wa_ref, wo_ref,
                out_ref, acc_ref):
    j = pl.program_id(1)

    @pl.when(j == 0)
    def _():
        acc_ref[...] = jnp.zeros_like(acc_ref)

    ya = jnp.dot(h_ref[...], wl_ref[...], preferred_element_type=F32)
    yb = jnp.dot(o_ref[...], wa_ref[...], preferred_element_type=F32)
    mix = ga_ref[...].astype(F32) * ya + gb_ref[...].astype(F32) * yb
    acc_ref[...] += jnp.dot(mix.astype(BF16), wo_ref[...], preferred_element_type=F32)

    @pl.when(j == pl.num_programs(1) - 1)
    def _():
        y = acc_ref[...]
        out_ref[...] = x_ref[...] + y * _rms_scale(y) * g_ref[...]


def _mix(x, g, h, o, z, w_lru_up, w_attn_up, w_out, layer):
    t = x.shape[0]
    ga_col = (D_LRU + 3 * D_ATTN) // TC_MIX
    gb_col = ga_col + D_MODEL // TC_MIX
    return pl.pallas_call(
        _mix_kernel,
        grid=(t // TM, D_MODEL // TC_MIX),
        in_specs=[
            pl.BlockSpec((TM, D_MODEL), lambda i, j: (i, 0)),
            pl.BlockSpec((1, D_MODEL), lambda i, j: (0, 0)),
            pl.BlockSpec((TM, D_LRU), lambda i, j: (i, 0)),
            pl.BlockSpec((TM, D_ATTN), lambda i, j: (i, 0)),
            pl.BlockSpec((TM, TC_MIX), lambda i, j: (i, ga_col + j)),
            pl.BlockSpec((TM, TC_MIX), lambda i, j: (i, gb_col + j)),
            pl.BlockSpec((None, D_LRU, TC_MIX), lambda i, j: (layer, 0, j)),
            pl.BlockSpec((None, D_ATTN, TC_MIX), lambda i, j: (layer, 0, j)),
            pl.BlockSpec((None, TC_MIX, D_MODEL), lambda i, j: (layer, j, 0)),
        ],
        out_specs=pl.BlockSpec((TM, D_MODEL), lambda i, j: (i, 0)),
        out_shape=jax.ShapeDtypeStruct((t, D_MODEL), F32),
        scratch_shapes=[pltpu.VMEM((TM, D_MODEL), F32)],
        compiler_params=_params(("parallel", "arbitrary")),
        name="mixer_merge_out_projection",
    )(x, g, h, o, z, z, w_lru_up, w_attn_up, w_out)


def _rope_tables(seq):
    pos = jnp.arange(seq, dtype=F32)
    inv = ROPE_THETA ** (-jnp.arange(0, ROT_DIM, 2, dtype=F32) / ROT_DIM)
    ang = pos[:, None] * inv[None, :]
    cos, sin = jnp.cos(ang), jnp.sin(ang)
    rest = HEAD_DIM - ROT_DIM
    zeros = jnp.zeros((seq, ROT_HALF), F32)
    tail0 = jnp.zeros((seq, rest), F32)
    cos_t = jnp.concatenate([cos, cos, jnp.ones((seq, rest), F32)], axis=1)
    sin_lo_t = jnp.concatenate([-sin, zeros, tail0], axis=1)
    sin_hi_t = jnp.concatenate([zeros, sin, tail0], axis=1)
    return cos_t, sin_lo_t, sin_hi_t


def _gate_weights(wa, wx):
    def pair_diag(w):
        depth = w.shape[0]
        w = w.reshape(depth, N_GATE_GROUPS, 2, LRU_BW, LRU_BW)
        z = jnp.zeros_like(w[:, :, 0])
        top = jnp.concatenate([w[:, :, 0], z], axis=-1)
        bot = jnp.concatenate([z, w[:, :, 1]], axis=-1)
        return jnp.concatenate([top, bot], axis=-2)
    return jnp.concatenate([pair_diag(wa), pair_diag(wx)], axis=-1).astype(BF16)


def kernel(x, norm_gains, ffn_w13, ffn_w2, w_in, b_gate, conv_w, conv_b, lru_wa, lru_ba,
           lru_wx, lru_bx, lru_lambda, w_lru_up, w_attn_up, w_out):
    bsz, seq, d = x.shape
    depth = norm_gains.shape[0]
    t = bsz * seq
    assert d == D_MODEL and seq % MOBA_BLOCK == 0 and seq % TM == 0 and seq % TS_LRU == 0

    w13 = ffn_w13.astype(BF16)
    w2 = ffn_w2.astype(BF16)
    w_in_b = w_in.astype(BF16)
    w_lru_up_b = w_lru_up.astype(BF16)
    w_attn_up_b = w_attn_up.astype(BF16)
    w_out_b = w_out.astype(BF16)
    w_gate = _gate_weights(lru_wa, lru_wx)
    b_gate3 = b_gate.reshape(depth, 1, 2 * D_MODEL)
    conv_b3 = conv_b.reshape(depth, 1, D_LRU)
    ba3 = lru_ba.reshape(depth, 1, D_LRU)
    bx3 = lru_bx.reshape(depth, 1, D_LRU)
    lam3 = lru_lambda.reshape(depth, 1, D_LRU)
    cos_t, sin_lo_t, sin_hi_t = _rope_tables(seq)

    xs = x.reshape(t, d)
    for l in range(depth):
        gain = lambda k: norm_gains[l, k].reshape(1, d)
        xs = _ffn(xs, gain(0), gain(1), w13, w2, l, 0)
        z = _inproj(xs, gain(2), w_in_b, b_gate3, cos_t, sin_lo_t, sin_hi_t, l, seq)
        z3 = z.reshape(bsz, seq, IN_COLS)
        h = _lru(z3, conv_w, conv_b3, w_gate, ba3, bx3, lam3, l)
        o = _moba(z3)
        xs = _mix(xs, gain(3), h.reshape(t, D_LRU), o.reshape(t, D_ATTN), z,
                  w_lru_up_b, w_attn_up_b, w_out_b, l)
        xs = _ffn(xs, gain(4), gain(5), w13, w2, l, 1)
    return xs.reshape(bsz, seq, d)
```

```python
import functools

import jax
import jax.numpy as jnp
import numpy as np
from jax import lax
from jax.experimental import pallas as pl
from jax.experimental.pallas import tpu as pltpu

F32 = jnp.float32
BF16 = jnp.bfloat16

D_MODEL = 2048
D_LRU = 1024
LRU_BLOCKS = 16
LRU_BW = D_LRU // LRU_BLOCKS
CONV_WIDTH = 4
LRU_C = 8.0
N_HEADS = 8
HEAD_DIM = 128
D_ATTN = N_HEADS * HEAD_DIM
MOBA_BLOCK = 256
MOBA_TOPK = 3
ROPE_THETA = 500000.0
ROT_DIM = HEAD_DIM // 4
ROT_HALF = ROT_DIM // 2
D_FF = 5632
NORM_EPS = 1e-6
IN_COLS = D_LRU + 3 * D_ATTN + 2 * D_MODEL

LANES = 128
SUBLANES = 8
VMEM_LIMIT_BYTES = 56 * 1024 * 1024

TM = 512
TF = 512
TN_IN = 1024
TC_MIX = 512
TS_LRU = 512
GATE_GROUP = 2 * LRU_BW
N_GATE_GROUPS = D_LRU // GATE_GROUP


def _rms_scale(x):
    return lax.rsqrt(jnp.mean(x * x, axis=-1, keepdims=True) + NORM_EPS)


def _params(semantics):
    return pltpu.CompilerParams(dimension_semantics=semantics,
                                vmem_limit_bytes=VMEM_LIMIT_BYTES)


def _ffn_kernel(x_ref, gin_ref, gout_ref, wa_ref, wb_ref, w2_ref, o_ref, xn_ref, acc_ref):
    j = pl.program_id(1)

    @pl.when(j == 0)
    def _():
        x = x_ref[...]
        xn_ref[...] = (x * _rms_scale(x) * gin_ref[...]).astype(BF16)
        acc_ref[...] = jnp.zeros_like(acc_ref)

    xn = xn_ref[...]
    a = jnp.dot(xn, wa_ref[...], preferred_element_type=F32)
    b = jnp.dot(xn, wb_ref[...], preferred_element_type=F32)
    h = (a * jax.nn.sigmoid(a) * b).astype(BF16)
    acc_ref[...] += jnp.dot(h, w2_ref[...], preferred_element_type=F32)

    @pl.when(j == pl.num_programs(1) - 1)
    def _():
        y = acc_ref[...]
        o_ref[...] = x_ref[...] + 0.5 * (y * _rms_scale(y) * gout_ref[...])


def _ffn(x, gin, gout, w13, w2, layer, half):
    t = x.shape[0]
    nf = D_FF // TF
    return pl.pallas_call(
        _ffn_kernel,
        grid=(t // TM, nf),
        in_specs=[
            pl.BlockSpec((TM, D_MODEL), lambda i, j: (i, 0)),
            pl.BlockSpec((1, D_MODEL), lambda i, j: (0, 0)),
            pl.BlockSpec((1, D_MODEL), lambda i, j: (0, 0)),
            pl.BlockSpec((None, None, D_MODEL, TF), lambda i, j: (layer, half, 0, j)),
            pl.BlockSpec((None, None, D_MODEL, TF), lambda i, j: (layer, half, 0, j + nf)),
            pl.BlockSpec((None, None, TF, D_MODEL), lambda i, j: (layer, half, j, 0)),
        ],
        out_specs=pl.BlockSpec((TM, D_MODEL), lambda i, j: (i, 0)),
        out_shape=jax.ShapeDtypeStruct((t, D_MODEL), F32),
        scratch_shapes=[pltpu.VMEM((TM, D_MODEL), BF16), pltpu.VMEM((TM, D_MODEL), F32)],
        compiler_params=_params(("parallel", "arbitrary")),
        name="ffn_half_step",
    )(x, gin, gout, w13, w13, w2)


Q_SCALE = HEAD_DIM ** -0.5 * 1.4426950408889634
HEADS_PER_PARTNER = HEAD_DIM // ROT_DIM
PARTNER_COLS = N_HEADS * ROT_DIM


def _rotary(acc, partner, cos_ref, sin_ref):
    heads = []
    for hd in range(N_HEADS):
        m = hd % HEADS_PER_PARTNER
        c = hd // HEADS_PER_PARTNER
        xh = acc[:, hd * HEAD_DIM:(hd + 1) * HEAD_DIM]
        ph = partner[:, c * HEAD_DIM:(c + 1) * HEAD_DIM]
        lanes = slice(m * HEAD_DIM, (m + 1) * HEAD_DIM)
        heads.append(xh * cos_ref[:, lanes] + ph * sin_ref[:, lanes])
    return jnp.concatenate(heads, axis=1)


def _inproj_kernel(x_ref, g_ref, w_ref, wp_ref, bg_ref, cos_ref, sin_ref, z_ref, xn_ref):
    j = pl.program_id(1)
    q_blk = D_LRU // TN_IN
    k_blk = (D_LRU + D_ATTN) // TN_IN
    v_blk = (D_LRU + 2 * D_ATTN) // TN_IN
    gate_blk = (D_LRU + 3 * D_ATTN) // TN_IN

    @pl.when(j == 0)
    def _():
        x = x_ref[...]
        xn_ref[...] = (x * _rms_scale(x) * g_ref[...]).astype(BF16)

    def project(w):
        return jnp.dot(xn_ref[...], w[...], preferred_element_type=F32)

    @pl.when((j < q_blk) | (j == v_blk))
    def _():
        z_ref[...] = project(w_ref).astype(BF16)

    @pl.when(j == q_blk)
    def _():
        rot = _rotary(project(w_ref), project(wp_ref), cos_ref, sin_ref)
        z_ref[...] = (rot * Q_SCALE).astype(BF16)

    @pl.when(j == k_blk)
    def _():
        z_ref[...] = _rotary(project(w_ref), project(wp_ref), cos_ref, sin_ref).astype(BF16)

    @pl.when(j >= gate_blk)
    def _():
        z_ref[...] = jax.nn.sigmoid(project(w_ref) + bg_ref[...]).astype(BF16)


def _inproj(x, g, w_in_ext, b_gate, cos_t, sin_t, layer, seq):
    t = x.shape[0]
    k_blk = (D_LRU + D_ATTN) // TN_IN
    gate_blk = (D_LRU + 3 * D_ATTN) // TN_IN
    partner_blk = IN_COLS // PARTNER_COLS
    seq_tiles = seq // TM
    table = lambda: pl.BlockSpec((TM, HEADS_PER_PARTNER * HEAD_DIM),
                                 lambda i, j: (i % seq_tiles, 0))
    return pl.pallas_call(
        _inproj_kernel,
        grid=(t // TM, IN_COLS // TN_IN),
        in_specs=[
            pl.BlockSpec((TM, D_MODEL), lambda i, j: (i, 0)),
            pl.BlockSpec((1, D_MODEL), lambda i, j: (0, 0)),
            pl.BlockSpec((None, D_MODEL, TN_IN), lambda i, j: (layer, 0, j)),
            pl.BlockSpec((None, D_MODEL, PARTNER_COLS),
                         lambda i, j: (layer, 0, partner_blk + (j == k_blk).astype(jnp.int32))),
            pl.BlockSpec((None, 1, TN_IN), lambda i, j: (layer, 0, jnp.maximum(j - gate_blk, 0))),
            table(), table(),
        ],
        out_specs=pl.BlockSpec((TM, TN_IN), lambda i, j: (i, j)),
        out_shape=jax.ShapeDtypeStruct((t, IN_COLS), BF16),
        scratch_shapes=[pltpu.VMEM((TM, D_MODEL), BF16)],
        compiler_params=_params(("parallel", "arbitrary")),
        name="mixer_in_projection",
    )(x, g, w_in_ext, w_in_ext, b_gate, cos_t, sin_t)


def _lru_kernel(u_ref, cw_ref, cb_ref, wg_ref, ba_ref, bx_ref, lam_ref, h_ref,
                ubuf_ref, a_ref, b_ref, hout_ref, carry_ref):
    ts = TS_LRU
    pad = SUBLANES

    @pl.when(pl.program_id(1) == 0)
    def _():
        ubuf_ref[0:pad, :] = jnp.zeros((pad, D_LRU), F32)
        carry_ref[...] = jnp.zeros_like(carry_ref)

    ubuf_ref[pad:pad + ts, :] = u_ref[0].astype(F32)
    xc = cb_ref[...]
    for tap in range(CONV_WIDTH):
        off = pad - (CONV_WIDTH - 1) + tap
        xc = xc + ubuf_ref[off:off + ts, :] * cw_ref[tap:tap + 1, :]
    ubuf_ref[0:pad, :] = ubuf_ref[ts:ts + pad, :]

    neg_c_softplus = -LRU_C * jax.nn.softplus(-lam_ref[...])
    for g in range(N_GATE_GROUPS):
        cols = slice(g * GATE_GROUP, (g + 1) * GATE_GROUP)
        xg = xc[:, cols]
        lin = jnp.dot(xg.astype(BF16), wg_ref[g], preferred_element_type=F32)
        r = jax.nn.sigmoid(lin[:, :GATE_GROUP] + ba_ref[:, cols])
        i = jax.nn.sigmoid(lin[:, GATE_GROUP:] + bx_ref[:, cols])
        log_a = r * neg_c_softplus[:, cols]
        a = jnp.exp(log_a)
        a_ref[:, cols] = a
        b_ref[:, cols] = jnp.sqrt(1.0 - a * a) * (i * xg)

    def step(t, h):
        h = a_ref[pl.ds(t, 1), :] * h + b_ref[pl.ds(t, 1), :]
        hout_ref[pl.ds(t, 1), :] = h
        return h

    carry_ref[...] = lax.fori_loop(0, ts, step, carry_ref[...], unroll=8)
    h_ref[0] = hout_ref[...].astype(BF16)


def _lru(z3, conv_w, conv_b, w_gate, ba, bx, lam, layer):
    bsz, seq, _ = z3.shape
    vec = lambda: pl.BlockSpec((None, 1, D_LRU), lambda b, t: (layer, 0, 0))
    return pl.pallas_call(
        _lru_kernel,
        grid=(bsz, seq // TS_LRU),
        in_specs=[
            pl.BlockSpec((1, TS_LRU, D_LRU), lambda b, t: (b, t, 0)),
            pl.BlockSpec((None, CONV_WIDTH, D_LRU), lambda b, t: (layer, 0, 0)),
            vec(),
            pl.BlockSpec((None, N_GATE_GROUPS, GATE_GROUP, 2 * GATE_GROUP),
                         lambda b, t: (layer, 0, 0, 0)),
            vec(), vec(), vec(),
        ],
        out_specs=pl.BlockSpec((1, TS_LRU, D_LRU), lambda b, t: (b, t, 0)),
        out_shape=jax.ShapeDtypeStruct((bsz, seq, D_LRU), BF16),
        scratch_shapes=[
            pltpu.VMEM((SUBLANES + TS_LRU, D_LRU), F32),
            pltpu.VMEM((TS_LRU, D_LRU), F32),
            pltpu.VMEM((TS_LRU, D_LRU), F32),
            pltpu.VMEM((TS_LRU, D_LRU), F32),
            pltpu.VMEM((1, D_LRU), F32),
        ],
        compiler_params=_params(("parallel", "arbitrary")),
        name="rg_lru_branch",
    )(z3, conv_w, conv_b, w_gate, ba, bx, lam)


def _nt_dot(a, b):
    return lax.dot_general(a, b, (((1,), (1,)), ((), ())), preferred_element_type=F32)


def _moba_prepare(q_ref, k_ref, v_ref, bias_ref, causal_ref, vt_ref):
    nb, seq = bias_ref.shape
    blk = MOBA_BLOCK
    neg = -jnp.inf
    kmean = jnp.mean(k_ref[0].astype(F32).reshape(nb, blk, HEAD_DIM), axis=1)
    hi = kmean.astype(BF16)
    lo = (kmean - hi.astype(F32)).astype(BF16)
    q_all = q_ref[0]
    gate = _nt_dot(hi, q_all) + _nt_dot(lo, q_all)
    blk_id = lax.broadcasted_iota(jnp.int32, (nb, seq), 0)
    qry_blk = lax.broadcasted_iota(jnp.int32, (nb, seq), 1) // blk
    gate = jnp.where(blk_id < qry_blk, gate, neg)
    bias = jnp.full((nb, seq), neg, F32)
    for _ in range(MOBA_TOPK):
        top = jnp.max(gate, axis=0, keepdims=True)
        first = jnp.min(jnp.where(gate == top, blk_id, nb), axis=0, keepdims=True)
        pick = (blk_id == first) & (top > neg)
        bias = jnp.where(pick, 0.0, bias)
        gate = jnp.where(pick, neg, gate)
    bias_ref[...] = bias
    key_pos = lax.broadcasted_iota(jnp.int32, (blk, blk), 0)
    qry_pos = lax.broadcasted_iota(jnp.int32, (blk, blk), 1)
    causal_ref[...] = jnp.where(key_pos <= qry_pos, 0.0, neg)
    for n in range(nb):
        cols = slice(n * blk, (n + 1) * blk)
        vt_ref[0:HEAD_DIM, cols] = v_ref[0, cols, :].astype(F32).T.astype(BF16)
    vt_ref[HEAD_DIM:, :] = jnp.ones((vt_ref.shape[0] - HEAD_DIM, seq), BF16)


def _moba_scores(j, slot, q_ref, k_ref, bias_ref, causal_ref, s_ref, m_ref, shift_ref):
    blk = MOBA_BLOCK
    lo, hi = j * blk, (j + 1) * blk
    q = q_ref[0, lo:hi, :]
    s_own = _nt_dot(k_ref[0, lo:hi, :], q) + causal_ref[...]
    s_ref[slot, lo:hi, :] = s_own
    m = jnp.max(s_own, axis=0, keepdims=True)
    if j > 0:
        s_past = _nt_dot(k_ref[0, 0:lo, :], q)
        s_ref[slot, 0:lo, :] = s_past
        bias = bias_ref[0:j, lo:hi]
        blk_max = jnp.max(s_past.reshape(j, blk, blk), axis=1)
        m = jnp.maximum(m, jnp.max(blk_max + bias, axis=0, keepdims=True))
        shift_ref[slot, 0:j, :] = m - bias
    m_ref[slot] = m


def _moba_output(i, slot, o_ref, vt_ref, s_ref, m_ref, shift_ref):
    blk = MOBA_BLOCK
    lo, hi = i * blk, (i + 1) * blk
    p_own = jnp.exp2(s_ref[slot, lo:hi, :] - m_ref[slot])
    acc = jnp.dot(vt_ref[:, lo:hi], p_own.astype(BF16), preferred_element_type=F32)
    if i > 0:
        s_past = s_ref[slot, 0:lo, :].reshape(i, blk, blk)
        p_past = jnp.exp2(s_past - shift_ref[slot, 0:i, :][:, None, :]).reshape(lo, blk)
        acc = acc + jnp.dot(vt_ref[:, 0:lo], p_past.astype(BF16), preferred_element_type=F32)
    inv_l = 1.0 / acc[HEAD_DIM:HEAD_DIM + 1, :]
    o_ref[0] = (acc[0:HEAD_DIM, :] * inv_l).T.astype(BF16)


def _moba_kernel(q_ref, k_ref, v_ref, o_ref, bias_ref, causal_ref, vt_ref, s_ref, m_ref,
                 shift_ref):
    qi = pl.program_id(2)
    nb = bias_ref.shape[0]

    @pl.when(qi == 0)
    def _():
        _moba_prepare(q_ref, k_ref, v_ref, bias_ref, causal_ref, vt_ref)
        _moba_scores(0, 0, q_ref, k_ref, bias_ref, causal_ref, s_ref, m_ref, shift_ref)

    for i in range(nb):
        @pl.when(qi == i)
        def _(i=i):
            if i + 1 < nb:
                _moba_scores(i + 1, (i + 1) % 2, q_ref, k_ref, bias_ref, causal_ref,
                             s_ref, m_ref, shift_ref)
            _moba_output(i, i % 2, o_ref, vt_ref, s_ref, m_ref, shift_ref)


def _moba(z3):
    bsz, seq, _ = z3.shape
    nb = seq // MOBA_BLOCK
    q_col = D_LRU // HEAD_DIM
    k_col = (D_LRU + D_ATTN) // HEAD_DIM
    v_col = (D_LRU + 2 * D_ATTN) // HEAD_DIM
    ones_rows = 2 * SUBLANES
    return pl.pallas_call(
        _moba_kernel,
        grid=(bsz, N_HEADS, nb),
        in_specs=[
            pl.BlockSpec((1, seq, HEAD_DIM), lambda b, h, i: (b, 0, q_col + h)),
            pl.BlockSpec((1, seq, HEAD_DIM), lambda b, h, i: (b, 0, k_col + h)),
            pl.BlockSpec((1, seq, HEAD_DIM), lambda b, h, i: (b, 0, v_col + h)),
        ],
        out_specs=pl.BlockSpec((1, MOBA_BLOCK, HEAD_DIM), lambda b, h, i: (b, i, h)),
        out_shape=jax.ShapeDtypeStruct((bsz, seq, D_ATTN), BF16),
        scratch_shapes=[
            pltpu.VMEM((nb, seq), F32),
            pltpu.VMEM((MOBA_BLOCK, MOBA_BLOCK), F32),
            pltpu.VMEM((HEAD_DIM + ones_rows, seq), BF16),
            pltpu.VMEM((2, seq, MOBA_BLOCK), F32),
            pltpu.VMEM((2, 1, MOBA_BLOCK), F32),
            pltpu.VMEM((2, nb, MOBA_BLOCK), F32),
        ],
        compiler_params=_params(("parallel", "parallel", "arbitrary")),
        name="moba_attention",
    )(z3, z3, z3)


def _mix_kernel(x_ref, g_ref, h_ref, o_ref, ga_ref, gb_ref, wl_ref, wa_ref, wo_ref,
                out_ref, acc_ref):
    j = pl.program_id(1)

    @pl.when(j == 0)
    def _():
        acc_ref[...] = jnp.zeros_like(acc_ref)

    ya = jnp.dot(h_ref[...], wl_ref[...], preferred_element_type=F32)
    yb = jnp.dot(o_ref[...], wa_ref[...], preferred_element_type=F32)
    mix = ga_ref[...].astype(F32) * ya + gb_ref[...].astype(F32) * yb
    acc_ref[...] += jnp.dot(mix.astype(BF16), wo_ref[...], preferred_element_type=F32)

    @pl.when(j == pl.num_programs(1) - 1)
    def _():
        y = acc_ref[...]
        out_ref[...] = x_ref[...] + y * _rms_scale(y) * g_ref[...]


def _mix(x, g, h, o, z, w_lru_up, w_attn_up, w_out, layer):
    t = x.shape[0]
    ga_col = (D_LRU + 3 * D_ATTN) // TC_MIX
    gb_col = ga_col + D_MODEL // TC_MIX
    return pl.pallas_call(
        _mix_kernel,
        grid=(t // TM, D_MODEL // TC_MIX),
        in_specs=[
            pl.BlockSpec((TM, D_MODEL), lambda i, j: (i, 0)),
            pl.BlockSpec((1, D_MODEL), lambda i, j: (0, 0)),
            pl.BlockSpec((TM, D_LRU), lambda i, j: (i, 0)),
            pl.BlockSpec((TM, D_ATTN), lambda i, j: (i, 0)),
            pl.BlockSpec((TM, TC_MIX), lambda i, j: (i, ga_col + j)),
            pl.BlockSpec((TM, TC_MIX), lambda i, j: (i, gb_col + j)),
            pl.BlockSpec((None, D_LRU, TC_MIX), lambda i, j: (layer, 0, j)),
            pl.BlockSpec((None, D_ATTN, TC_MIX), lambda i, j: (layer, 0, j)),
            pl.BlockSpec((None, TC_MIX, D_MODEL), lambda i, j: (layer, j, 0)),
        ],
        out_specs=pl.BlockSpec((TM, D_MODEL), lambda i, j: (i, 0)),
        out_shape=jax.ShapeDtypeStruct((t, D_MODEL), F32),
        scratch_shapes=[pltpu.VMEM((TM, D_MODEL), F32)],
        compiler_params=_params(("parallel", "arbitrary")),
        name="mixer_merge_out_projection",
    )(x, g, h, o, z, z, w_lru_up, w_attn_up, w_out)


def _rope_tables(seq):
    pos = jnp.arange(seq, dtype=F32)
    inv = ROPE_THETA ** (-jnp.arange(0, ROT_DIM, 2, dtype=F32) / ROT_DIM)
    ang = pos[:, None] * inv[None, :]
    cos2 = jnp.tile(jnp.cos(ang), (1, 2))
    sin2 = jnp.tile(jnp.sin(ang), (1, 2))
    cos_groups, sin_groups = [], []
    for m in range(HEADS_PER_PARTNER):
        before = m * ROT_DIM
        after = HEAD_DIM - before - ROT_DIM
        cos_groups += [jnp.ones((seq, before), F32), cos2, jnp.ones((seq, after), F32)]
        sin_groups += [jnp.zeros((seq, before), F32), sin2, jnp.zeros((seq, after), F32)]
    return jnp.concatenate(cos_groups, axis=1), jnp.concatenate(sin_groups, axis=1)


def _inproj_columns():
    cols = np.arange(IN_COLS)
    partner_cols, partner_sign = [], []
    for base in (D_LRU, D_LRU + D_ATTN):
        part = np.zeros(PARTNER_COLS, np.int64)
        sign = np.zeros(PARTNER_COLS, np.float32)
        for hd in range(N_HEADS):
            m = hd % HEADS_PER_PARTNER
            head = base + hd * HEAD_DIM
            perm = np.arange(HEAD_DIM)
            perm[:ROT_DIM], perm[m * ROT_DIM:(m + 1) * ROT_DIM] = (
                np.arange(m * ROT_DIM, (m + 1) * ROT_DIM), np.arange(ROT_DIM))
            cols[head:head + HEAD_DIM] = head + perm
            lane0 = (hd // HEADS_PER_PARTNER) * HEAD_DIM + m * ROT_DIM
            r = np.arange(ROT_DIM)
            part[lane0:lane0 + ROT_DIM] = head + np.where(r < ROT_HALF, r + ROT_HALF, r - ROT_HALF)
            sign[lane0:lane0 + ROT_DIM] = np.where(r < ROT_HALF, -1.0, 1.0)
        partner_cols.append(part)
        partner_sign.append(sign)
    all_cols = np.concatenate([cols] + partner_cols).astype(np.int32)
    all_sign = np.concatenate([np.ones(IN_COLS, np.float32)] + partner_sign)
    return all_cols, all_sign


def _gate_weights(wa, wx):
    def pair_diag(w):
        depth = w.shape[0]
        w = w.reshape(depth, N_GATE_GROUPS, 2, LRU_BW, LRU_BW)
        z = jnp.zeros_like(w[:, :, 0])
        top = jnp.concatenate([w[:, :, 0], z], axis=-1)
        bot = jnp.concatenate([z, w[:, :, 1]], axis=-1)
        return jnp.concatenate([top, bot], axis=-2)
    return jnp.concatenate([pair_diag(wa), pair_diag(wx)], axis=-1).astype(BF16)


def kernel(x, norm_gains, ffn_w13, ffn_w2, w_in, b_gate, conv_w, conv_b, lru_wa, lru_ba,
           lru_wx, lru_bx, lru_lambda, w_lru_up, w_attn_up, w_out):
    bsz, seq, d = x.shape
    depth = norm_gains.shape[0]
    t = bsz * seq
    assert d == D_MODEL and seq % MOBA_BLOCK == 0 and seq % TM == 0 and seq % TS_LRU == 0

    w13 = ffn_w13.astype(BF16)
    w2 = ffn_w2.astype(BF16)
    in_cols, in_sign = _inproj_columns()
    w_in_ext = (jnp.take(w_in, in_cols, axis=2) * in_sign).astype(BF16)
    w_lru_up_b = w_lru_up.astype(BF16)
    w_attn_up_b = w_attn_up.astype(BF16)
    w_out_b = w_out.astype(BF16)
    w_gate = _gate_weights(lru_wa, lru_wx)
    b_gate3 = b_gate.reshape(depth, 1, 2 * D_MODEL)
    conv_b3 = conv_b.reshape(depth, 1, D_LRU)
    ba3 = lru_ba.reshape(depth, 1, D_LRU)
    bx3 = lru_bx.reshape(depth, 1, D_LRU)
    lam3 = lru_lambda.reshape(depth, 1, D_LRU)
    cos_t, sin_t = _rope_tables(seq)

    xs = x.reshape(t, d)
    for l in range(depth):
        gain = lambda k: norm_gains[l, k].reshape(1, d)
        xs = _ffn(xs, gain(0), gain(1), w13, w2, l, 0)
        z = _inproj(xs, gain(2), w_in_ext, b_gate3, cos_t, sin_t, l, seq)
        z3 = z.reshape(bsz, seq, IN_COLS)
        h = _lru(z3, conv_w, conv_b3, w_gate, ba3, bx3, lam3, l)
        o = _moba(z3)
        xs = _mix(xs, gain(3), h.reshape(t, D_LRU), o.reshape(t, D_ATTN), z,
                  w_lru_up_b, w_attn_up_b, w_out_b, l)
        xs = _ffn(xs, gain(4), gain(5), w13, w2, l, 1)
    return xs.reshape(bsz, seq, d)
```

```python
import functools

import jax
import jax.numpy as jnp
from jax import lax
from jax.experimental import pallas as pl
from jax.experimental.pallas import tpu as pltpu

F32 = jnp.float32
BF16 = jnp.bfloat16

D_MODEL = 2048
D_LRU = 1024
LRU_BLOCKS = 16
LRU_BW = D_LRU // LRU_BLOCKS
CONV_WIDTH = 4
LRU_C = 8.0
N_HEADS = 8
HEAD_DIM = 128
D_ATTN = N_HEADS * HEAD_DIM
MOBA_BLOCK = 256
MOBA_TOPK = 3
ROPE_THETA = 500000.0
ROT_DIM = HEAD_DIM // 4
ROT_HALF = ROT_DIM // 2
D_FF = 5632
NORM_EPS = 1e-6
IN_COLS = D_LRU + 3 * D_ATTN + 2 * D_MODEL

LANES = 128
SUBLANES = 8
VMEM_LIMIT_BYTES = 56 * 1024 * 1024

TM = 512
TF = 512
TN_IN = 1024
TC_MIX = 512
TS_LRU = 512
MOBA_Q_BLOCKS_PER_STEP = 4
GATE_GROUP = 2 * LRU_BW
N_GATE_GROUPS = D_LRU // GATE_GROUP


def _rms_scale(x):
    return lax.rsqrt(jnp.mean(x * x, axis=-1, keepdims=True) + NORM_EPS)


def _params(semantics):
    return pltpu.CompilerParams(dimension_semantics=semantics,
                                vmem_limit_bytes=VMEM_LIMIT_BYTES)


def _ffn_kernel(x_ref, gin_ref, gout_ref, wa_ref, wb_ref, w2_ref, o_ref, xn_ref, acc_ref):
    j = pl.program_id(1)

    @pl.when(j == 0)
    def _():
        x = x_ref[...]
        xn_ref[...] = (x * _rms_scale(x) * gin_ref[...]).astype(BF16)
        acc_ref[...] = jnp.zeros_like(acc_ref)

    xn = xn_ref[...]
    a = jnp.dot(xn, wa_ref[...], preferred_element_type=F32)
    b = jnp.dot(xn, wb_ref[...], preferred_element_type=F32)
    h = (a * jax.nn.sigmoid(a) * b).astype(BF16)
    acc_ref[...] += jnp.dot(h, w2_ref[...], preferred_element_type=F32)

    @pl.when(j == pl.num_programs(1) - 1)
    def _():
        y = acc_ref[...]
        o_ref[...] = x_ref[...] + 0.5 * (y * _rms_scale(y) * gout_ref[...])


def _ffn(x, gin, gout, w13, w2, layer, half):
    t = x.shape[0]
    nf = D_FF // TF
    return pl.pallas_call(
        _ffn_kernel,
        grid=(t // TM, nf),
        in_specs=[
            pl.BlockSpec((TM, D_MODEL), lambda i, j: (i, 0)),
            pl.BlockSpec((1, D_MODEL), lambda i, j: (0, 0)),
            pl.BlockSpec((1, D_MODEL), lambda i, j: (0, 0)),
            pl.BlockSpec((None, None, D_MODEL, TF), lambda i, j: (layer, half, 0, j)),
            pl.BlockSpec((None, None, D_MODEL, TF), lambda i, j: (layer, half, 0, j + nf)),
            pl.BlockSpec((None, None, TF, D_MODEL), lambda i, j: (layer, half, j, 0)),
        ],
        out_specs=pl.BlockSpec((TM, D_MODEL), lambda i, j: (i, 0)),
        out_shape=jax.ShapeDtypeStruct((t, D_MODEL), F32),
        scratch_shapes=[pltpu.VMEM((TM, D_MODEL), BF16), pltpu.VMEM((TM, D_MODEL), F32)],
        compiler_params=_params(("parallel", "arbitrary")),
        name="ffn_half_step",
    )(x, gin, gout, w13, w13, w2)


Q_SCALE = HEAD_DIM ** -0.5 * 1.4426950408889634
HEADS_PER_PARTNER = HEAD_DIM // ROT_DIM
PARTNER_COLS = N_HEADS * ROT_DIM


def _rotary(acc, partner, cos_ref, sin_ref):
    heads = []
    for hd in range(N_HEADS):
        m = hd % HEADS_PER_PARTNER
        c = hd // HEADS_PER_PARTNER
        xh = acc[:, hd * HEAD_DIM:(hd + 1) * HEAD_DIM]
        ph = partner[:, c * HEAD_DIM:(c + 1) * HEAD_DIM]
        lanes = slice(m * HEAD_DIM, (m + 1) * HEAD_DIM)
        heads.append(xh * cos_ref[:, lanes] + ph * sin_ref[:, lanes])
    return jnp.concatenate(heads, axis=1)


def _inproj_kernel(x_ref, g_ref, w_ref, wqk_ref, wp_ref, bg_ref, cos_ref, sin_ref, z_ref,
                   xn_ref):
    j = pl.program_id(1)
    q_blk = D_LRU // TN_IN
    k_blk = (D_LRU + D_ATTN) // TN_IN
    v_blk = (D_LRU + 2 * D_ATTN) // TN_IN
    gate_blk = (D_LRU + 3 * D_ATTN) // TN_IN

    @pl.when(j == 0)
    def _():
        x = x_ref[...]
        xn_ref[...] = (x * _rms_scale(x) * g_ref[...]).astype(BF16)

    def project(w):
        return jnp.dot(xn_ref[...], w[...], preferred_element_type=F32)

    @pl.when((j < q_blk) | (j == v_blk))
    def _():
        z_ref[...] = project(w_ref).astype(BF16)

    @pl.when(j == q_blk)
    def _():
        rot = _rotary(project(wqk_ref), project(wp_ref), cos_ref, sin_ref)
        z_ref[...] = (rot * Q_SCALE).astype(BF16)

    @pl.when(j == k_blk)
    def _():
        z_ref[...] = _rotary(project(wqk_ref), project(wp_ref), cos_ref, sin_ref).astype(BF16)

    @pl.when(j >= gate_blk)
    def _():
        z_ref[...] = jax.nn.sigmoid(project(w_ref) + bg_ref[...]).astype(BF16)


def _inproj(x, g, w_in, w_qk, b_gate, cos_t, sin_t, layer, seq):
    t = x.shape[0]
    q_blk = D_LRU // TN_IN
    k_blk = (D_LRU + D_ATTN) // TN_IN
    gate_blk = (D_LRU + 3 * D_ATTN) // TN_IN
    partner_blk = 2 * D_ATTN // PARTNER_COLS
    seq_tiles = seq // TM
    is_k = lambda j: (j == k_blk).astype(jnp.int32)
    main_blk = lambda j: jnp.where((j == q_blk) | (j == k_blk), q_blk - 1, j)
    table = lambda: pl.BlockSpec((TM, HEADS_PER_PARTNER * HEAD_DIM),
                                 lambda i, j: (i % seq_tiles, 0))
    return pl.pallas_call(
        _inproj_kernel,
        grid=(t // TM, IN_COLS // TN_IN),
        in_specs=[
            pl.BlockSpec((TM, D_MODEL), lambda i, j: (i, 0)),
            pl.BlockSpec((1, D_MODEL), lambda i, j: (0, 0)),
            pl.BlockSpec((None, D_MODEL, TN_IN), lambda i, j: (layer, 0, main_blk(j))),
            pl.BlockSpec((None, D_MODEL, D_ATTN), lambda i, j: (layer, 0, is_k(j))),
            pl.BlockSpec((None, D_MODEL, PARTNER_COLS),
                         lambda i, j: (layer, 0, partner_blk + is_k(j))),
            pl.BlockSpec((None, 1, TN_IN), lambda i, j: (layer, 0, jnp.maximum(j - gate_blk, 0))),
            table(), table(),
        ],
        out_specs=pl.BlockSpec((TM, TN_IN), lambda i, j: (i, j)),
        out_shape=jax.ShapeDtypeStruct((t, IN_COLS), BF16),
        scratch_shapes=[pltpu.VMEM((TM, D_MODEL), BF16)],
        compiler_params=_params(("parallel", "arbitrary")),
        name="mixer_in_projection",
    )(x, g, w_in, w_qk, w_qk, b_gate, cos_t, sin_t)


def _lru_kernel(u_ref, cw_ref, cb_ref, wg_ref, ba_ref, bx_ref, lam_ref, h_ref,
                ubuf_ref, a_ref, b_ref, hout_ref, carry_ref):
    ts = TS_LRU
    pad = SUBLANES

    @pl.when(pl.program_id(1) == 0)
    def _():
        ubuf_ref[0:pad, :] = jnp.zeros((pad, D_LRU), F32)
        carry_ref[...] = jnp.zeros_like(carry_ref)

    ubuf_ref[pad:pad + ts, :] = u_ref[0].astype(F32)
    xc = cb_ref[...]
    for tap in range(CONV_WIDTH):
        off = pad - (CONV_WIDTH - 1) + tap
        xc = xc + ubuf_ref[off:off + ts, :] * cw_ref[tap:tap + 1, :]
    ubuf_ref[0:pad, :] = ubuf_ref[ts:ts + pad, :]

    neg_c_softplus = -LRU_C * jax.nn.softplus(-lam_ref[...])
    for g in range(N_GATE_GROUPS):
        cols = slice(g * GATE_GROUP, (g + 1) * GATE_GROUP)
        xg = xc[:, cols]
        lin = jnp.dot(xg.astype(BF16), wg_ref[g], preferred_element_type=F32)
        r = jax.nn.sigmoid(lin[:, :GATE_GROUP] + ba_ref[:, cols])
        i = jax.nn.sigmoid(lin[:, GATE_GROUP:] + bx_ref[:, cols])
        log_a = r * neg_c_softplus[:, cols]
        a = jnp.exp(log_a)
        a_ref[:, cols] = a
        b_ref[:, cols] = jnp.sqrt(1.0 - a * a) * (i * xg)

    def step(t, h):
        h = a_ref[pl.ds(t, 1), :] * h + b_ref[pl.ds(t, 1), :]
        hout_ref[pl.ds(t, 1), :] = h
        return h

    carry_ref[...] = lax.fori_loop(0, ts, step, carry_ref[...], unroll=8)
    h_ref[0] = hout_ref[...].astype(BF16)


def _lru(z3, conv_w, conv_b, w_gate, ba, bx, lam, layer):
    bsz, seq, _ = z3.shape
    vec = lambda: pl.BlockSpec((None, 1, D_LRU), lambda b, t: (layer, 0, 0))
    return pl.pallas_call(
        _lru_kernel,
        grid=(bsz, seq // TS_LRU),
        in_specs=[
            pl.BlockSpec((1, TS_LRU, D_LRU), lambda b, t: (b, t, 0)),
            pl.BlockSpec((None, CONV_WIDTH, D_LRU), lambda b, t: (layer, 0, 0)),
            vec(),
            pl.BlockSpec((None, N_GATE_GROUPS, GATE_GROUP, 2 * GATE_GROUP),
                         lambda b, t: (layer, 0, 0, 0)),
            vec(), vec(), vec(),
        ],
        out_specs=pl.BlockSpec((1, TS_LRU, D_LRU), lambda b, t: (b, t, 0)),
        out_shape=jax.ShapeDtypeStruct((bsz, seq, D_LRU), BF16),
        scratch_shapes=[
            pltpu.VMEM((SUBLANES + TS_LRU, D_LRU), F32),
            pltpu.VMEM((TS_LRU, D_LRU), F32),
            pltpu.VMEM((TS_LRU, D_LRU), F32),
            pltpu.VMEM((TS_LRU, D_LRU), F32),
            pltpu.VMEM((1, D_LRU), F32),
        ],
        compiler_params=_params(("parallel", "arbitrary")),
        name="rg_lru_branch",
    )(z3, conv_w, conv_b, w_gate, ba, bx, lam)


def _nt_dot(a, b):
    return lax.dot_general(a, b, (((1,), (1,)), ((), ())), preferred_element_type=F32)


def _moba_prepare(q_ref, k_ref, v_ref, bias_ref, causal_ref, vt_ref):
    nb, seq = bias_ref.shape
    blk = MOBA_BLOCK
    neg = -jnp.inf
    kmean = jnp.mean(k_ref[0].astype(F32).reshape(nb, blk, HEAD_DIM), axis=1)
    hi = kmean.astype(BF16)
    lo = (kmean - hi.astype(F32)).astype(BF16)
    q_all = q_ref[0]
    gate = _nt_dot(hi, q_all) + _nt_dot(lo, q_all)
    blk_id = lax.broadcasted_iota(jnp.int32, (nb, seq), 0)
    qry_blk = lax.broadcasted_iota(jnp.int32, (nb, seq), 1) // blk
    gate = jnp.where(blk_id < qry_blk, gate, neg)
    bias = jnp.full((nb, seq), neg, F32)
    for _ in range(MOBA_TOPK):
        top = jnp.max(gate, axis=0, keepdims=True)
        first = jnp.min(jnp.where(gate == top, blk_id, nb), axis=0, keepdims=True)
        pick = (blk_id == first) & (top > neg)
        bias = jnp.where(pick, 0.0, bias)
        gate = jnp.where(pick, neg, gate)
    bias_ref[...] = bias
    key_pos = lax.broadcasted_iota(jnp.int32, (blk, blk), 0)
    qry_pos = lax.broadcasted_iota(jnp.int32, (blk, blk), 1)
    causal_ref[...] = jnp.where(key_pos <= qry_pos, 0.0, neg)
    for n in range(nb):
        cols = slice(n * blk, (n + 1) * blk)
        vt_ref[0:HEAD_DIM, cols] = v_ref[0, cols, :].astype(F32).T.astype(BF16)
    vt_ref[HEAD_DIM:, :] = jnp.ones((vt_ref.shape[0] - HEAD_DIM, seq), BF16)


def _moba_scores(j, slot, q_ref, k_ref, bias_ref, causal_ref, s_ref, m_ref, shift_ref):
    blk = MOBA_BLOCK
    lo, hi = j * blk, (j + 1) * blk
    q = q_ref[0, lo:hi, :]
    s_own = _nt_dot(k_ref[0, lo:hi, :], q) + causal_ref[...]
    s_ref[slot, lo:hi, :] = s_own
    m = jnp.max(s_own, axis=0, keepdims=True)
    if j > 0:
        s_past = _nt_dot(k_ref[0, 0:lo, :], q)
        s_ref[slot, 0:lo, :] = s_past
        bias = bias_ref[0:j, lo:hi]
        blk_max = jnp.max(s_past.reshape(j, blk, blk), axis=1)
        m = jnp.maximum(m, jnp.max(blk_max + bias, axis=0, keepdims=True))
        shift_ref[slot, 0:j, :] = m - bias
    m_ref[slot] = m


def _moba_output(i, slot, out_row, o_ref, vt_ref, s_ref, m_ref, shift_ref):
    blk = MOBA_BLOCK
    lo, hi = i * blk, (i + 1) * blk
    p_own = jnp.exp2(s_ref[slot, lo:hi, :] - m_ref[slot])
    acc = jnp.dot(vt_ref[:, lo:hi], p_own.astype(BF16), preferred_element_type=F32)
    if i > 0:
        s_past = s_ref[slot, 0:lo, :].reshape(i, blk, blk)
        p_past = jnp.exp2(s_past - shift_ref[slot, 0:i, :][:, None, :]).reshape(lo, blk)
        acc = acc + jnp.dot(vt_ref[:, 0:lo], p_past.astype(BF16), preferred_element_type=F32)
    inv_l = 1.0 / acc[HEAD_DIM:HEAD_DIM + 1, :]
    o_ref[0, out_row:out_row + blk, :] = (acc[0:HEAD_DIM, :] * inv_l).T.astype(BF16)


def _moba_kernel(q_ref, k_ref, v_ref, o_ref, bias_ref, causal_ref, vt_ref, s_ref, m_ref,
                 shift_ref):
    qi = pl.program_id(2)
    nb = bias_ref.shape[0]

    @pl.when(qi == 0)
    def _():
        _moba_prepare(q_ref, k_ref, v_ref, bias_ref, causal_ref, vt_ref)
        _moba_scores(0, 0, q_ref, k_ref, bias_ref, causal_ref, s_ref, m_ref, shift_ref)

    for g in range(nb // MOBA_Q_BLOCKS_PER_STEP):
        @pl.when(qi == g)
        def _(g=g):
            for r in range(MOBA_Q_BLOCKS_PER_STEP):
                i = g * MOBA_Q_BLOCKS_PER_STEP + r
                if i + 1 < nb:
                    _moba_scores(i + 1, (i + 1) % 2, q_ref, k_ref, bias_ref, causal_ref,
                                 s_ref, m_ref, shift_ref)
                _moba_output(i, i % 2, r * MOBA_BLOCK, o_ref, vt_ref, s_ref, m_ref, shift_ref)


def _moba(z3):
    bsz, seq, _ = z3.shape
    nb = seq // MOBA_BLOCK
    q_col = D_LRU // HEAD_DIM
    k_col = (D_LRU + D_ATTN) // HEAD_DIM
    v_col = (D_LRU + 2 * D_ATTN) // HEAD_DIM
    ones_rows = 2 * SUBLANES
    return pl.pallas_call(
        _moba_kernel,
        grid=(bsz, N_HEADS, nb // MOBA_Q_BLOCKS_PER_STEP),
        in_specs=[
            pl.BlockSpec((1, seq, HEAD_DIM), lambda b, h, i: (b, 0, q_col + h)),
            pl.BlockSpec((1, seq, HEAD_DIM), lambda b, h, i: (b, 0, k_col + h)),
            pl.BlockSpec((1, seq, HEAD_DIM), lambda b, h, i: (b, 0, v_col + h)),
        ],
        out_specs=pl.BlockSpec((1, MOBA_Q_BLOCKS_PER_STEP * MOBA_BLOCK, HEAD_DIM),
                               lambda b, h, i: (b, i, h)),
        out_shape=jax.ShapeDtypeStruct((bsz, seq, D_ATTN), BF16),
        scratch_shapes=[
            pltpu.VMEM((nb, seq), F32),
            pltpu.VMEM((MOBA_BLOCK, MOBA_BLOCK), F32),
            pltpu.VMEM((HEAD_DIM + ones_rows, seq), BF16),
            pltpu.VMEM((2, seq, MOBA_BLOCK), F32),
            pltpu.VMEM((2, 1, MOBA_BLOCK), F32),
            pltpu.VMEM((2, nb, MOBA_BLOCK), F32),
        ],
        compiler_params=_params(("parallel", "parallel", "arbitrary")),
        name="moba_attention",
    )(z3, z3, z3)


def _mix_kernel(x_ref, g_ref, h_ref, o_ref, ga_ref, gb_ref, wl_ref, wa_ref, wo_ref,
                out_ref, acc_ref):
    j = pl.program_id(1)

    @pl.when(j == 0)
    def _():
        acc_ref[...] = jnp.zeros_like(acc_ref)

    ya = jnp.dot(h_ref[...], wl_ref[...], preferred_element_type=F32)
    yb = jnp.dot(o_ref[...], wa_ref[...], preferred_element_type=F32)
    mix = ga_ref[...].astype(F32) * ya + gb_ref[...].astype(F32) * yb
    acc_ref[...] += jnp.dot(mix.astype(BF16), wo_ref[...], preferred_element_type=F32)

    @pl.when(j == pl.num_programs(1) - 1)
    def _():
        y = acc_ref[...]
        out_ref[...] = x_ref[...] + y * _rms_scale(y) * g_ref[...]


def _mix(x, g, h, o, z, w_lru_up, w_attn_up, w_out, layer):
    t = x.shape[0]
    ga_col = (D_LRU + 3 * D_ATTN) // TC_MIX
    gb_col = ga_col + D_MODEL // TC_MIX
    return pl.pallas_call(
        _mix_kernel,
        grid=(t // TM, D_MODEL // TC_MIX),
        in_specs=[
            pl.BlockSpec((TM, D_MODEL), lambda i, j: (i, 0)),
            pl.BlockSpec((1, D_MODEL), lambda i, j: (0, 0)),
            pl.BlockSpec((TM, D_LRU), lambda i, j: (i, 0)),
            pl.BlockSpec((TM, D_ATTN), lambda i, j: (i, 0)),
            pl.BlockSpec((TM, TC_MIX), lambda i, j: (i, ga_col + j)),
            pl.BlockSpec((TM, TC_MIX), lambda i, j: (i, gb_col + j)),
            pl.BlockSpec((None, D_LRU, TC_MIX), lambda i, j: (layer, 0, j)),
            pl.BlockSpec((None, D_ATTN, TC_MIX), lambda i, j: (layer, 0, j)),
            pl.BlockSpec((None, TC_MIX, D_MODEL), lambda i, j: (layer, j, 0)),
        ],
        out_specs=pl.BlockSpec((TM, D_MODEL), lambda i, j: (i, 0)),
        out_shape=jax.ShapeDtypeStruct((t, D_MODEL), F32),
        scratch_shapes=[pltpu.VMEM((TM, D_MODEL), F32)],
        compiler_params=_params(("parallel", "arbitrary")),
        name="mixer_merge_out_projection",
    )(x, g, h, o, z, z, w_lru_up, w_attn_up, w_out)


def _rope_tables(seq):
    pos = jnp.arange(seq, dtype=F32)
    inv = ROPE_THETA ** (-jnp.arange(0, ROT_DIM, 2, dtype=F32) / ROT_DIM)
    ang = pos[:, None] * inv[None, :]
    cos2 = jnp.tile(jnp.cos(ang), (1, 2))
    sin2 = jnp.tile(jnp.sin(ang), (1, 2))
    cos_groups, sin_groups = [], []
    for m in range(HEADS_PER_PARTNER):
        before = m * ROT_DIM
        after = HEAD_DIM - before - ROT_DIM
        cos_groups += [jnp.ones((seq, before), F32), cos2, jnp.ones((seq, after), F32)]
        sin_groups += [jnp.zeros((seq, before), F32), sin2, jnp.zeros((seq, after), F32)]
    return jnp.concatenate(cos_groups, axis=1), jnp.concatenate(sin_groups, axis=1)


def _qk_weights(w_in):
    depth = w_in.shape[0]
    n_chunks = HEAD_DIM // ROT_DIM
    groups = N_HEADS // HEADS_PER_PARTNER
    w = w_in[:, :, D_LRU:D_LRU + 2 * D_ATTN].reshape(
        depth, D_MODEL, 2, groups, HEADS_PER_PARTNER, n_chunks, ROT_DIM)
    per_m = []
    for m in range(HEADS_PER_PARTNER):
        order = list(range(n_chunks))
        order[0], order[m] = order[m], order[0]
        per_m.append(jnp.concatenate([w[:, :, :, :, m, c:c + 1, :] for c in order], axis=-2))
    main = jnp.stack(per_m, axis=4).reshape(depth, D_MODEL, 2 * D_ATTN)
    rot = w[:, :, :, :, :, 0, :]
    partner = jnp.concatenate([-rot[..., ROT_HALF:], rot[..., :ROT_HALF]], axis=-1)
    partner = partner.reshape(depth, D_MODEL, 2 * PARTNER_COLS)
    return jnp.concatenate([main, partner], axis=-1).astype(BF16)


def _gate_weights(wa, wx):
    def pair_diag(w):
        depth = w.shape[0]
        w = w.reshape(depth, N_GATE_GROUPS, 2, LRU_BW, LRU_BW)
        z = jnp.zeros_like(w[:, :, 0])
        top = jnp.concatenate([w[:, :, 0], z], axis=-1)
        bot = jnp.concatenate([z, w[:, :, 1]], axis=-1)
        return jnp.concatenate([top, bot], axis=-2)
    return jnp.concatenate([pair_diag(wa), pair_diag(wx)], axis=-1).astype(BF16)


def kernel(x, norm_gains, ffn_w13, ffn_w2, w_in, b_gate, conv_w, conv_b, lru_wa, lru_ba,
           lru_wx, lru_bx, lru_lambda, w_lru_up, w_attn_up, w_out):
    bsz, seq, d = x.shape
    depth = norm_gains.shape[0]
    t = bsz * seq
    assert d == D_MODEL and seq % MOBA_BLOCK == 0 and seq % TM == 0 and seq % TS_LRU == 0

    w13 = ffn_w13.astype(BF16)
    w2 = ffn_w2.astype(BF16)
    w_in_b = w_in.astype(BF16)
    w_qk = _qk_weights(w_in)
    w_lru_up_b = w_lru_up.astype(BF16)
    w_attn_up_b = w_attn_up.astype(BF16)
    w_out_b = w_out.astype(BF16)
    w_gate = _gate_weights(lru_wa, lru_wx)
    b_gate3 = b_gate.reshape(depth, 1, 2 * D_MODEL)
    conv_b3 = conv_b.reshape(depth, 1, D_LRU)
    ba3 = lru_ba.reshape(depth, 1, D_LRU)
    bx3 = lru_bx.reshape(depth, 1, D_LRU)
    lam3 = lru_lambda.reshape(depth, 1, D_LRU)
    cos_t, sin_t = _rope_tables(seq)

    xs = x.reshape(t, d)
    for l in range(depth):
        gain = lambda k: norm_gains[l, k].reshape(1, d)
        xs = _ffn(xs, gain(0), gain(1), w13, w2, l, 0)
        z = _inproj(xs, gain(2), w_in_b, w_qk, b_gate3, cos_t, sin_t, l, seq)
        z3 = z.reshape(bsz, seq, IN_COLS)
        h = _lru(z3, conv_w, conv_b3, w_gate, ba3, bx3, lam3, l)
        o = _moba(z3)
        xs = _mix(xs, gain(3), h.reshape(t, D_LRU), o.reshape(t, D_ATTN), z,
                  w_lru_up_b, w_attn_up_b, w_out_b, l)
        xs = _ffn(xs, gain(4), gain(5), w13, w2, l, 1)
    return xs.reshape(bsz, seq, d)
```

```python
import functools

import jax
import jax.numpy as jnp
from jax import lax
from jax.experimental import pallas as pl
from jax.experimental.pallas import tpu as pltpu

F32 = jnp.float32
BF16 = jnp.bfloat16

D_MODEL = 2048
D_LRU = 1024
LRU_BLOCKS = 16
LRU_BW = D_LRU // LRU_BLOCKS
CONV_WIDTH = 4
LRU_C = 8.0
N_HEADS = 8
HEAD_DIM = 128
D_ATTN = N_HEADS * HEAD_DIM
MOBA_BLOCK = 256
MOBA_TOPK = 3
ROPE_THETA = 500000.0
ROT_DIM = HEAD_DIM // 4
ROT_HALF = ROT_DIM // 2
D_FF = 5632
NORM_EPS = 1e-6
IN_COLS = D_LRU + 3 * D_ATTN + 2 * D_MODEL

LANES = 128
SUBLANES = 8
VMEM_LIMIT_BYTES = 56 * 1024 * 1024

TM = 512
TF = 512
TN_IN = 1024
ROWS_MIX = 256
TS_LRU = 512
MOBA_Q_BLOCKS_PER_STEP = 4
GATE_GROUP = 2 * LRU_BW
N_GATE_GROUPS = D_LRU // GATE_GROUP


def _rms_scale(x):
    return lax.rsqrt(jnp.mean(x * x, axis=-1, keepdims=True) + NORM_EPS)


def _params(semantics):
    return pltpu.CompilerParams(dimension_semantics=semantics,
                                vmem_limit_bytes=VMEM_LIMIT_BYTES)


def _ffn_kernel(x_ref, gin_ref, gout_ref, wa_ref, wb_ref, w2_ref, o_ref, xn_ref, acc_ref):
    j = pl.program_id(1)

    @pl.when(j == 0)
    def _():
        x = x_ref[...]
        xn_ref[...] = (x * _rms_scale(x) * gin_ref[...]).astype(BF16)
        acc_ref[...] = jnp.zeros_like(acc_ref)

    xn = xn_ref[...]
    a = jnp.dot(xn, wa_ref[...], preferred_element_type=F32)
    b = jnp.dot(xn, wb_ref[...], preferred_element_type=F32)
    h = (a * jax.nn.sigmoid(a) * b).astype(BF16)
    acc_ref[...] += jnp.dot(h, w2_ref[...], preferred_element_type=F32)

    @pl.when(j == pl.num_programs(1) - 1)
    def _():
        y = acc_ref[...]
        o_ref[...] = x_ref[...] + 0.5 * (y * _rms_scale(y) * gout_ref[...])


def _ffn(x, gin, gout, w13, w2, layer, half):
    t = x.shape[0]
    nf = D_FF // TF
    return pl.pallas_call(
        _ffn_kernel,
        grid=(t // TM, nf),
        in_specs=[
            pl.BlockSpec((TM, D_MODEL), lambda i, j: (i, 0)),
            pl.BlockSpec((1, D_MODEL), lambda i, j: (0, 0)),
            pl.BlockSpec((1, D_MODEL), lambda i, j: (0, 0)),
            pl.BlockSpec((None, None, None, D_MODEL, TF), lambda i, j: (layer, half, j, 0, 0)),
            pl.BlockSpec((None, None, None, D_MODEL, TF),
                         lambda i, j: (layer, half, j + nf, 0, 0)),
            pl.BlockSpec((None, None, TF, D_MODEL), lambda i, j: (layer, half, j, 0)),
        ],
        out_specs=pl.BlockSpec((TM, D_MODEL), lambda i, j: (i, 0)),
        out_shape=jax.ShapeDtypeStruct((t, D_MODEL), F32),
        scratch_shapes=[pltpu.VMEM((TM, D_MODEL), BF16), pltpu.VMEM((TM, D_MODEL), F32)],
        compiler_params=_params(("parallel", "arbitrary")),
        name="ffn_half_step",
    )(x, gin, gout, w13, w13, w2)


Q_SCALE = HEAD_DIM ** -0.5 * 1.4426950408889634
HEADS_PER_PARTNER = HEAD_DIM // ROT_DIM
PARTNER_COLS = N_HEADS * ROT_DIM


def _rotary(acc, partner, cos_ref, sin_ref):
    heads = []
    for hd in range(N_HEADS):
        m = hd % HEADS_PER_PARTNER
        c = hd // HEADS_PER_PARTNER
        xh = acc[:, hd * HEAD_DIM:(hd + 1) * HEAD_DIM]
        ph = partner[:, c * HEAD_DIM:(c + 1) * HEAD_DIM]
        lanes = slice(m * HEAD_DIM, (m + 1) * HEAD_DIM)
        heads.append(xh * cos_ref[:, lanes] + ph * sin_ref[:, lanes])
    return jnp.concatenate(heads, axis=1)


def _inproj_kernel(x_ref, g_ref, w_ref, wqk_ref, wp_ref, bg_ref, cos_ref, sin_ref, z_ref,
                   xn_ref):
    j = pl.program_id(1)
    q_blk = D_LRU // TN_IN
    k_blk = (D_LRU + D_ATTN) // TN_IN
    v_blk = (D_LRU + 2 * D_ATTN) // TN_IN
    gate_blk = (D_LRU + 3 * D_ATTN) // TN_IN

    @pl.when(j == 0)
    def _():
        x = x_ref[...]
        xn_ref[...] = (x * _rms_scale(x) * g_ref[...]).astype(BF16)

    def project(w):
        return jnp.dot(xn_ref[...], w[...], preferred_element_type=F32)

    @pl.when((j < q_blk) | (j == v_blk))
    def _():
        z_ref[...] = project(w_ref).astype(BF16)

    @pl.when(j == q_blk)
    def _():
        rot = _rotary(project(wqk_ref), project(wp_ref), cos_ref, sin_ref)
        z_ref[...] = (rot * Q_SCALE).astype(BF16)

    @pl.when(j == k_blk)
    def _():
        z_ref[...] = _rotary(project(wqk_ref), project(wp_ref), cos_ref, sin_ref).astype(BF16)

    @pl.when(j >= gate_blk)
    def _():
        z_ref[...] = jax.nn.sigmoid(project(w_ref) + bg_ref[...]).astype(BF16)


def _inproj(x, g, w_in, w_qk, w_partner, b_gate, cos_t, sin_t, layer, seq):
    t = x.shape[0]
    q_blk = D_LRU // TN_IN
    k_blk = (D_LRU + D_ATTN) // TN_IN
    gate_blk = (D_LRU + 3 * D_ATTN) // TN_IN
    seq_tiles = seq // TM
    is_k = lambda j: (j == k_blk).astype(jnp.int32)
    main_blk = lambda j: jnp.where((j == q_blk) | (j == k_blk), q_blk - 1, j)
    table = lambda: pl.BlockSpec((TM, HEADS_PER_PARTNER * HEAD_DIM),
                                 lambda i, j: (i % seq_tiles, 0))
    return pl.pallas_call(
        _inproj_kernel,
        grid=(t // TM, IN_COLS // TN_IN),
        in_specs=[
            pl.BlockSpec((TM, D_MODEL), lambda i, j: (i, 0)),
            pl.BlockSpec((1, D_MODEL), lambda i, j: (0, 0)),
            pl.BlockSpec((None, None, D_MODEL, TN_IN), lambda i, j: (layer, main_blk(j), 0, 0)),
            pl.BlockSpec((None, None, D_MODEL, D_ATTN), lambda i, j: (layer, is_k(j), 0, 0)),
            pl.BlockSpec((None, None, D_MODEL, PARTNER_COLS),
                         lambda i, j: (layer, is_k(j), 0, 0)),
            pl.BlockSpec((None, 1, TN_IN), lambda i, j: (layer, 0, jnp.maximum(j - gate_blk, 0))),
            table(), table(),
        ],
        out_specs=pl.BlockSpec((TM, TN_IN), lambda i, j: (i, j)),
        out_shape=jax.ShapeDtypeStruct((t, IN_COLS), BF16),
        scratch_shapes=[pltpu.VMEM((TM, D_MODEL), BF16)],
        compiler_params=_params(("parallel", "arbitrary")),
        name="mixer_in_projection",
    )(x, g, w_in, w_qk, w_partner, b_gate, cos_t, sin_t)


def _lru_kernel(u_ref, cw_ref, cb_ref, wg_ref, ba_ref, bx_ref, lam_ref, h_ref,
                ubuf_ref, a_ref, b_ref, hout_ref, carry_ref):
    ts = TS_LRU
    pad = SUBLANES

    @pl.when(pl.program_id(1) == 0)
    def _():
        ubuf_ref[0:pad, :] = jnp.zeros((pad, D_LRU), F32)
        carry_ref[...] = jnp.zeros_like(carry_ref)

    ubuf_ref[pad:pad + ts, :] = u_ref[0].astype(F32)
    xc = cb_ref[...]
    for tap in range(CONV_WIDTH):
        off = pad - (CONV_WIDTH - 1) + tap
        xc = xc + ubuf_ref[off:off + ts, :] * cw_ref[tap:tap + 1, :]
    ubuf_ref[0:pad, :] = ubuf_ref[ts:ts + pad, :]

    neg_c_softplus = -LRU_C * jax.nn.softplus(-lam_ref[...])
    for g in range(N_GATE_GROUPS):
        cols = slice(g * GATE_GROUP, (g + 1) * GATE_GROUP)
        xg = xc[:, cols]
        lin = jnp.dot(xg.astype(BF16), wg_ref[g], preferred_element_type=F32)
        r = jax.nn.sigmoid(lin[:, :GATE_GROUP] + ba_ref[:, cols])
        i = jax.nn.sigmoid(lin[:, GATE_GROUP:] + bx_ref[:, cols])
        log_a = r * neg_c_softplus[:, cols]
        a = jnp.exp(log_a)
        a_ref[:, cols] = a
        b_ref[:, cols] = jnp.sqrt(1.0 - a * a) * (i * xg)

    def step(t, h):
        h = a_ref[pl.ds(t, 1), :] * h + b_ref[pl.ds(t, 1), :]
        hout_ref[pl.ds(t, 1), :] = h
        return h

    carry_ref[...] = lax.fori_loop(0, ts, step, carry_ref[...], unroll=8)
    h_ref[0] = hout_ref[...].astype(BF16)


def _lru(z3, conv_w, conv_b, w_gate, ba, bx, lam, layer):
    bsz, seq, _ = z3.shape
    vec = lambda: pl.BlockSpec((None, 1, D_LRU), lambda b, t: (layer, 0, 0))
    return pl.pallas_call(
        _lru_kernel,
        grid=(bsz, seq // TS_LRU),
        in_specs=[
            pl.BlockSpec((1, TS_LRU, D_LRU), lambda b, t: (b, t, 0)),
            pl.BlockSpec((None, CONV_WIDTH, D_LRU), lambda b, t: (layer, 0, 0)),
            vec(),
            pl.BlockSpec((None, N_GATE_GROUPS, GATE_GROUP, 2 * GATE_GROUP),
                         lambda b, t: (layer, 0, 0, 0)),
            vec(), vec(), vec(),
        ],
        out_specs=pl.BlockSpec((1, TS_LRU, D_LRU), lambda b, t: (b, t, 0)),
        out_shape=jax.ShapeDtypeStruct((bsz, seq, D_LRU), BF16),
        scratch_shapes=[
            pltpu.VMEM((SUBLANES + TS_LRU, D_LRU), F32),
            pltpu.VMEM((TS_LRU, D_LRU), F32),
            pltpu.VMEM((TS_LRU, D_LRU), F32),
            pltpu.VMEM((TS_LRU, D_LRU), F32),
            pltpu.VMEM((1, D_LRU), F32),
        ],
        compiler_params=_params(("parallel", "arbitrary")),
        name="rg_lru_branch",
    )(z3, conv_w, conv_b, w_gate, ba, bx, lam)


def _nt_dot(a, b):
    return lax.dot_general(a, b, (((1,), (1,)), ((), ())), preferred_element_type=F32)


def _moba_prepare(q_ref, k_ref, v_ref, bias_ref, causal_ref, vt_ref):
    nb, seq = bias_ref.shape
    blk = MOBA_BLOCK
    neg = -jnp.inf
    kmean = jnp.mean(k_ref[0].astype(F32).reshape(nb, blk, HEAD_DIM), axis=1)
    hi = kmean.astype(BF16)
    lo = (kmean - hi.astype(F32)).astype(BF16)
    q_all = q_ref[0]
    gate = _nt_dot(hi, q_all) + _nt_dot(lo, q_all)
    blk_id = lax.broadcasted_iota(jnp.int32, (nb, seq), 0)
    qry_blk = lax.broadcasted_iota(jnp.int32, (nb, seq), 1) // blk
    gate = jnp.where(blk_id < qry_blk, gate, neg)
    bias = jnp.full((nb, seq), neg, F32)
    for _ in range(MOBA_TOPK):
        top = jnp.max(gate, axis=0, keepdims=True)
        first = jnp.min(jnp.where(gate == top, blk_id, nb), axis=0, keepdims=True)
        pick = (blk_id == first) & (top > neg)
        bias = jnp.where(pick, 0.0, bias)
        gate = jnp.where(pick, neg, gate)
    bias_ref[...] = bias
    key_pos = lax.broadcasted_iota(jnp.int32, (blk, blk), 0)
    qry_pos = lax.broadcasted_iota(jnp.int32, (blk, blk), 1)
    causal_ref[...] = jnp.where(key_pos <= qry_pos, 0.0, neg)
    for n in range(nb):
        cols = slice(n * blk, (n + 1) * blk)
        vt_ref[0:HEAD_DIM, cols] = v_ref[0, cols, :].astype(F32).T.astype(BF16)
    vt_ref[HEAD_DIM:, :] = jnp.ones((vt_ref.shape[0] - HEAD_DIM, seq), BF16)


def _moba_scores(j, slot, q_ref, k_ref, bias_ref, causal_ref, s_ref, m_ref, shift_ref):
    blk = MOBA_BLOCK
    lo, hi = j * blk, (j + 1) * blk
    q = q_ref[0, lo:hi, :]
    s_own = _nt_dot(k_ref[0, lo:hi, :], q) + causal_ref[...]
    s_ref[slot, lo:hi, :] = s_own
    m = jnp.max(s_own, axis=0, keepdims=True)
    if j > 0:
        s_past = _nt_dot(k_ref[0, 0:lo, :], q)
        s_ref[slot, 0:lo, :] = s_past
        bias = bias_ref[0:j, lo:hi]
        blk_max = jnp.max(s_past.reshape(j, blk, blk), axis=1)
        m = jnp.maximum(m, jnp.max(blk_max + bias, axis=0, keepdims=True))
        shift_ref[slot, 0:j, :] = m - bias
    m_ref[slot] = m


def _moba_output(i, slot, out_row, o_ref, vt_ref, s_ref, m_ref, shift_ref):
    blk = MOBA_BLOCK
    lo, hi = i * blk, (i + 1) * blk
    p_own = jnp.exp2(s_ref[slot, lo:hi, :] - m_ref[slot])
    acc = jnp.dot(vt_ref[:, lo:hi], p_own.astype(BF16), preferred_element_type=F32)
    if i > 0:
        s_past = s_ref[slot, 0:lo, :].reshape(i, blk, blk)
        p_past = jnp.exp2(s_past - shift_ref[slot, 0:i, :][:, None, :]).reshape(lo, blk)
        acc = acc + jnp.dot(vt_ref[:, 0:lo], p_past.astype(BF16), preferred_element_type=F32)
    inv_l = 1.0 / acc[HEAD_DIM:HEAD_DIM + 1, :]
    o_ref[0, out_row:out_row + blk, :] = (acc[0:HEAD_DIM, :] * inv_l).T.astype(BF16)


def _moba_kernel(q_ref, k_ref, v_ref, o_ref, bias_ref, causal_ref, vt_ref, s_ref, m_ref,
                 shift_ref):
    qi = pl.program_id(2)
    nb = bias_ref.shape[0]

    @pl.when(qi == 0)
    def _():
        _moba_prepare(q_ref, k_ref, v_ref, bias_ref, causal_ref, vt_ref)
        _moba_scores(0, 0, q_ref, k_ref, bias_ref, causal_ref, s_ref, m_ref, shift_ref)

    for g in range(nb // MOBA_Q_BLOCKS_PER_STEP):
        @pl.when(qi == g)
        def _(g=g):
            for r in range(MOBA_Q_BLOCKS_PER_STEP):
                i = g * MOBA_Q_BLOCKS_PER_STEP + r
                if i + 1 < nb:
                    _moba_scores(i + 1, (i + 1) % 2, q_ref, k_ref, bias_ref, causal_ref,
                                 s_ref, m_ref, shift_ref)
                _moba_output(i, i % 2, r * MOBA_BLOCK, o_ref, vt_ref, s_ref, m_ref, shift_ref)


def _moba(z3):
    bsz, seq, _ = z3.shape
    nb = seq // MOBA_BLOCK
    q_col = D_LRU // HEAD_DIM
    k_col = (D_LRU + D_ATTN) // HEAD_DIM
    v_col = (D_LRU + 2 * D_ATTN) // HEAD_DIM
    ones_rows = 2 * SUBLANES
    return pl.pallas_call(
        _moba_kernel,
        grid=(bsz, N_HEADS, nb // MOBA_Q_BLOCKS_PER_STEP),
        in_specs=[
            pl.BlockSpec((1, seq, HEAD_DIM), lambda b, h, i: (b, 0, q_col + h)),
            pl.BlockSpec((1, seq, HEAD_DIM), lambda b, h, i: (b, 0, k_col + h)),
            pl.BlockSpec((1, seq, HEAD_DIM), lambda b, h, i: (b, 0, v_col + h)),
        ],
        out_specs=pl.BlockSpec((1, MOBA_Q_BLOCKS_PER_STEP * MOBA_BLOCK, HEAD_DIM),
                               lambda b, h, i: (b, i, h)),
        out_shape=jax.ShapeDtypeStruct((bsz, seq, D_ATTN), BF16),
        scratch_shapes=[
            pltpu.VMEM((nb, seq), F32),
            pltpu.VMEM((MOBA_BLOCK, MOBA_BLOCK), F32),
            pltpu.VMEM((HEAD_DIM + ones_rows, seq), BF16),
            pltpu.VMEM((2, seq, MOBA_BLOCK), F32),
            pltpu.VMEM((2, 1, MOBA_BLOCK), F32),
            pltpu.VMEM((2, nb, MOBA_BLOCK), F32),
        ],
        compiler_params=_params(("parallel", "parallel", "arbitrary")),
        name="moba_attention",
    )(z3, z3, z3)


def _mix_kernel(x_ref, g_ref, h_ref, o_ref, gate_ref, wl_ref, wa_ref, wo_ref, out_ref):
    for r in range(TM // ROWS_MIX):
        rows = slice(r * ROWS_MIX, (r + 1) * ROWS_MIX)
        ya = jnp.dot(h_ref[rows, :], wl_ref[...], preferred_element_type=F32)
        yb = jnp.dot(o_ref[rows, :], wa_ref[...], preferred_element_type=F32)
        mix = (gate_ref[rows, 0:D_MODEL].astype(F32) * ya
               + gate_ref[rows, D_MODEL:2 * D_MODEL].astype(F32) * yb)
        y = jnp.dot(mix.astype(BF16), wo_ref[...], preferred_element_type=F32)
        out_ref[rows, :] = x_ref[rows, :] + y * _rms_scale(y) * g_ref[...]


def _mix(x, g, h, o, z, w_lru_up, w_attn_up, w_out, layer):
    t = x.shape[0]
    gate_blk = (D_LRU + 3 * D_ATTN) // (2 * D_MODEL)
    resident = lambda rows, cols: pl.BlockSpec((None, rows, cols), lambda i: (layer, 0, 0),
                                               pipeline_mode=pl.Buffered(1))
    return pl.pallas_call(
        _mix_kernel,
        grid=(t // TM,),
        in_specs=[
            pl.BlockSpec((TM, D_MODEL), lambda i: (i, 0)),
            pl.BlockSpec((1, D_MODEL), lambda i: (0, 0)),
            pl.BlockSpec((TM, D_LRU), lambda i: (i, 0)),
            pl.BlockSpec((TM, D_ATTN), lambda i: (i, 0)),
            pl.BlockSpec((TM, 2 * D_MODEL), lambda i: (i, gate_blk)),
            resident(D_LRU, D_MODEL),
            resident(D_ATTN, D_MODEL),
            resident(D_MODEL, D_MODEL),
        ],
        out_specs=pl.BlockSpec((TM, D_MODEL), lambda i: (i, 0)),
        out_shape=jax.ShapeDtypeStruct((t, D_MODEL), F32),
        compiler_params=_params(("parallel",)),
        name="mixer_merge_out_projection",
    )(x, g, h, o, z, w_lru_up, w_attn_up, w_out)


def _rope_tables(seq):
    pos = jnp.arange(seq, dtype=F32)
    inv = ROPE_THETA ** (-jnp.arange(0, ROT_DIM, 2, dtype=F32) / ROT_DIM)
    ang = pos[:, None] * inv[None, :]
    cos2 = jnp.tile(jnp.cos(ang), (1, 2))
    sin2 = jnp.tile(jnp.sin(ang), (1, 2))
    cos_groups, sin_groups = [], []
    for m in range(HEADS_PER_PARTNER):
        before = m * ROT_DIM
        after = HEAD_DIM - before - ROT_DIM
        cos_groups += [jnp.ones((seq, before), F32), cos2, jnp.ones((seq, after), F32)]
        sin_groups += [jnp.zeros((seq, before), F32), sin2, jnp.zeros((seq, after), F32)]
    return jnp.concatenate(cos_groups, axis=1), jnp.concatenate(sin_groups, axis=1)


def _qk_weights(w_in):
    depth = w_in.shape[0]
    n_chunks = HEAD_DIM // ROT_DIM
    groups = N_HEADS // HEADS_PER_PARTNER
    w = w_in[:, :, D_LRU:D_LRU + 2 * D_ATTN].reshape(
        depth, D_MODEL, 2, groups, HEADS_PER_PARTNER, n_chunks, ROT_DIM)
    per_m = []
    for m in range(HEADS_PER_PARTNER):
        order = list(range(n_chunks))
        order[0], order[m] = order[m], order[0]
        per_m.append(jnp.concatenate([w[:, :, :, :, m, c:c + 1, :] for c in order], axis=-2))
    main = jnp.stack(per_m, axis=4).reshape(depth, D_MODEL, 2 * D_ATTN)
    rot = w[:, :, :, :, :, 0, :]
    partner = jnp.concatenate([-rot[..., ROT_HALF:], rot[..., :ROT_HALF]], axis=-1)
    partner = partner.reshape(depth, D_MODEL, 2 * PARTNER_COLS)
    return _column_blocks(main, D_ATTN), _column_blocks(partner, PARTNER_COLS)


def _column_blocks(w, width):
    *lead, rows, cols = w.shape
    w = w.astype(BF16).reshape(*lead, rows, cols // width, width)
    return jnp.swapaxes(w, -3, -2)


def _gate_weights(wa, wx):
    def pair_diag(w):
        depth = w.shape[0]
        w = w.reshape(depth, N_GATE_GROUPS, 2, LRU_BW, LRU_BW)
        z = jnp.zeros_like(w[:, :, 0])
        top = jnp.concatenate([w[:, :, 0], z], axis=-1)
        bot = jnp.concatenate([z, w[:, :, 1]], axis=-1)
        return jnp.concatenate([top, bot], axis=-2)
    return jnp.concatenate([pair_diag(wa), pair_diag(wx)], axis=-1).astype(BF16)


def kernel(x, norm_gains, ffn_w13, ffn_w2, w_in, b_gate, conv_w, conv_b, lru_wa, lru_ba,
           lru_wx, lru_bx, lru_lambda, w_lru_up, w_attn_up, w_out):
    bsz, seq, d = x.shape
    depth = norm_gains.shape[0]
    t = bsz * seq
    assert d == D_MODEL and seq % MOBA_BLOCK == 0 and seq % TM == 0 and seq % TS_LRU == 0

    w13 = _column_blocks(ffn_w13, TF)
    w2 = ffn_w2.astype(BF16)
    w_in_b = _column_blocks(w_in, TN_IN)
    w_qk, w_partner = _qk_weights(w_in)
    w_lru_up_b = w_lru_up.astype(BF16)
    w_attn_up_b = w_attn_up.astype(BF16)
    w_out_b = w_out.astype(BF16)
    w_gate = _gate_weights(lru_wa, lru_wx)
    b_gate3 = b_gate.reshape(depth, 1, 2 * D_MODEL)
    conv_b3 = conv_b.reshape(depth, 1, D_LRU)
    ba3 = lru_ba.reshape(depth, 1, D_LRU)
    bx3 = lru_bx.reshape(depth, 1, D_LRU)
    lam3 = lru_lambda.reshape(depth, 1, D_LRU)
    cos_t, sin_t = _rope_tables(seq)

    xs = x.reshape(t, d)
    for l in range(depth):
        gain = lambda k: norm_gains[l, k].reshape(1, d)
        xs = _ffn(xs, gain(0), gain(1), w13, w2, l, 0)
        z = _inproj(xs, gain(2), w_in_b, w_qk, w_partner, b_gate3, cos_t, sin_t, l, seq)
        z3 = z.reshape(bsz, seq, IN_COLS)
        h = _lru(z3, conv_w, conv_b3, w_gate, ba3, bx3, lam3, l)
        o = _moba(z3)
        xs = _mix(xs, gain(3), h.reshape(t, D_LRU), o.reshape(t, D_ATTN), z,
                  w_lru_up_b, w_attn_up_b, w_out_b, l)
        xs = _ffn(xs, gain(4), gain(5), w13, w2, l, 1)
    return xs.reshape(bsz, seq, d)
```

```python
import jax
import jax.numpy as jnp
from jax import lax
from jax.experimental import pallas as pl
from jax.experimental.pallas import tpu as pltpu

F32 = jnp.float32
BF16 = jnp.bfloat16

D_MODEL = 2048
D_LRU = 1024
LRU_BLOCKS = 16
LRU_BW = D_LRU // LRU_BLOCKS
CONV_WIDTH = 4
LRU_C = 8.0
N_HEADS = 8
HEAD_DIM = 128
D_ATTN = N_HEADS * HEAD_DIM
MOBA_BLOCK = 256
MOBA_TOPK = 3
ROPE_THETA = 500000.0
ROT_DIM = HEAD_DIM // 4
ROT_HALF = ROT_DIM // 2
D_FF = 5632
NORM_EPS = 1e-6
IN_COLS = D_LRU + 3 * D_ATTN + 2 * D_MODEL

SUBLANES = 8
VMEM_LIMIT_BYTES = 56 * 1024 * 1024

TM = 512
TF = 512
TM_IN = 1024
TN_IN = 1024
ROWS_MIX = 256
TS_LRU = 512
MOBA_Q_BLOCKS_PER_STEP = 4
GATE_GROUP = 2 * LRU_BW
N_GATE_GROUPS = D_LRU // GATE_GROUP


def _rms_scale(x):
    return lax.rsqrt(jnp.mean(x * x, axis=-1, keepdims=True) + NORM_EPS)


def _params(semantics):
    return pltpu.CompilerParams(dimension_semantics=semantics,
                                vmem_limit_bytes=VMEM_LIMIT_BYTES)


def _ffn_kernel(x_ref, gin_ref, gout_ref, gnext_ref, wa_ref, wb_ref, w2_ref, o_ref, *rest):
    next_ref = rest[0] if len(rest) == 3 else None
    xn_ref, acc_ref = rest[-2:]
    j = pl.program_id(1)

    @pl.when(j == 0)
    def _():
        x = x_ref[...]
        xn_ref[...] = (x * _rms_scale(x) * gin_ref[...]).astype(BF16)
        acc_ref[...] = jnp.zeros_like(acc_ref)

    xn = xn_ref[...]
    a = jnp.dot(xn, wa_ref[...], preferred_element_type=F32)
    b = jnp.dot(xn, wb_ref[...], preferred_element_type=F32)
    h = (a * jax.nn.sigmoid(a) * b).astype(BF16)
    acc_ref[...] += jnp.dot(h, w2_ref[...], preferred_element_type=F32)

    @pl.when(j == pl.num_programs(1) - 1)
    def _():
        y = acc_ref[...]
        out = x_ref[...] + 0.5 * (y * _rms_scale(y) * gout_ref[...])
        o_ref[...] = out
        if next_ref is not None:
            next_ref[...] = (out * _rms_scale(out) * gnext_ref[...]).astype(BF16)


def _ffn(x, gin, gout, gnext, w13, w2, layer, half):
    t = x.shape[0]
    nf = D_FF // TF
    row_block = pl.BlockSpec((TM, D_MODEL), lambda i, j: (i, 0))
    gain_block = pl.BlockSpec((1, D_MODEL), lambda i, j: (0, 0))
    out_specs, out_shape = row_block, jax.ShapeDtypeStruct((t, D_MODEL), F32)
    if gnext is not None:
        out_specs = [row_block, row_block]
        out_shape = [out_shape, jax.ShapeDtypeStruct((t, D_MODEL), BF16)]
    return pl.pallas_call(
        _ffn_kernel,
        grid=(t // TM, nf),
        in_specs=[
            row_block, gain_block, gain_block, gain_block,
            pl.BlockSpec((None, None, D_MODEL, TF), lambda i, j: (layer, half, 0, j)),
            pl.BlockSpec((None, None, D_MODEL, TF), lambda i, j: (layer, half, 0, j + nf)),
            pl.BlockSpec((None, None, TF, D_MODEL), lambda i, j: (layer, half, j, 0)),
        ],
        out_specs=out_specs,
        out_shape=out_shape,
        scratch_shapes=[pltpu.VMEM((TM, D_MODEL), BF16), pltpu.VMEM((TM, D_MODEL), F32)],
        compiler_params=_params(("parallel", "arbitrary")),
        name="ffn_half_step",
    )(x, gin, gout, gout if gnext is None else gnext, w13, w13, w2)


Q_SCALE = HEAD_DIM ** -0.5 * 1.4426950408889634
HEADS_PER_PARTNER = HEAD_DIM // ROT_DIM
PARTNER_COLS = N_HEADS * ROT_DIM


def _rotary(acc, partner, cos_ref, sin_ref):
    heads = []
    for hd in range(N_HEADS):
        m = hd % HEADS_PER_PARTNER
        c = hd // HEADS_PER_PARTNER
        xh = acc[:, hd * HEAD_DIM:(hd + 1) * HEAD_DIM]
        ph = partner[:, c * HEAD_DIM:(c + 1) * HEAD_DIM]
        lanes = slice(m * HEAD_DIM, (m + 1) * HEAD_DIM)
        heads.append(xh * cos_ref[:, lanes] + ph * sin_ref[:, lanes])
    return jnp.concatenate(heads, axis=1)


def _inproj_kernel(xn_ref, w_ref, wqk_ref, wp_ref, bg_ref, cos_ref, sin_ref, z_ref):
    j = pl.program_id(1)
    q_blk = D_LRU // TN_IN
    k_blk = (D_LRU + D_ATTN) // TN_IN
    v_blk = (D_LRU + 2 * D_ATTN) // TN_IN
    gate_blk = (D_LRU + 3 * D_ATTN) // TN_IN

    def project(w):
        return jnp.dot(xn_ref[...], w[...], preferred_element_type=F32)

    @pl.when((j < q_blk) | (j == v_blk))
    def _():
        z_ref[...] = project(w_ref).astype(BF16)

    @pl.when(j == q_blk)
    def _():
        rot = _rotary(project(wqk_ref), project(wp_ref), cos_ref, sin_ref)
        z_ref[...] = (rot * Q_SCALE).astype(BF16)

    @pl.when(j == k_blk)
    def _():
        z_ref[...] = _rotary(project(wqk_ref), project(wp_ref), cos_ref, sin_ref).astype(BF16)

    @pl.when(j >= gate_blk)
    def _():
        z_ref[...] = jax.nn.sigmoid(project(w_ref) + bg_ref[...]).astype(BF16)


def _inproj(xn, w_in, w_qk, w_partner, b_gate, cos_t, sin_t, layer, seq):
    t = xn.shape[0]
    q_blk = D_LRU // TN_IN
    k_blk = (D_LRU + D_ATTN) // TN_IN
    gate_blk = (D_LRU + 3 * D_ATTN) // TN_IN
    seq_tiles = seq // TM_IN
    is_k = lambda j: (j == k_blk).astype(jnp.int32)
    main_blk = lambda j: jnp.where((j == q_blk) | (j == k_blk), q_blk - 1, j)
    table = lambda: pl.BlockSpec((TM_IN, HEADS_PER_PARTNER * HEAD_DIM),
                                 lambda i, j: (i % seq_tiles, 0))
    return pl.pallas_call(
        _inproj_kernel,
        grid=(t // TM_IN, IN_COLS // TN_IN),
        in_specs=[
            pl.BlockSpec((TM_IN, D_MODEL), lambda i, j: (i, 0)),
            pl.BlockSpec((None, D_MODEL, TN_IN), lambda i, j: (layer, 0, main_blk(j))),
            pl.BlockSpec((None, D_MODEL, D_ATTN), lambda i, j: (layer, 0, is_k(j))),
            pl.BlockSpec((None, D_MODEL, PARTNER_COLS), lambda i, j: (layer, 0, is_k(j))),
            pl.BlockSpec((None, 1, TN_IN), lambda i, j: (layer, 0, jnp.maximum(j - gate_blk, 0))),
            table(), table(),
        ],
        out_specs=pl.BlockSpec((TM_IN, TN_IN), lambda i, j: (i, j)),
        out_shape=jax.ShapeDtypeStruct((t, IN_COLS), BF16),
        compiler_params=_params(("parallel", "arbitrary")),
        name="mixer_in_projection",
    )(xn, w_in, w_qk, w_partner, b_gate, cos_t, sin_t)


def _lru_kernel(u_ref, cw_ref, cb_ref, wg_ref, ba_ref, bx_ref, lam_ref, h_ref,
                ubuf_ref, a_ref, b_ref, hout_ref, carry_ref):
    ts = TS_LRU
    pad = SUBLANES

    @pl.when(pl.program_id(1) == 0)
    def _():
        ubuf_ref[0:pad, :] = jnp.zeros((pad, D_LRU), F32)
        carry_ref[...] = jnp.zeros_like(carry_ref)

    ubuf_ref[pad:pad + ts, :] = u_ref[0].astype(F32)
    xc = cb_ref[...]
    for tap in range(CONV_WIDTH):
        off = pad - (CONV_WIDTH - 1) + tap
        xc = xc + ubuf_ref[off:off + ts, :] * cw_ref[tap:tap + 1, :]
    ubuf_ref[0:pad, :] = ubuf_ref[ts:ts + pad, :]

    neg_c_softplus = -LRU_C * jax.nn.softplus(-lam_ref[...])
    for g in range(N_GATE_GROUPS):
        cols = slice(g * GATE_GROUP, (g + 1) * GATE_GROUP)
        xg = xc[:, cols]
        lin = jnp.dot(xg.astype(BF16), wg_ref[g], preferred_element_type=F32)
        r = jax.nn.sigmoid(lin[:, :GATE_GROUP] + ba_ref[:, cols])
        i = jax.nn.sigmoid(lin[:, GATE_GROUP:] + bx_ref[:, cols])
        log_a = r * neg_c_softplus[:, cols]
        a = jnp.exp(log_a)
        a_ref[:, cols] = a
        b_ref[:, cols] = jnp.sqrt(1.0 - a * a) * (i * xg)

    def step(t, h):
        h = a_ref[pl.ds(t, 1), :] * h + b_ref[pl.ds(t, 1), :]
        hout_ref[pl.ds(t, 1), :] = h
        return h

    carry_ref[...] = lax.fori_loop(0, ts, step, carry_ref[...], unroll=8)
    h_ref[0] = hout_ref[...].astype(BF16)


def _lru(z3, conv_w, conv_b, w_gate, ba, bx, lam, layer):
    bsz, seq, _ = z3.shape
    vec = lambda: pl.BlockSpec((None, 1, D_LRU), lambda b, t: (layer, 0, 0))
    return pl.pallas_call(
        _lru_kernel,
        grid=(bsz, seq // TS_LRU),
        in_specs=[
            pl.BlockSpec((1, TS_LRU, D_LRU), lambda b, t: (b, t, 0)),
            pl.BlockSpec((None, CONV_WIDTH, D_LRU), lambda b, t: (layer, 0, 0)),
            vec(),
            pl.BlockSpec((None, N_GATE_GROUPS, GATE_GROUP, 2 * GATE_GROUP),
                         lambda b, t: (layer, 0, 0, 0)),
            vec(), vec(), vec(),
        ],
        out_specs=pl.BlockSpec((1, TS_LRU, D_LRU), lambda b, t: (b, t, 0)),
        out_shape=jax.ShapeDtypeStruct((bsz, seq, D_LRU), BF16),
        scratch_shapes=[
            pltpu.VMEM((SUBLANES + TS_LRU, D_LRU), F32),
            pltpu.VMEM((TS_LRU, D_LRU), F32),
            pltpu.VMEM((TS_LRU, D_LRU), F32),
            pltpu.VMEM((TS_LRU, D_LRU), F32),
            pltpu.VMEM((1, D_LRU), F32),
        ],
        compiler_params=_params(("parallel", "arbitrary")),
        name="rg_lru_branch",
    )(z3, conv_w, conv_b, w_gate, ba, bx, lam)


def _nt_dot(a, b):
    return lax.dot_general(a, b, (((1,), (1,)), ((), ())), preferred_element_type=F32)


def _moba_prepare(q_ref, k_ref, v_ref, bias_ref, causal_ref, vt_ref):
    nb, seq = bias_ref.shape
    blk = MOBA_BLOCK
    neg = -jnp.inf
    kmean = jnp.mean(k_ref[0].astype(F32).reshape(nb, blk, HEAD_DIM), axis=1)
    hi = kmean.astype(BF16)
    lo = (kmean - hi.astype(F32)).astype(BF16)
    q_all = q_ref[0]
    gate = _nt_dot(hi, q_all) + _nt_dot(lo, q_all)
    blk_id = lax.broadcasted_iota(jnp.int32, (nb, seq), 0)
    qry_blk = lax.broadcasted_iota(jnp.int32, (nb, seq), 1) // blk
    gate = jnp.where(blk_id < qry_blk, gate, neg)
    bias = jnp.full((nb, seq), neg, F32)
    for _ in range(MOBA_TOPK):
        top = jnp.max(gate, axis=0, keepdims=True)
        first = jnp.min(jnp.where(gate == top, blk_id, nb), axis=0, keepdims=True)
        pick = (blk_id == first) & (top > neg)
        bias = jnp.where(pick, 0.0, bias)
        gate = jnp.where(pick, neg, gate)
    bias_ref[...] = bias
    key_pos = lax.broadcasted_iota(jnp.int32, (blk, blk), 0)
    qry_pos = lax.broadcasted_iota(jnp.int32, (blk, blk), 1)
    causal_ref[...] = jnp.where(key_pos <= qry_pos, 0.0, neg)
    for n in range(nb):
        cols = slice(n * blk, (n + 1) * blk)
        vt_ref[0:HEAD_DIM, cols] = v_ref[0, cols, :].astype(F32).T.astype(BF16)
    vt_ref[HEAD_DIM:, :] = jnp.ones((vt_ref.shape[0] - HEAD_DIM, seq), BF16)


def _moba_scores(j, slot, q_ref, k_ref, bias_ref, causal_ref, s_ref, m_ref, shift_ref):
    blk = MOBA_BLOCK
    lo, hi = j * blk, (j + 1) * blk
    q = q_ref[0, lo:hi, :]
    s_own = _nt_dot(k_ref[0, lo:hi, :], q) + causal_ref[...]
    s_ref[slot, lo:hi, :] = s_own
    m = jnp.max(s_own, axis=0, keepdims=True)
    if j > 0:
        s_past = _nt_dot(k_ref[0, 0:lo, :], q)
        s_ref[slot, 0:lo, :] = s_past
        bias = bias_ref[0:j, lo:hi]
        blk_max = jnp.max(s_past.reshape(j, blk, blk), axis=1)
        m = jnp.maximum(m, jnp.max(blk_max + bias, axis=0, keepdims=True))
        shift_ref[slot, 0:j, :] = m - bias
    m_ref[slot] = m


def _moba_output(i, slot, out_row, o_ref, vt_ref, s_ref, m_ref, shift_ref):
    blk = MOBA_BLOCK
    lo, hi = i * blk, (i + 1) * blk
    p_own = jnp.exp2(s_ref[slot, lo:hi, :] - m_ref[slot])
    acc = jnp.dot(vt_ref[:, lo:hi], p_own.astype(BF16), preferred_element_type=F32)
    if i > 0:
        s_past = s_ref[slot, 0:lo, :].reshape(i, blk, blk)
        p_past = jnp.exp2(s_past - shift_ref[slot, 0:i, :][:, None, :]).reshape(lo, blk)
        acc = acc + jnp.dot(vt_ref[:, 0:lo], p_past.astype(BF16), preferred_element_type=F32)
    inv_l = 1.0 / acc[HEAD_DIM:HEAD_DIM + 1, :]
    o_ref[0, out_row:out_row + blk, :] = (acc[0:HEAD_DIM, :] * inv_l).T.astype(BF16)


def _moba_kernel(q_ref, k_ref, v_ref, o_ref, bias_ref, causal_ref, vt_ref, s_ref, m_ref,
                 shift_ref):
    qi = pl.program_id(2)
    nb = bias_ref.shape[0]

    @pl.when(qi == 0)
    def _():
        _moba_prepare(q_ref, k_ref, v_ref, bias_ref, causal_ref, vt_ref)
        _moba_scores(0, 0, q_ref, k_ref, bias_ref, causal_ref, s_ref, m_ref, shift_ref)

    for g in range(nb // MOBA_Q_BLOCKS_PER_STEP):
        @pl.when(qi == g)
        def _(g=g):
            for r in range(MOBA_Q_BLOCKS_PER_STEP):
                i = g * MOBA_Q_BLOCKS_PER_STEP + r
                if i + 1 < nb:
                    _moba_scores(i + 1, (i + 1) % 2, q_ref, k_ref, bias_ref, causal_ref,
                                 s_ref, m_ref, shift_ref)
                _moba_output(i, i % 2, r * MOBA_BLOCK, o_ref, vt_ref, s_ref, m_ref, shift_ref)


def _moba(z3):
    bsz, seq, _ = z3.shape
    nb = seq // MOBA_BLOCK
    q_col = D_LRU // HEAD_DIM
    k_col = (D_LRU + D_ATTN) // HEAD_DIM
    v_col = (D_LRU + 2 * D_ATTN) // HEAD_DIM
    ones_rows = 2 * SUBLANES
    return pl.pallas_call(
        _moba_kernel,
        grid=(bsz, N_HEADS, nb // MOBA_Q_BLOCKS_PER_STEP),
        in_specs=[
            pl.BlockSpec((1, seq, HEAD_DIM), lambda b, h, i: (b, 0, q_col + h)),
            pl.BlockSpec((1, seq, HEAD_DIM), lambda b, h, i: (b, 0, k_col + h)),
            pl.BlockSpec((1, seq, HEAD_DIM), lambda b, h, i: (b, 0, v_col + h)),
        ],
        out_specs=pl.BlockSpec((1, MOBA_Q_BLOCKS_PER_STEP * MOBA_BLOCK, HEAD_DIM),
                               lambda b, h, i: (b, i, h)),
        out_shape=jax.ShapeDtypeStruct((bsz, seq, D_ATTN), BF16),
        scratch_shapes=[
            pltpu.VMEM((nb, seq), F32),
            pltpu.VMEM((MOBA_BLOCK, MOBA_BLOCK), F32),
            pltpu.VMEM((HEAD_DIM + ones_rows, seq), BF16),
            pltpu.VMEM((2, seq, MOBA_BLOCK), F32),
            pltpu.VMEM((2, 1, MOBA_BLOCK), F32),
            pltpu.VMEM((2, nb, MOBA_BLOCK), F32),
        ],
        compiler_params=_params(("parallel", "parallel", "arbitrary")),
        name="moba_attention",
    )(z3, z3, z3)


def _mix_kernel(x_ref, g_ref, h_ref, o_ref, gate_ref, wl_ref, wa_ref, wo_ref, out_ref):
    for r in range(TM // ROWS_MIX):
        rows = slice(r * ROWS_MIX, (r + 1) * ROWS_MIX)
        ya = jnp.dot(h_ref[rows, :], wl_ref[...], preferred_element_type=F32)
        yb = jnp.dot(o_ref[rows, :], wa_ref[...], preferred_element_type=F32)
        mix = (gate_ref[rows, 0:D_MODEL].astype(F32) * ya
               + gate_ref[rows, D_MODEL:2 * D_MODEL].astype(F32) * yb)
        y = jnp.dot(mix.astype(BF16), wo_ref[...], preferred_element_type=F32)
        out_ref[rows, :] = x_ref[rows, :] + y * _rms_scale(y) * g_ref[...]


def _mix(x, g, h, o, z, w_lru_up, w_attn_up, w_out, layer):
    t = x.shape[0]
    gate_blk = (D_LRU + 3 * D_ATTN) // (2 * D_MODEL)
    resident = lambda rows, cols: pl.BlockSpec((None, rows, cols), lambda i: (layer, 0, 0),
                                               pipeline_mode=pl.Buffered(1))
    return pl.pallas_call(
        _mix_kernel,
        grid=(t // TM,),
        in_specs=[
            pl.BlockSpec((TM, D_MODEL), lambda i: (i, 0)),
            pl.BlockSpec((1, D_MODEL), lambda i: (0, 0)),
            pl.BlockSpec((TM, D_LRU), lambda i: (i, 0)),
            pl.BlockSpec((TM, D_ATTN), lambda i: (i, 0)),
            pl.BlockSpec((TM, 2 * D_MODEL), lambda i: (i, gate_blk)),
            resident(D_LRU, D_MODEL),
            resident(D_ATTN, D_MODEL),
            resident(D_MODEL, D_MODEL),
        ],
        out_specs=pl.BlockSpec((TM, D_MODEL), lambda i: (i, 0)),
        out_shape=jax.ShapeDtypeStruct((t, D_MODEL), F32),
        compiler_params=_params(("parallel",)),
        name="mixer_merge_out_projection",
    )(x, g, h, o, z, w_lru_up, w_attn_up, w_out)


def _rope_tables(seq):
    pos = jnp.arange(seq, dtype=F32)
    inv = ROPE_THETA ** (-jnp.arange(0, ROT_DIM, 2, dtype=F32) / ROT_DIM)
    ang = pos[:, None] * inv[None, :]
    cos2 = jnp.tile(jnp.cos(ang), (1, 2))
    sin2 = jnp.tile(jnp.sin(ang), (1, 2))
    cos_groups, sin_groups = [], []
    for m in range(HEADS_PER_PARTNER):
        before = m * ROT_DIM
        after = HEAD_DIM - before - ROT_DIM
        cos_groups += [jnp.ones((seq, before), F32), cos2, jnp.ones((seq, after), F32)]
        sin_groups += [jnp.zeros((seq, before), F32), sin2, jnp.zeros((seq, after), F32)]
    return jnp.concatenate(cos_groups, axis=1), jnp.concatenate(sin_groups, axis=1)


def _qk_weights(w_in):
    depth = w_in.shape[0]
    n_chunks = HEAD_DIM // ROT_DIM
    groups = N_HEADS // HEADS_PER_PARTNER
    w = w_in[:, :, D_LRU:D_LRU + 2 * D_ATTN].reshape(
        depth, D_MODEL, 2, groups, HEADS_PER_PARTNER, n_chunks, ROT_DIM)
    per_m = []
    for m in range(HEADS_PER_PARTNER):
        order = list(range(n_chunks))
        order[0], order[m] = order[m], order[0]
        per_m.append(jnp.concatenate([w[:, :, :, :, m, c:c + 1, :] for c in order], axis=-2))
    main = jnp.stack(per_m, axis=4).reshape(depth, D_MODEL, 2 * D_ATTN)
    rot = w[:, :, :, :, :, 0, :]
    partner = jnp.concatenate([-rot[..., ROT_HALF:], rot[..., :ROT_HALF]], axis=-1)
    partner = partner.reshape(depth, D_MODEL, 2 * PARTNER_COLS)
    return main.astype(BF16), partner.astype(BF16)


def _gate_weights(wa, wx):
    def pair_diag(w):
        depth = w.shape[0]
        w = w.reshape(depth, N_GATE_GROUPS, 2, LRU_BW, LRU_BW)
        z = jnp.zeros_like(w[:, :, 0])
        top = jnp.concatenate([w[:, :, 0], z], axis=-1)
        bot = jnp.concatenate([z, w[:, :, 1]], axis=-1)
        return jnp.concatenate([top, bot], axis=-2)
    return jnp.concatenate([pair_diag(wa), pair_diag(wx)], axis=-1).astype(BF16)


def kernel(x, norm_gains, ffn_w13, ffn_w2, w_in, b_gate, conv_w, conv_b, lru_wa, lru_ba,
           lru_wx, lru_bx, lru_lambda, w_lru_up, w_attn_up, w_out):
    bsz, seq, d = x.shape
    depth = norm_gains.shape[0]
    t = bsz * seq
    assert d == D_MODEL and seq % MOBA_BLOCK == 0 and seq % TM_IN == 0 and seq % TS_LRU == 0

    w13 = ffn_w13.astype(BF16)
    w2 = ffn_w2.astype(BF16)
    w_in_b = w_in.astype(BF16)
    w_qk, w_partner = _qk_weights(w_in)
    w_lru_up_b = w_lru_up.astype(BF16)
    w_attn_up_b = w_attn_up.astype(BF16)
    w_out_b = w_out.astype(BF16)
    w_gate = _gate_weights(lru_wa, lru_wx)
    b_gate3 = b_gate.reshape(depth, 1, 2 * D_MODEL)
    conv_b3 = conv_b.reshape(depth, 1, D_LRU)
    ba3 = lru_ba.reshape(depth, 1, D_LRU)
    bx3 = lru_bx.reshape(depth, 1, D_LRU)
    lam3 = lru_lambda.reshape(depth, 1, D_LRU)
    cos_t, sin_t = _rope_tables(seq)

    xs = x.reshape(t, d)
    for l in range(depth):
        gain = lambda k: norm_gains[l, k].reshape(1, d)
        xs, xn = _ffn(xs, gain(0), gain(1), gain(2), w13, w2, l, 0)
        z = _inproj(xn, w_in_b, w_qk, w_partner, b_gate3, cos_t, sin_t, l, seq)
        z3 = z.reshape(bsz, seq, IN_COLS)
        h = _lru(z3, conv_w, conv_b3, w_gate, ba3, bx3, lam3, l)
        o = _moba(z3)
        xs = _mix(xs, gain(3), h.reshape(t, D_LRU), o.reshape(t, D_ATTN), z,
                  w_lru_up_b, w_attn_up_b, w_out_b, l)
        xs = _ffn(xs, gain(4), gain(5), None, w13, w2, l, 1)
    return xs.reshape(bsz, seq, d)
```

```python
import jax
import jax.numpy as jnp
from jax import lax
from jax.experimental import pallas as pl
from jax.experimental.pallas import tpu as pltpu

F32 = jnp.float32
BF16 = jnp.bfloat16

D_MODEL = 2048
D_LRU = 1024
LRU_BLOCKS = 16
LRU_BW = D_LRU // LRU_BLOCKS
CONV_WIDTH = 4
LRU_C = 8.0
N_HEADS = 8
HEAD_DIM = 128
D_ATTN = N_HEADS * HEAD_DIM
MOBA_BLOCK = 256
MOBA_TOPK = 3
ROPE_THETA = 500000.0
ROT_DIM = HEAD_DIM // 4
ROT_HALF = ROT_DIM // 2
D_FF = 5632
NORM_EPS = 1e-6
IN_COLS = D_LRU + 3 * D_ATTN + 2 * D_MODEL

SUBLANES = 8
VMEM_LIMIT_BYTES = 56 * 1024 * 1024

TM = 512
TM_FFN = 1024
ROWS_FFN = 512
TF = 256
TM_IN = 1024
TN_IN = 1024
ROWS_MIX = 256
TS_LRU = 512
MOBA_Q_BLOCKS_PER_STEP = 4
GATE_GROUP = 2 * LRU_BW
N_GATE_GROUPS = D_LRU // GATE_GROUP


def _rms_scale(x):
    return lax.rsqrt(jnp.mean(x * x, axis=-1, keepdims=True) + NORM_EPS)


def _params(semantics):
    return pltpu.CompilerParams(dimension_semantics=semantics,
                                vmem_limit_bytes=VMEM_LIMIT_BYTES)


def _ffn_kernel(x_ref, gin_ref, gout_ref, gnext_ref, wa_ref, wb_ref, w2_ref, o_ref, *rest):
    next_ref = rest[0] if len(rest) == 2 else None
    xn_ref = rest[-1]
    j = pl.program_id(1)

    @pl.when(j == 0)
    def _():
        x = x_ref[...]
        xn_ref[...] = (x * _rms_scale(x) * gin_ref[...]).astype(BF16)
        o_ref[...] = jnp.zeros_like(o_ref)

    for r in range(TM_FFN // ROWS_FFN):
        rows = slice(r * ROWS_FFN, (r + 1) * ROWS_FFN)
        xn = xn_ref[rows, :]
        a = jnp.dot(xn, wa_ref[...], preferred_element_type=F32)
        b = jnp.dot(xn, wb_ref[...], preferred_element_type=F32)
        h = (a * jax.nn.sigmoid(a) * b).astype(BF16)
        o_ref[rows, :] += jnp.dot(h, w2_ref[...], preferred_element_type=F32)

    @pl.when(j == pl.num_programs(1) - 1)
    def _():
        y = o_ref[...]
        out = x_ref[...] + 0.5 * (y * _rms_scale(y) * gout_ref[...])
        o_ref[...] = out
        if next_ref is not None:
            next_ref[...] = (out * _rms_scale(out) * gnext_ref[...]).astype(BF16)


def _ffn(x, gin, gout, gnext, w13, w2, layer, half):
    t = x.shape[0]
    nf = D_FF // TF
    row_block = pl.BlockSpec((TM_FFN, D_MODEL), lambda i, j: (i, 0))
    gain_block = pl.BlockSpec((1, D_MODEL), lambda i, j: (0, 0))
    out_specs, out_shape = row_block, jax.ShapeDtypeStruct((t, D_MODEL), F32)
    if gnext is not None:
        out_specs = [row_block, row_block]
        out_shape = [out_shape, jax.ShapeDtypeStruct((t, D_MODEL), BF16)]
    return pl.pallas_call(
        _ffn_kernel,
        grid=(t // TM_FFN, nf),
        in_specs=[
            row_block, gain_block, gain_block, gain_block,
            pl.BlockSpec((None, None, D_MODEL, TF), lambda i, j: (layer, half, 0, j)),
            pl.BlockSpec((None, None, D_MODEL, TF), lambda i, j: (layer, half, 0, j + nf)),
            pl.BlockSpec((None, None, TF, D_MODEL), lambda i, j: (layer, half, j, 0)),
        ],
        out_specs=out_specs,
        out_shape=out_shape,
        scratch_shapes=[pltpu.VMEM((TM_FFN, D_MODEL), BF16)],
        compiler_params=_params(("parallel", "arbitrary")),
        name="ffn_half_step",
    )(x, gin, gout, gout if gnext is None else gnext, w13, w13, w2)


Q_SCALE = HEAD_DIM ** -0.5 * 1.4426950408889634
HEADS_PER_PARTNER = HEAD_DIM // ROT_DIM
PARTNER_COLS = N_HEADS * ROT_DIM


def _rotary(acc, partner, cos_ref, sin_ref):
    heads = []
    for hd in range(N_HEADS):
        m = hd % HEADS_PER_PARTNER
        c = hd // HEADS_PER_PARTNER
        xh = acc[:, hd * HEAD_DIM:(hd + 1) * HEAD_DIM]
        ph = partner[:, c * HEAD_DIM:(c + 1) * HEAD_DIM]
        lanes = slice(m * HEAD_DIM, (m + 1) * HEAD_DIM)
        heads.append(xh * cos_ref[:, lanes] + ph * sin_ref[:, lanes])
    return jnp.concatenate(heads, axis=1)


def _inproj_kernel(xn_ref, w_ref, wqk_ref, wp_ref, bg_ref, cos_ref, sin_ref, z_ref):
    j = pl.program_id(1)
    q_blk = D_LRU // TN_IN
    k_blk = (D_LRU + D_ATTN) // TN_IN
    v_blk = (D_LRU + 2 * D_ATTN) // TN_IN
    gate_blk = (D_LRU + 3 * D_ATTN) // TN_IN

    def project(w):
        return jnp.dot(xn_ref[...], w[...], preferred_element_type=F32)

    @pl.when((j < q_blk) | (j == v_blk))
    def _():
        z_ref[...] = project(w_ref).astype(BF16)

    @pl.when(j == q_blk)
    def _():
        rot = _rotary(project(wqk_ref), project(wp_ref), cos_ref, sin_ref)
        z_ref[...] = (rot * Q_SCALE).astype(BF16)

    @pl.when(j == k_blk)
    def _():
        z_ref[...] = _rotary(project(wqk_ref), project(wp_ref), cos_ref, sin_ref).astype(BF16)

    @pl.when(j >= gate_blk)
    def _():
        z_ref[...] = jax.nn.sigmoid(project(w_ref) + bg_ref[...]).astype(BF16)


def _inproj(xn, w_in, w_qk, w_partner, b_gate, cos_t, sin_t, layer, seq):
    t = xn.shape[0]
    q_blk = D_LRU // TN_IN
    k_blk = (D_LRU + D_ATTN) // TN_IN
    gate_blk = (D_LRU + 3 * D_ATTN) // TN_IN
    seq_tiles = seq // TM_IN
    is_k = lambda j: (j == k_blk).astype(jnp.int32)
    main_blk = lambda j: jnp.where((j == q_blk) | (j == k_blk), q_blk - 1, j)
    table = lambda: pl.BlockSpec((TM_IN, HEADS_PER_PARTNER * HEAD_DIM),
                                 lambda i, j: (i % seq_tiles, 0))
    return pl.pallas_call(
        _inproj_kernel,
        grid=(t // TM_IN, IN_COLS // TN_IN),
        in_specs=[
            pl.BlockSpec((TM_IN, D_MODEL), lambda i, j: (i, 0)),
            pl.BlockSpec((None, D_MODEL, TN_IN), lambda i, j: (layer, 0, main_blk(j))),
            pl.BlockSpec((None, D_MODEL, D_ATTN), lambda i, j: (layer, 0, is_k(j))),
            pl.BlockSpec((None, D_MODEL, PARTNER_COLS), lambda i, j: (layer, 0, is_k(j))),
            pl.BlockSpec((None, 1, TN_IN), lambda i, j: (layer, 0, jnp.maximum(j - gate_blk, 0))),
            table(), table(),
        ],
        out_specs=pl.BlockSpec((TM_IN, TN_IN), lambda i, j: (i, j)),
        out_shape=jax.ShapeDtypeStruct((t, IN_COLS), BF16),
        compiler_params=_params(("parallel", "arbitrary")),
        name="mixer_in_projection",
    )(xn, w_in, w_qk, w_partner, b_gate, cos_t, sin_t)


def _lru_kernel(u_ref, cw_ref, cb_ref, wg_ref, ba_ref, bx_ref, lam_ref, h_ref,
                ubuf_ref, a_ref, b_ref, hout_ref, carry_ref):
    ts = TS_LRU
    pad = SUBLANES

    @pl.when(pl.program_id(1) == 0)
    def _():
        ubuf_ref[0:pad, :] = jnp.zeros((pad, D_LRU), F32)
        carry_ref[...] = jnp.zeros_like(carry_ref)

    ubuf_ref[pad:pad + ts, :] = u_ref[0].astype(F32)
    xc = cb_ref[...]
    for tap in range(CONV_WIDTH):
        off = pad - (CONV_WIDTH - 1) + tap
        xc = xc + ubuf_ref[off:off + ts, :] * cw_ref[tap:tap + 1, :]
    ubuf_ref[0:pad, :] = ubuf_ref[ts:ts + pad, :]

    neg_c_softplus = -LRU_C * jax.nn.softplus(-lam_ref[...])
    for g in range(N_GATE_GROUPS):
        cols = slice(g * GATE_GROUP, (g + 1) * GATE_GROUP)
        xg = xc[:, cols]
        lin = jnp.dot(xg.astype(BF16), wg_ref[g], preferred_element_type=F32)
        r = jax.nn.sigmoid(lin[:, :GATE_GROUP] + ba_ref[:, cols])
        i = jax.nn.sigmoid(lin[:, GATE_GROUP:] + bx_ref[:, cols])
        log_a = r * neg_c_softplus[:, cols]
        a = jnp.exp(log_a)
        a_ref[:, cols] = a
        b_ref[:, cols] = jnp.sqrt(1.0 - a * a) * (i * xg)

    def step(t, h):
        h = a_ref[pl.ds(t, 1), :] * h + b_ref[pl.ds(t, 1), :]
        hout_ref[pl.ds(t, 1), :] = h
        return h

    carry_ref[...] = lax.fori_loop(0, ts, step, carry_ref[...], unroll=8)
    h_ref[0] = hout_ref[...].astype(BF16)


def _lru(z3, conv_w, conv_b, w_gate, ba, bx, lam, layer):
    bsz, seq, _ = z3.shape
    vec = lambda: pl.BlockSpec((None, 1, D_LRU), lambda b, t: (layer, 0, 0))
    return pl.pallas_call(
        _lru_kernel,
        grid=(bsz, seq // TS_LRU),
        in_specs=[
            pl.BlockSpec((1, TS_LRU, D_LRU), lambda b, t: (b, t, 0)),
            pl.BlockSpec((None, CONV_WIDTH, D_LRU), lambda b, t: (layer, 0, 0)),
            vec(),
            pl.BlockSpec((None, N_GATE_GROUPS, GATE_GROUP, 2 * GATE_GROUP),
                         lambda b, t: (layer, 0, 0, 0)),
            vec(), vec(), vec(),
        ],
        out_specs=pl.BlockSpec((1, TS_LRU, D_LRU), lambda b, t: (b, t, 0)),
        out_shape=jax.ShapeDtypeStruct((bsz, seq, D_LRU), BF16),
        scratch_shapes=[
            pltpu.VMEM((SUBLANES + TS_LRU, D_LRU), F32),
            pltpu.VMEM((TS_LRU, D_LRU), F32),
            pltpu.VMEM((TS_LRU, D_LRU), F32),
            pltpu.VMEM((TS_LRU, D_LRU), F32),
            pltpu.VMEM((1, D_LRU), F32),
        ],
        compiler_params=_params(("parallel", "arbitrary")),
        name="rg_lru_branch",
    )(z3, conv_w, conv_b, w_gate, ba, bx, lam)


def _nt_dot(a, b):
    return lax.dot_general(a, b, (((1,), (1,)), ((), ())), preferred_element_type=F32)


def _moba_prepare(q_ref, k_ref, v_ref, bias_ref, causal_ref, vt_ref):
    nb, seq = bias_ref.shape
    blk = MOBA_BLOCK
    neg = -jnp.inf
    kmean = jnp.mean(k_ref[0].astype(F32).reshape(nb, blk, HEAD_DIM), axis=1)
    hi = kmean.astype(BF16)
    lo = (kmean - hi.astype(F32)).astype(BF16)
    q_all = q_ref[0]
    gate = _nt_dot(hi, q_all) + _nt_dot(lo, q_all)
    blk_id = lax.broadcasted_iota(jnp.int32, (nb, seq), 0)
    qry_blk = lax.broadcasted_iota(jnp.int32, (nb, seq), 1) // blk
    gate = jnp.where(blk_id < qry_blk, gate, neg)
    bias = jnp.full((nb, seq), neg, F32)
    for _ in range(MOBA_TOPK):
        top = jnp.max(gate, axis=0, keepdims=True)
        first = jnp.min(jnp.where(gate == top, blk_id, nb), axis=0, keepdims=True)
        pick = (blk_id == first) & (top > neg)
        bias = jnp.where(pick, 0.0, bias)
        gate = jnp.where(pick, neg, gate)
    bias_ref[...] = bias
    key_pos = lax.broadcasted_iota(jnp.int32, (blk, blk), 0)
    qry_pos = lax.broadcasted_iota(jnp.int32, (blk, blk), 1)
    causal_ref[...] = jnp.where(key_pos <= qry_pos, 0.0, neg)
    for n in range(nb):
        cols = slice(n * blk, (n + 1) * blk)
        vt_ref[0:HEAD_DIM, cols] = v_ref[0, cols, :].astype(F32).T.astype(BF16)
    vt_ref[HEAD_DIM:, :] = jnp.ones((vt_ref.shape[0] - HEAD_DIM, seq), BF16)


def _moba_scores(j, slot, q_ref, k_ref, bias_ref, causal_ref, s_ref, m_ref, shift_ref):
    blk = MOBA_BLOCK
    lo, hi = j * blk, (j + 1) * blk
    q = q_ref[0, lo:hi, :]
    s_own = _nt_dot(k_ref[0, lo:hi, :], q) + causal_ref[...]
    s_ref[slot, lo:hi, :] = s_own
    m = jnp.max(s_own, axis=0, keepdims=True)
    if j > 0:
        s_past = _nt_dot(k_ref[0, 0:lo, :], q)
        s_ref[slot, 0:lo, :] = s_past
        bias = bias_ref[0:j, lo:hi]
        blk_max = jnp.max(s_past.reshape(j, blk, blk), axis=1)
        m = jnp.maximum(m, jnp.max(blk_max + bias, axis=0, keepdims=True))
        shift_ref[slot, 0:j, :] = m - bias
    m_ref[slot] = m


def _moba_output(i, slot, out_row, o_ref, vt_ref, s_ref, m_ref, shift_ref):
    blk = MOBA_BLOCK
    lo, hi = i * blk, (i + 1) * blk
    p_own = jnp.exp2(s_ref[slot, lo:hi, :] - m_ref[slot])
    acc = jnp.dot(vt_ref[:, lo:hi], p_own.astype(BF16), preferred_element_type=F32)
    if i > 0:
        s_past = s_ref[slot, 0:lo, :].reshape(i, blk, blk)
        p_past = jnp.exp2(s_past - shift_ref[slot, 0:i, :][:, None, :]).reshape(lo, blk)
        acc = acc + jnp.dot(vt_ref[:, 0:lo], p_past.astype(BF16), preferred_element_type=F32)
    inv_l = 1.0 / acc[HEAD_DIM:HEAD_DIM + 1, :]
    o_ref[0, out_row:out_row + blk, :] = (acc[0:HEAD_DIM, :] * inv_l).T.astype(BF16)


def _moba_kernel(q_ref, k_ref, v_ref, o_ref, bias_ref, causal_ref, vt_ref, s_ref, m_ref,
                 shift_ref):
    qi = pl.program_id(2)
    nb = bias_ref.shape[0]

    @pl.when(qi == 0)
    def _():
        _moba_prepare(q_ref, k_ref, v_ref, bias_ref, causal_ref, vt_ref)
        _moba_scores(0, 0, q_ref, k_ref, bias_ref, causal_ref, s_ref, m_ref, shift_ref)

    for g in range(nb // MOBA_Q_BLOCKS_PER_STEP):
        @pl.when(qi == g)
        def _(g=g):
            for r in range(MOBA_Q_BLOCKS_PER_STEP):
                i = g * MOBA_Q_BLOCKS_PER_STEP + r
                if i + 1 < nb:
                    _moba_scores(i + 1, (i + 1) % 2, q_ref, k_ref, bias_ref, causal_ref,
                                 s_ref, m_ref, shift_ref)
                _moba_output(i, i % 2, r * MOBA_BLOCK, o_ref, vt_ref, s_ref, m_ref, shift_ref)


def _moba(z3):
    bsz, seq, _ = z3.shape
    nb = seq // MOBA_BLOCK
    q_col = D_LRU // HEAD_DIM
    k_col = (D_LRU + D_ATTN) // HEAD_DIM
    v_col = (D_LRU + 2 * D_ATTN) // HEAD_DIM
    ones_rows = 2 * SUBLANES
    return pl.pallas_call(
        _moba_kernel,
        grid=(bsz, N_HEADS, nb // MOBA_Q_BLOCKS_PER_STEP),
        in_specs=[
            pl.BlockSpec((1, seq, HEAD_DIM), lambda b, h, i: (b, 0, q_col + h)),
            pl.BlockSpec((1, seq, HEAD_DIM), lambda b, h, i: (b, 0, k_col + h)),
            pl.BlockSpec((1, seq, HEAD_DIM), lambda b, h, i: (b, 0, v_col + h)),
        ],
        out_specs=pl.BlockSpec((1, MOBA_Q_BLOCKS_PER_STEP * MOBA_BLOCK, HEAD_DIM),
                               lambda b, h, i: (b, i, h)),
        out_shape=jax.ShapeDtypeStruct((bsz, seq, D_ATTN), BF16),
        scratch_shapes=[
            pltpu.VMEM((nb, seq), F32),
            pltpu.VMEM((MOBA_BLOCK, MOBA_BLOCK), F32),
            pltpu.VMEM((HEAD_DIM + ones_rows, seq), BF16),
            pltpu.VMEM((2, seq, MOBA_BLOCK), F32),
            pltpu.VMEM((2, 1, MOBA_BLOCK), F32),
            pltpu.VMEM((2, nb, MOBA_BLOCK), F32),
        ],
        compiler_params=_params(("parallel", "parallel", "arbitrary")),
        name="moba_attention",
    )(z3, z3, z3)


def _mix_kernel(x_ref, g_ref, h_ref, o_ref, gate_ref, wl_ref, wa_ref, wo_ref, out_ref):
    for r in range(TM // ROWS_MIX):
        rows = slice(r * ROWS_MIX, (r + 1) * ROWS_MIX)
        ya = jnp.dot(h_ref[rows, :], wl_ref[...], preferred_element_type=F32)
        yb = jnp.dot(o_ref[rows, :], wa_ref[...], preferred_element_type=F32)
        mix = (gate_ref[rows, 0:D_MODEL].astype(F32) * ya
               + gate_ref[rows, D_MODEL:2 * D_MODEL].astype(F32) * yb)
        y = jnp.dot(mix.astype(BF16), wo_ref[...], preferred_element_type=F32)
        out_ref[rows, :] = x_ref[rows, :] + y * _rms_scale(y) * g_ref[...]


def _mix(x, g, h, o, z, w_lru_up, w_attn_up, w_out, layer):
    t = x.shape[0]
    gate_blk = (D_LRU + 3 * D_ATTN) // (2 * D_MODEL)
    resident = lambda rows, cols: pl.BlockSpec((None, rows, cols), lambda i: (layer, 0, 0),
                                               pipeline_mode=pl.Buffered(1))
    return pl.pallas_call(
        _mix_kernel,
        grid=(t // TM,),
        in_specs=[
            pl.BlockSpec((TM, D_MODEL), lambda i: (i, 0)),
            pl.BlockSpec((1, D_MODEL), lambda i: (0, 0)),
            pl.BlockSpec((TM, D_LRU), lambda i: (i, 0)),
            pl.BlockSpec((TM, D_ATTN), lambda i: (i, 0)),
            pl.BlockSpec((TM, 2 * D_MODEL), lambda i: (i, gate_blk)),
            resident(D_LRU, D_MODEL),
            resident(D_ATTN, D_MODEL),
            resident(D_MODEL, D_MODEL),
        ],
        out_specs=pl.BlockSpec((TM, D_MODEL), lambda i: (i, 0)),
        out_shape=jax.ShapeDtypeStruct((t, D_MODEL), F32),
        compiler_params=_params(("parallel",)),
        name="mixer_merge_out_projection",
    )(x, g, h, o, z, w_lru_up, w_attn_up, w_out)


def _rope_tables(seq):
    pos = jnp.arange(seq, dtype=F32)
    inv = ROPE_THETA ** (-jnp.arange(0, ROT_DIM, 2, dtype=F32) / ROT_DIM)
    ang = pos[:, None] * inv[None, :]
    cos2 = jnp.tile(jnp.cos(ang), (1, 2))
    sin2 = jnp.tile(jnp.sin(ang), (1, 2))
    cos_groups, sin_groups = [], []
    for m in range(HEADS_PER_PARTNER):
        before = m * ROT_DIM
        after = HEAD_DIM - before - ROT_DIM
        cos_groups += [jnp.ones((seq, before), F32), cos2, jnp.ones((seq, after), F32)]
        sin_groups += [jnp.zeros((seq, before), F32), sin2, jnp.zeros((seq, after), F32)]
    return jnp.concatenate(cos_groups, axis=1), jnp.concatenate(sin_groups, axis=1)


def _qk_weights(w_in):
    depth = w_in.shape[0]
    n_chunks = HEAD_DIM // ROT_DIM
    groups = N_HEADS // HEADS_PER_PARTNER
    w = w_in[:, :, D_LRU:D_LRU + 2 * D_ATTN].reshape(
        depth, D_MODEL, 2, groups, HEADS_PER_PARTNER, n_chunks, ROT_DIM)
    per_m = []
    for m in range(HEADS_PER_PARTNER):
        order = list(range(n_chunks))
        order[0], order[m] = order[m], order[0]
        per_m.append(jnp.concatenate([w[:, :, :, :, m, c:c + 1, :] for c in order], axis=-2))
    main = jnp.stack(per_m, axis=4).reshape(depth, D_MODEL, 2 * D_ATTN)
    rot = w[:, :, :, :, :, 0, :]
    partner = jnp.concatenate([-rot[..., ROT_HALF:], rot[..., :ROT_HALF]], axis=-1)
    partner = partner.reshape(depth, D_MODEL, 2 * PARTNER_COLS)
    return main.astype(BF16), partner.astype(BF16)


def _gate_weights(wa, wx):
    def pair_diag(w):
        depth = w.shape[0]
        w = w.reshape(depth, N_GATE_GROUPS, 2, LRU_BW, LRU_BW)
        z = jnp.zeros_like(w[:, :, 0])
        top = jnp.concatenate([w[:, :, 0], z], axis=-1)
        bot = jnp.concatenate([z, w[:, :, 1]], axis=-1)
        return jnp.concatenate([top, bot], axis=-2)
    return jnp.concatenate([pair_diag(wa), pair_diag(wx)], axis=-1).astype(BF16)


def kernel(x, norm_gains, ffn_w13, ffn_w2, w_in, b_gate, conv_w, conv_b, lru_wa, lru_ba,
           lru_wx, lru_bx, lru_lambda, w_lru_up, w_attn_up, w_out):
    bsz, seq, d = x.shape
    depth = norm_gains.shape[0]
    t = bsz * seq
    assert d == D_MODEL and seq % MOBA_BLOCK == 0 and seq % TM_IN == 0 and seq % TS_LRU == 0

    w13 = ffn_w13.astype(BF16)
    w2 = ffn_w2.astype(BF16)
    w_in_b = w_in.astype(BF16)
    w_qk, w_partner = _qk_weights(w_in)
    w_lru_up_b = w_lru_up.astype(BF16)
    w_attn_up_b = w_attn_up.astype(BF16)
    w_out_b = w_out.astype(BF16)
    w_gate = _gate_weights(lru_wa, lru_wx)
    b_gate3 = b_gate.reshape(depth, 1, 2 * D_MODEL)
    conv_b3 = conv_b.reshape(depth, 1, D_LRU)
    ba3 = lru_ba.reshape(depth, 1, D_LRU)
    bx3 = lru_bx.reshape(depth, 1, D_LRU)
    lam3 = lru_lambda.reshape(depth, 1, D_LRU)
    cos_t, sin_t = _rope_tables(seq)

    xs = x.reshape(t, d)
    for l in range(depth):
        gain = lambda k: norm_gains[l, k].reshape(1, d)
        xs, xn = _ffn(xs, gain(0), gain(1), gain(2), w13, w2, l, 0)
        z = _inproj(xn, w_in_b, w_qk, w_partner, b_gate3, cos_t, sin_t, l, seq)
        z3 = z.reshape(bsz, seq, IN_COLS)
        h = _lru(z3, conv_w, conv_b3, w_gate, ba3, bx3, lam3, l)
        o = _moba(z3)
        xs = _mix(xs, gain(3), h.reshape(t, D_LRU), o.reshape(t, D_ATTN), z,
                  w_lru_up_b, w_attn_up_b, w_out_b, l)
        xs = _ffn(xs, gain(4), gain(5), None, w13, w2, l, 1)
    return xs.reshape(bsz, seq, d)
```

```python
import jax
import jax.numpy as jnp
from jax import lax
from jax.experimental import pallas as pl
from jax.experimental.pallas import tpu as pltpu

F32 = jnp.float32
BF16 = jnp.bfloat16

D_MODEL = 2048
D_LRU = 1024
LRU_BLOCKS = 16
LRU_BW = D_LRU // LRU_BLOCKS
CONV_WIDTH = 4
LRU_C = 8.0
N_HEADS = 8
HEAD_DIM = 128
D_ATTN = N_HEADS * HEAD_DIM
MOBA_BLOCK = 256
MOBA_TOPK = 3
ROPE_THETA = 500000.0
ROT_DIM = HEAD_DIM // 4
ROT_HALF = ROT_DIM // 2
D_FF = 5632
NORM_EPS = 1e-6
IN_COLS = D_LRU + 3 * D_ATTN + 2 * D_MODEL

SUBLANES = 8
VMEM_LIMIT_BYTES = 60 * 1024 * 1024

TM = 512
TM_FFN = 1024
ROWS_FFN = 512
TF = 512
TM_IN = 1024
TN_IN = 1024
ROWS_MIX = 256
TS_LRU = 512
MOBA_Q_BLOCKS_PER_STEP = 4
GATE_GROUP = 2 * LRU_BW
N_GATE_GROUPS = D_LRU // GATE_GROUP


def _rms_scale(x):
    return lax.rsqrt(jnp.mean(x * x, axis=-1, keepdims=True) + NORM_EPS)


def _params(semantics):
    return pltpu.CompilerParams(dimension_semantics=semantics,
                                vmem_limit_bytes=VMEM_LIMIT_BYTES)


def _ffn_kernel(x_ref, gin_ref, gout_ref, gnext_ref, wa_ref, wb_ref, w2_ref, o_ref, *rest):
    next_ref = rest[0] if len(rest) == 2 else None
    xn_ref = rest[-1]
    j = pl.program_id(1)

    @pl.when(j == 0)
    def _():
        x = x_ref[...]
        xn_ref[...] = (x * _rms_scale(x) * gin_ref[...]).astype(BF16)
        o_ref[...] = jnp.zeros_like(o_ref)

    for r in range(TM_FFN // ROWS_FFN):
        rows = slice(r * ROWS_FFN, (r + 1) * ROWS_FFN)
        xn = xn_ref[rows, :]
        a = jnp.dot(xn, wa_ref[...], preferred_element_type=F32)
        b = jnp.dot(xn, wb_ref[...], preferred_element_type=F32)
        h = (a * jax.nn.sigmoid(a) * b).astype(BF16)
        o_ref[rows, :] += jnp.dot(h, w2_ref[...], preferred_element_type=F32)

    @pl.when(j == pl.num_programs(1) - 1)
    def _():
        y = o_ref[...]
        out = x_ref[...] + 0.5 * (y * _rms_scale(y) * gout_ref[...])
        o_ref[...] = out
        if next_ref is not None:
            next_ref[...] = (out * _rms_scale(out) * gnext_ref[...]).astype(BF16)


def _ffn(x, gin, gout, gnext, w13, w2, layer, half):
    t = x.shape[0]
    nf = D_FF // TF
    row_block = pl.BlockSpec((TM_FFN, D_MODEL), lambda i, j: (i, 0))
    gain_block = pl.BlockSpec((1, D_MODEL), lambda i, j: (0, 0))
    out_specs, out_shape = row_block, jax.ShapeDtypeStruct((t, D_MODEL), F32)
    if gnext is not None:
        out_specs = [row_block, row_block]
        out_shape = [out_shape, jax.ShapeDtypeStruct((t, D_MODEL), BF16)]
    return pl.pallas_call(
        _ffn_kernel,
        grid=(t // TM_FFN, nf),
        in_specs=[
            row_block, gain_block, gain_block, gain_block,
            pl.BlockSpec((None, None, D_MODEL, TF), lambda i, j: (layer, half, 0, j)),
            pl.BlockSpec((None, None, D_MODEL, TF), lambda i, j: (layer, half, 0, j + nf)),
            pl.BlockSpec((None, None, TF, D_MODEL), lambda i, j: (layer, half, j, 0)),
        ],
        out_specs=out_specs,
        out_shape=out_shape,
        scratch_shapes=[pltpu.VMEM((TM_FFN, D_MODEL), BF16)],
        compiler_params=_params(("parallel", "arbitrary")),
        name="ffn_half_step",
    )(x, gin, gout, gout if gnext is None else gnext, w13, w13, w2)


Q_SCALE = HEAD_DIM ** -0.5 * 1.4426950408889634
HEADS_PER_PARTNER = HEAD_DIM // ROT_DIM
PARTNER_COLS = N_HEADS * ROT_DIM


def _rotary(acc, partner, cos_ref, sin_ref):
    heads = []
    for hd in range(N_HEADS):
        m = hd % HEADS_PER_PARTNER
        c = hd // HEADS_PER_PARTNER
        xh = acc[:, hd * HEAD_DIM:(hd + 1) * HEAD_DIM]
        ph = partner[:, c * HEAD_DIM:(c + 1) * HEAD_DIM]
        lanes = slice(m * HEAD_DIM, (m + 1) * HEAD_DIM)
        heads.append(xh * cos_ref[:, lanes] + ph * sin_ref[:, lanes])
    return jnp.concatenate(heads, axis=1)


def _inproj_kernel(xn_ref, w_ref, wqk_ref, wp_ref, bg_ref, cos_ref, sin_ref, z_ref):
    j = pl.program_id(1)
    q_blk = D_LRU // TN_IN
    k_blk = (D_LRU + D_ATTN) // TN_IN
    v_blk = (D_LRU + 2 * D_ATTN) // TN_IN
    gate_blk = (D_LRU + 3 * D_ATTN) // TN_IN

    def project(w):
        return jnp.dot(xn_ref[...], w[...], preferred_element_type=F32)

    @pl.when((j < q_blk) | (j == v_blk))
    def _():
        z_ref[...] = project(w_ref).astype(BF16)

    @pl.when(j == q_blk)
    def _():
        rot = _rotary(project(wqk_ref), project(wp_ref), cos_ref, sin_ref)
        z_ref[...] = (rot * Q_SCALE).astype(BF16)

    @pl.when(j == k_blk)
    def _():
        z_ref[...] = _rotary(project(wqk_ref), project(wp_ref), cos_ref, sin_ref).astype(BF16)

    @pl.when(j >= gate_blk)
    def _():
        z_ref[...] = jax.nn.sigmoid(project(w_ref) + bg_ref[...]).astype(BF16)


def _inproj(xn, w_in, w_qk, w_partner, b_gate, cos_t, sin_t, layer, seq):
    t = xn.shape[0]
    q_blk = D_LRU // TN_IN
    k_blk = (D_LRU + D_ATTN) // TN_IN
    gate_blk = (D_LRU + 3 * D_ATTN) // TN_IN
    seq_tiles = seq // TM_IN
    is_k = lambda j: (j == k_blk).astype(jnp.int32)
    main_blk = lambda j: jnp.where((j == q_blk) | (j == k_blk), q_blk - 1, j)
    table = lambda: pl.BlockSpec((TM_IN, HEADS_PER_PARTNER * HEAD_DIM),
                                 lambda i, j: (i % seq_tiles, 0))
    return pl.pallas_call(
        _inproj_kernel,
        grid=(t // TM_IN, IN_COLS // TN_IN),
        in_specs=[
            pl.BlockSpec((TM_IN, D_MODEL), lambda i, j: (i, 0)),
            pl.BlockSpec((None, D_MODEL, TN_IN), lambda i, j: (layer, 0, main_blk(j))),
            pl.BlockSpec((None, D_MODEL, D_ATTN), lambda i, j: (layer, 0, is_k(j))),
            pl.BlockSpec((None, D_MODEL, PARTNER_COLS), lambda i, j: (layer, 0, is_k(j))),
            pl.BlockSpec((None, 1, TN_IN), lambda i, j: (layer, 0, jnp.maximum(j - gate_blk, 0))),
            table(), table(),
        ],
        out_specs=pl.BlockSpec((TM_IN, TN_IN), lambda i, j: (i, j)),
        out_shape=jax.ShapeDtypeStruct((t, IN_COLS), BF16),
        compiler_params=_params(("parallel", "arbitrary")),
        name="mixer_in_projection",
    )(xn, w_in, w_qk, w_partner, b_gate, cos_t, sin_t)


def _lru_kernel(u_ref, cw_ref, cb_ref, wg_ref, ba_ref, bx_ref, lam_ref, h_ref,
                ubuf_ref, a_ref, b_ref, hout_ref, carry_ref):
    ts = TS_LRU
    pad = SUBLANES

    @pl.when(pl.program_id(1) == 0)
    def _():
        ubuf_ref[0:pad, :] = jnp.zeros((pad, D_LRU), F32)
        carry_ref[...] = jnp.zeros_like(carry_ref)

    ubuf_ref[pad:pad + ts, :] = u_ref[0].astype(F32)
    xc = cb_ref[...]
    for tap in range(CONV_WIDTH):
        off = pad - (CONV_WIDTH - 1) + tap
        xc = xc + ubuf_ref[off:off + ts, :] * cw_ref[tap:tap + 1, :]
    ubuf_ref[0:pad, :] = ubuf_ref[ts:ts + pad, :]

    neg_c_softplus = -LRU_C * jax.nn.softplus(-lam_ref[...])
    for g in range(N_GATE_GROUPS):
        cols = slice(g * GATE_GROUP, (g + 1) * GATE_GROUP)
        xg = xc[:, cols]
        lin = jnp.dot(xg.astype(BF16), wg_ref[g], preferred_element_type=F32)
        r = jax.nn.sigmoid(lin[:, :GATE_GROUP] + ba_ref[:, cols])
        i = jax.nn.sigmoid(lin[:, GATE_GROUP:] + bx_ref[:, cols])
        log_a = r * neg_c_softplus[:, cols]
        a = jnp.exp(log_a)
        a_ref[:, cols] = a
        b_ref[:, cols] = jnp.sqrt(1.0 - a * a) * (i * xg)

    def step(t, h):
        h = a_ref[pl.ds(t, 1), :] * h + b_ref[pl.ds(t, 1), :]
        hout_ref[pl.ds(t, 1), :] = h
        return h

    carry_ref[...] = lax.fori_loop(0, ts, step, carry_ref[...], unroll=8)
    h_ref[0] = hout_ref[...].astype(BF16)


def _lru(z3, conv_w, conv_b, w_gate, ba, bx, lam, layer):
    bsz, seq, _ = z3.shape
    vec = lambda: pl.BlockSpec((None, 1, D_LRU), lambda b, t: (layer, 0, 0))
    return pl.pallas_call(
        _lru_kernel,
        grid=(bsz, seq // TS_LRU),
        in_specs=[
            pl.BlockSpec((1, TS_LRU, D_LRU), lambda b, t: (b, t, 0)),
            pl.BlockSpec((None, CONV_WIDTH, D_LRU), lambda b, t: (layer, 0, 0)),
            vec(),
            pl.BlockSpec((None, N_GATE_GROUPS, GATE_GROUP, 2 * GATE_GROUP),
                         lambda b, t: (layer, 0, 0, 0)),
            vec(), vec(), vec(),
        ],
        out_specs=pl.BlockSpec((1, TS_LRU, D_LRU), lambda b, t: (b, t, 0)),
        out_shape=jax.ShapeDtypeStruct((bsz, seq, D_LRU), BF16),
        scratch_shapes=[
            pltpu.VMEM((SUBLANES + TS_LRU, D_LRU), F32),
            pltpu.VMEM((TS_LRU, D_LRU), F32),
            pltpu.VMEM((TS_LRU, D_LRU), F32),
            pltpu.VMEM((TS_LRU, D_LRU), F32),
            pltpu.VMEM((1, D_LRU), F32),
        ],
        compiler_params=_params(("parallel", "arbitrary")),
        name="rg_lru_branch",
    )(z3, conv_w, conv_b, w_gate, ba, bx, lam)


def _nt_dot(a, b):
    return lax.dot_general(a, b, (((1,), (1,)), ((), ())), preferred_element_type=F32)


def _moba_prepare(q_ref, k_ref, v_ref, bias_ref, causal_ref, vt_ref):
    nb, seq = bias_ref.shape
    blk = MOBA_BLOCK
    neg = -jnp.inf
    kmean = jnp.mean(k_ref[0].astype(F32).reshape(nb, blk, HEAD_DIM), axis=1)
    hi = kmean.astype(BF16)
    lo = (kmean - hi.astype(F32)).astype(BF16)
    q_all = q_ref[0]
    gate = _nt_dot(hi, q_all) + _nt_dot(lo, q_all)
    blk_id = lax.broadcasted_iota(jnp.int32, (nb, seq), 0)
    qry_blk = lax.broadcasted_iota(jnp.int32, (nb, seq), 1) // blk
    gate = jnp.where(blk_id < qry_blk, gate, neg)
    bias = jnp.full((nb, seq), neg, F32)
    for _ in range(MOBA_TOPK):
        top = jnp.max(gate, axis=0, keepdims=True)
        first = jnp.min(jnp.where(gate == top, blk_id, nb), axis=0, keepdims=True)
        pick = (blk_id == first) & (top > neg)
        bias = jnp.where(pick, 0.0, bias)
        gate = jnp.where(pick, neg, gate)
    bias_ref[...] = bias
    key_pos = lax.broadcasted_iota(jnp.int32, (blk, blk), 0)
    qry_pos = lax.broadcasted_iota(jnp.int32, (blk, blk), 1)
    causal_ref[...] = jnp.where(key_pos <= qry_pos, 0.0, neg)
    for n in range(nb):
        cols = slice(n * blk, (n + 1) * blk)
        vt_ref[0:HEAD_DIM, cols] = v_ref[0, cols, :].astype(F32).T.astype(BF16)
    vt_ref[HEAD_DIM:, :] = jnp.ones((vt_ref.shape[0] - HEAD_DIM, seq), BF16)


def _moba_scores(j, slot, q_ref, k_ref, bias_ref, causal_ref, s_ref, m_ref, shift_ref):
    blk = MOBA_BLOCK
    lo, hi = j * blk, (j + 1) * blk
    q = q_ref[0, lo:hi, :]
    s_own = _nt_dot(k_ref[0, lo:hi, :], q) + causal_ref[...]
    s_ref[slot, lo:hi, :] = s_own
    m = jnp.max(s_own, axis=0, keepdims=True)
    if j > 0:
        s_past = _nt_dot(k_ref[0, 0:lo, :], q)
        s_ref[slot, 0:lo, :] = s_past
        bias = bias_ref[0:j, lo:hi]
        blk_max = jnp.max(s_past.reshape(j, blk, blk), axis=1)
        m = jnp.maximum(m, jnp.max(blk_max + bias, axis=0, keepdims=True))
        shift_ref[slot, 0:j, :] = m - bias
    m_ref[slot] = m


def _moba_output(i, slot, out_row, o_ref, vt_ref, s_ref, m_ref, shift_ref):
    blk = MOBA_BLOCK
    lo, hi = i * blk, (i + 1) * blk
    p_own = jnp.exp2(s_ref[slot, lo:hi, :] - m_ref[slot])
    acc = jnp.dot(vt_ref[:, lo:hi], p_own.astype(BF16), preferred_element_type=F32)
    if i > 0:
        s_past = s_ref[slot, 0:lo, :].reshape(i, blk, blk)
        p_past = jnp.exp2(s_past - shift_ref[slot, 0:i, :][:, None, :]).reshape(lo, blk)
        acc = acc + jnp.dot(vt_ref[:, 0:lo], p_past.astype(BF16), preferred_element_type=F32)
    inv_l = 1.0 / acc[HEAD_DIM:HEAD_DIM + 1, :]
    o_ref[0, out_row:out_row + blk, :] = (acc[0:HEAD_DIM, :] * inv_l).T.astype(BF16)


def _moba_kernel(q_ref, k_ref, v_ref, o_ref, bias_ref, causal_ref, vt_ref, s_ref, m_ref,
                 shift_ref):
    qi = pl.program_id(2)
    nb = bias_ref.shape[0]

    @pl.when(qi == 0)
    def _():
        _moba_prepare(q_ref, k_ref, v_ref, bias_ref, causal_ref, vt_ref)
        _moba_scores(0, 0, q_ref, k_ref, bias_ref, causal_ref, s_ref, m_ref, shift_ref)

    for g in range(nb // MOBA_Q_BLOCKS_PER_STEP):
        @pl.when(qi == g)
        def _(g=g):
            for r in range(MOBA_Q_BLOCKS_PER_STEP):
                i = g * MOBA_Q_BLOCKS_PER_STEP + r
                if i + 1 < nb:
                    _moba_scores(i + 1, (i + 1) % 2, q_ref, k_ref, bias_ref, causal_ref,
                                 s_ref, m_ref, shift_ref)
                _moba_output(i, i % 2, r * MOBA_BLOCK, o_ref, vt_ref, s_ref, m_ref, shift_ref)


def _moba(z3):
    bsz, seq, _ = z3.shape
    nb = seq // MOBA_BLOCK
    q_col = D_LRU // HEAD_DIM
    k_col = (D_LRU + D_ATTN) // HEAD_DIM
    v_col = (D_LRU + 2 * D_ATTN) // HEAD_DIM
    ones_rows = 2 * SUBLANES
    return pl.pallas_call(
        _moba_kernel,
        grid=(bsz, N_HEADS, nb // MOBA_Q_BLOCKS_PER_STEP),
        in_specs=[
            pl.BlockSpec((1, seq, HEAD_DIM), lambda b, h, i: (b, 0, q_col + h)),
            pl.BlockSpec((1, seq, HEAD_DIM), lambda b, h, i: (b, 0, k_col + h)),
            pl.BlockSpec((1, seq, HEAD_DIM), lambda b, h, i: (b, 0, v_col + h)),
        ],
        out_specs=pl.BlockSpec((1, MOBA_Q_BLOCKS_PER_STEP * MOBA_BLOCK, HEAD_DIM),
                               lambda b, h, i: (b, i, h)),
        out_shape=jax.ShapeDtypeStruct((bsz, seq, D_ATTN), BF16),
        scratch_shapes=[
            pltpu.VMEM((nb, seq), F32),
            pltpu.VMEM((MOBA_BLOCK, MOBA_BLOCK), F32),
            pltpu.VMEM((HEAD_DIM + ones_rows, seq), BF16),
            pltpu.VMEM((2, seq, MOBA_BLOCK), F32),
            pltpu.VMEM((2, 1, MOBA_BLOCK), F32),
            pltpu.VMEM((2, nb, MOBA_BLOCK), F32),
        ],
        compiler_params=_params(("parallel", "parallel", "arbitrary")),
        name="moba_attention",
    )(z3, z3, z3)


def _mix_kernel(x_ref, g_ref, h_ref, o_ref, gate_ref, wl_ref, wa_ref, wo_ref, out_ref):
    for r in range(TM // ROWS_MIX):
        rows = slice(r * ROWS_MIX, (r + 1) * ROWS_MIX)
        ya = jnp.dot(h_ref[rows, :], wl_ref[...], preferred_element_type=F32)
        yb = jnp.dot(o_ref[rows, :], wa_ref[...], preferred_element_type=F32)
        mix = (gate_ref[rows, 0:D_MODEL].astype(F32) * ya
               + gate_ref[rows, D_MODEL:2 * D_MODEL].astype(F32) * yb)
        y = jnp.dot(mix.astype(BF16), wo_ref[...], preferred_element_type=F32)
        out_ref[rows, :] = x_ref[rows, :] + y * _rms_scale(y) * g_ref[...]


def _mix(x, g, h, o, z, w_lru_up, w_attn_up, w_out, layer):
    t = x.shape[0]
    gate_blk = (D_LRU + 3 * D_ATTN) // (2 * D_MODEL)
    resident = lambda rows, cols: pl.BlockSpec((None, rows, cols), lambda i: (layer, 0, 0),
                                               pipeline_mode=pl.Buffered(1))
    return pl.pallas_call(
        _mix_kernel,
        grid=(t // TM,),
        in_specs=[
            pl.BlockSpec((TM, D_MODEL), lambda i: (i, 0)),
            pl.BlockSpec((1, D_MODEL), lambda i: (0, 0)),
            pl.BlockSpec((TM, D_LRU), lambda i: (i, 0)),
            pl.BlockSpec((TM, D_ATTN), lambda i: (i, 0)),
            pl.BlockSpec((TM, 2 * D_MODEL), lambda i: (i, gate_blk)),
            resident(D_LRU, D_MODEL),
            resident(D_ATTN, D_MODEL),
            resident(D_MODEL, D_MODEL),
        ],
        out_specs=pl.BlockSpec((TM, D_MODEL), lambda i: (i, 0)),
        out_shape=jax.ShapeDtypeStruct((t, D_MODEL), F32),
        compiler_params=_params(("parallel",)),
        name="mixer_merge_out_projection",
    )(x, g, h, o, z, w_lru_up, w_attn_up, w_out)


def _rope_tables(seq):
    pos = jnp.arange(seq, dtype=F32)
    inv = ROPE_THETA ** (-jnp.arange(0, ROT_DIM, 2, dtype=F32) / ROT_DIM)
    ang = pos[:, None] * inv[None, :]
    cos2 = jnp.tile(jnp.cos(ang), (1, 2))
    sin2 = jnp.tile(jnp.sin(ang), (1, 2))
    cos_groups, sin_groups = [], []
    for m in range(HEADS_PER_PARTNER):
        before = m * ROT_DIM
        after = HEAD_DIM - before - ROT_DIM
        cos_groups += [jnp.ones((seq, before), F32), cos2, jnp.ones((seq, after), F32)]
        sin_groups += [jnp.zeros((seq, before), F32), sin2, jnp.zeros((seq, after), F32)]
    return jnp.concatenate(cos_groups, axis=1), jnp.concatenate(sin_groups, axis=1)


def _qk_weights(w_in):
    depth = w_in.shape[0]
    n_chunks = HEAD_DIM // ROT_DIM
    groups = N_HEADS // HEADS_PER_PARTNER
    w = w_in[:, :, D_LRU:D_LRU + 2 * D_ATTN].reshape(
        depth, D_MODEL, 2, groups, HEADS_PER_PARTNER, n_chunks, ROT_DIM)
    per_m = []
    for m in range(HEADS_PER_PARTNER):
        order = list(range(n_chunks))
        order[0], order[m] = order[m], order[0]
        per_m.append(jnp.concatenate([w[:, :, :, :, m, c:c + 1, :] for c in order], axis=-2))
    main = jnp.stack(per_m, axis=4).reshape(depth, D_MODEL, 2 * D_ATTN)
    rot = w[:, :, :, :, :, 0, :]
    partner = jnp.concatenate([-rot[..., ROT_HALF:], rot[..., :ROT_HALF]], axis=-1)
    partner = partner.reshape(depth, D_MODEL, 2 * PARTNER_COLS)
    return main.astype(BF16), partner.astype(BF16)


def _gate_weights(wa, wx):
    def pair_diag(w):
        depth = w.shape[0]
        w = w.reshape(depth, N_GATE_GROUPS, 2, LRU_BW, LRU_BW)
        z = jnp.zeros_like(w[:, :, 0])
        top = jnp.concatenate([w[:, :, 0], z], axis=-1)
        bot = jnp.concatenate([z, w[:, :, 1]], axis=-1)
        return jnp.concatenate([top, bot], axis=-2)
    return jnp.concatenate([pair_diag(wa), pair_diag(wx)], axis=-1).astype(BF16)


def kernel(x, norm_gains, ffn_w13, ffn_w2, w_in, b_gate, conv_w, conv_b, lru_wa, lru_ba,
           lru_wx, lru_bx, lru_lambda, w_lru_up, w_attn_up, w_out):
    bsz, seq, d = x.shape
    depth = norm_gains.shape[0]
    t = bsz * seq
    assert d == D_MODEL and seq % MOBA_BLOCK == 0 and seq % TM_IN == 0 and seq % TS_LRU == 0

    w13 = ffn_w13.astype(BF16)
    w2 = ffn_w2.astype(BF16)
    w_in_b = w_in.astype(BF16)
    w_qk, w_partner = _qk_weights(w_in)
    w_lru_up_b = w_lru_up.astype(BF16)
    w_attn_up_b = w_attn_up.astype(BF16)
    w_out_b = w_out.astype(BF16)
    w_gate = _gate_weights(lru_wa, lru_wx)
    b_gate3 = b_gate.reshape(depth, 1, 2 * D_MODEL)
    conv_b3 = conv_b.reshape(depth, 1, D_LRU)
    ba3 = lru_ba.reshape(depth, 1, D_LRU)
    bx3 = lru_bx.reshape(depth, 1, D_LRU)
    lam3 = lru_lambda.reshape(depth, 1, D_LRU)
    cos_t, sin_t = _rope_tables(seq)

    xs = x.reshape(t, d)
    for l in range(depth):
        gain = lambda k: norm_gains[l, k].reshape(1, d)
        xs, xn = _ffn(xs, gain(0), gain(1), gain(2), w13, w2, l, 0)
        z = _inproj(xn, w_in_b, w_qk, w_partner, b_gate3, cos_t, sin_t, l, seq)
        z3 = z.reshape(bsz, seq, IN_COLS)
        h = _lru(z3, conv_w, conv_b3, w_gate, ba3, bx3, lam3, l)
        o = _moba(z3)
        xs = _mix(xs, gain(3), h.reshape(t, D_LRU), o.reshape(t, D_ATTN), z,
                  w_lru_up_b, w_attn_up_b, w_out_b, l)
        xs = _ffn(xs, gain(4), gain(5), None, w13, w2, l, 1)
    return xs.reshape(bsz, seq, d)
```

```python
import functools

import jax
import jax.numpy as jnp
from jax import lax
from jax.experimental import pallas as pl
from jax.experimental.pallas import tpu as pltpu

F32 = jnp.float32
BF16 = jnp.bfloat16

D_MODEL = 2048
D_LRU = 1024
LRU_BLOCKS = 16
LRU_BW = D_LRU // LRU_BLOCKS
CONV_WIDTH = 4
LRU_C = 8.0
N_HEADS = 8
HEAD_DIM = 128
D_ATTN = N_HEADS * HEAD_DIM
MOBA_BLOCK = 256
MOBA_TOPK = 3
ROPE_THETA = 500000.0
ROT_DIM = HEAD_DIM // 4
ROT_HALF = ROT_DIM // 2
D_FF = 5632
NORM_EPS = 1e-6
IN_COLS = D_LRU + 3 * D_ATTN + 2 * D_MODEL

SUBLANES = 8
VMEM_LIMIT_BYTES = 62 * 1024 * 1024

TM = 512
TM_FFN = 1024
ROWS_FFN = 512
TF = 512
TM_IN = 1024
TN_IN = 1024
ROWS_MIX = 256
TS_LRU = 512
MOBA_Q_BLOCKS_PER_STEP = 4
GATE_GROUP = 2 * LRU_BW
N_GATE_GROUPS = D_LRU // GATE_GROUP


def _rms_scale(x):
    return lax.rsqrt(jnp.mean(x * x, axis=-1, keepdims=True) + NORM_EPS)


def _params(semantics):
    return pltpu.CompilerParams(dimension_semantics=semantics,
                                vmem_limit_bytes=VMEM_LIMIT_BYTES)


def _ffn_kernel(x_ref, gin_ref, gout_ref, gnext_ref, wa_ref, wb_ref, w2_ref, o_ref, *rest):
    next_ref = rest[0] if len(rest) == 2 else None
    xn_ref = rest[-1]
    j = pl.program_id(1)
    last = pl.num_programs(1) - 1

    def step(is_first, is_last):
        for r in range(TM_FFN // ROWS_FFN):
            rows = slice(r * ROWS_FFN, (r + 1) * ROWS_FFN)
            if is_first:
                x = x_ref[rows, :]
                xn = (x * _rms_scale(x) * gin_ref[...]).astype(BF16)
                xn_ref[rows, :] = xn
            else:
                xn = xn_ref[rows, :]
            a = jnp.dot(xn, wa_ref[...], preferred_element_type=F32)
            b = jnp.dot(xn, wb_ref[...], preferred_element_type=F32)
            h = (a * jax.nn.sigmoid(a) * b).astype(BF16)
            y = jnp.dot(h, w2_ref[...], preferred_element_type=F32)
            if not is_first:
                y = y + o_ref[rows, :]
            if is_last:
                y = x_ref[rows, :] + 0.5 * (y * _rms_scale(y) * gout_ref[...])
                if next_ref is not None:
                    next_ref[rows, :] = (y * _rms_scale(y) * gnext_ref[...]).astype(BF16)
            o_ref[rows, :] = y

    pl.when(j == 0)(functools.partial(step, True, False))
    pl.when((j > 0) & (j < last))(functools.partial(step, False, False))
    pl.when(j == last)(functools.partial(step, False, True))


def _ffn(x, gin, gout, gnext, w13, w2, layer, half):
    t = x.shape[0]
    nf = D_FF // TF
    row_block = pl.BlockSpec((TM_FFN, D_MODEL), lambda i, j: (i, 0))
    gain_block = pl.BlockSpec((1, D_MODEL), lambda i, j: (0, 0))
    out_specs, out_shape = row_block, jax.ShapeDtypeStruct((t, D_MODEL), F32)
    if gnext is not None:
        out_specs = [row_block, row_block]
        out_shape = [out_shape, jax.ShapeDtypeStruct((t, D_MODEL), BF16)]
    return pl.pallas_call(
        _ffn_kernel,
        grid=(t // TM_FFN, nf),
        in_specs=[
            row_block, gain_block, gain_block, gain_block,
            pl.BlockSpec((None, None, D_MODEL, TF), lambda i, j: (layer, half, 0, j)),
            pl.BlockSpec((None, None, D_MODEL, TF), lambda i, j: (layer, half, 0, j + nf)),
            pl.BlockSpec((None, None, TF, D_MODEL), lambda i, j: (layer, half, j, 0)),
        ],
        out_specs=out_specs,
        out_shape=out_shape,
        scratch_shapes=[pltpu.VMEM((TM_FFN, D_MODEL), BF16)],
        compiler_params=_params(("parallel", "arbitrary")),
        name="ffn_half_step",
    )(x, gin, gout, gout if gnext is None else gnext, w13, w13, w2)


Q_SCALE = HEAD_DIM ** -0.5 * 1.4426950408889634
HEADS_PER_PARTNER = HEAD_DIM // ROT_DIM
PARTNER_COLS = N_HEADS * ROT_DIM


def _rotary(acc, partner, cos_ref, sin_ref):
    heads = []
    for hd in range(N_HEADS):
        m = hd % HEADS_PER_PARTNER
        c = hd // HEADS_PER_PARTNER
        xh = acc[:, hd * HEAD_DIM:(hd + 1) * HEAD_DIM]
        ph = partner[:, c * HEAD_DIM:(c + 1) * HEAD_DIM]
        lanes = slice(m * HEAD_DIM, (m + 1) * HEAD_DIM)
        heads.append(xh * cos_ref[:, lanes] + ph * sin_ref[:, lanes])
    return jnp.concatenate(heads, axis=1)


def _inproj_kernel(xn_ref, w_ref, wqk_ref, wp_ref, bg_ref, cos_ref, sin_ref, z_ref):
    j = pl.program_id(1)
    q_blk = D_LRU // TN_IN
    k_blk = (D_LRU + D_ATTN) // TN_IN
    v_blk = (D_LRU + 2 * D_ATTN) // TN_IN
    gate_blk = (D_LRU + 3 * D_ATTN) // TN_IN

    def project(w):
        return jnp.dot(xn_ref[...], w[...], preferred_element_type=F32)

    @pl.when((j < q_blk) | (j == v_blk))
    def _():
        z_ref[...] = project(w_ref).astype(BF16)

    @pl.when(j == q_blk)
    def _():
        rot = _rotary(project(wqk_ref), project(wp_ref), cos_ref, sin_ref)
        z_ref[...] = (rot * Q_SCALE).astype(BF16)

    @pl.when(j == k_blk)
    def _():
        z_ref[...] = _rotary(project(wqk_ref), project(wp_ref), cos_ref, sin_ref).astype(BF16)

    @pl.when(j >= gate_blk)
    def _():
        z_ref[...] = jax.nn.sigmoid(project(w_ref) + bg_ref[...]).astype(BF16)


def _inproj(xn, w_in, w_qk, w_partner, b_gate, cos_t, sin_t, layer, seq):
    t = xn.shape[0]
    q_blk = D_LRU // TN_IN
    k_blk = (D_LRU + D_ATTN) // TN_IN
    gate_blk = (D_LRU + 3 * D_ATTN) // TN_IN
    seq_tiles = seq // TM_IN
    is_k = lambda j: (j == k_blk).astype(jnp.int32)
    main_blk = lambda j: jnp.where((j == q_blk) | (j == k_blk), q_blk - 1, j)
    table = lambda: pl.BlockSpec((TM_IN, HEADS_PER_PARTNER * HEAD_DIM),
                                 lambda i, j: (i % seq_tiles, 0))
    return pl.pallas_call(
        _inproj_kernel,
        grid=(t // TM_IN, IN_COLS // TN_IN),
        in_specs=[
            pl.BlockSpec((TM_IN, D_MODEL), lambda i, j: (i, 0)),
            pl.BlockSpec((None, D_MODEL, TN_IN), lambda i, j: (layer, 0, main_blk(j))),
            pl.BlockSpec((None, D_MODEL, D_ATTN), lambda i, j: (layer, 0, is_k(j))),
            pl.BlockSpec((None, D_MODEL, PARTNER_COLS), lambda i, j: (layer, 0, is_k(j))),
            pl.BlockSpec((None, 1, TN_IN), lambda i, j: (layer, 0, jnp.maximum(j - gate_blk, 0))),
            table(), table(),
        ],
        out_specs=pl.BlockSpec((TM_IN, TN_IN), lambda i, j: (i, j)),
        out_shape=jax.ShapeDtypeStruct((t, IN_COLS), BF16),
        compiler_params=_params(("parallel", "arbitrary")),
        name="mixer_in_projection",
    )(xn, w_in, w_qk, w_partner, b_gate, cos_t, sin_t)


def _lru_kernel(u_ref, cw_ref, cb_ref, wg_ref, ba_ref, bx_ref, lam_ref, h_ref,
                ubuf_ref, a_ref, b_ref, hout_ref, carry_ref):
    ts = TS_LRU
    pad = SUBLANES

    @pl.when(pl.program_id(1) == 0)
    def _():
        ubuf_ref[0:pad, :] = jnp.zeros((pad, D_LRU), F32)
        carry_ref[...] = jnp.zeros_like(carry_ref)

    ubuf_ref[pad:pad + ts, :] = u_ref[0].astype(F32)
    xc = cb_ref[...]
    for tap in range(CONV_WIDTH):
        off = pad - (CONV_WIDTH - 1) + tap
        xc = xc + ubuf_ref[off:off + ts, :] * cw_ref[tap:tap + 1, :]
    ubuf_ref[0:pad, :] = ubuf_ref[ts:ts + pad, :]

    neg_c_softplus = -LRU_C * jax.nn.softplus(-lam_ref[...])
    for g in range(N_GATE_GROUPS):
        cols = slice(g * GATE_GROUP, (g + 1) * GATE_GROUP)
        xg = xc[:, cols]
        lin = jnp.dot(xg.astype(BF16), wg_ref[g], preferred_element_type=F32)
        r = jax.nn.sigmoid(lin[:, :GATE_GROUP] + ba_ref[:, cols])
        i = jax.nn.sigmoid(lin[:, GATE_GROUP:] + bx_ref[:, cols])
        log_a = r * neg_c_softplus[:, cols]
        a = jnp.exp(log_a)
        a_ref[:, cols] = a
        b_ref[:, cols] = jnp.sqrt(1.0 - a * a) * (i * xg)

    def step(t, h):
        h = a_ref[pl.ds(t, 1), :] * h + b_ref[pl.ds(t, 1), :]
        hout_ref[pl.ds(t, 1), :] = h
        return h

    carry_ref[...] = lax.fori_loop(0, ts, step, carry_ref[...], unroll=8)
    h_ref[0] = hout_ref[...].astype(BF16)


def _lru(z3, conv_w, conv_b, w_gate, ba, bx, lam, layer):
    bsz, seq, _ = z3.shape
    vec = lambda: pl.BlockSpec((None, 1, D_LRU), lambda b, t: (layer, 0, 0))
    return pl.pallas_call(
        _lru_kernel,
        grid=(bsz, seq // TS_LRU),
        in_specs=[
            pl.BlockSpec((1, TS_LRU, D_LRU), lambda b, t: (b, t, 0)),
            pl.BlockSpec((None, CONV_WIDTH, D_LRU), lambda b, t: (layer, 0, 0)),
            vec(),
            pl.BlockSpec((None, N_GATE_GROUPS, GATE_GROUP, 2 * GATE_GROUP),
                         lambda b, t: (layer, 0, 0, 0)),
            vec(), vec(), vec(),
        ],
        out_specs=pl.BlockSpec((1, TS_LRU, D_LRU), lambda b, t: (b, t, 0)),
        out_shape=jax.ShapeDtypeStruct((bsz, seq, D_LRU), BF16),
        scratch_shapes=[
            pltpu.VMEM((SUBLANES + TS_LRU, D_LRU), F32),
            pltpu.VMEM((TS_LRU, D_LRU), F32),
            pltpu.VMEM((TS_LRU, D_LRU), F32),
            pltpu.VMEM((TS_LRU, D_LRU), F32),
            pltpu.VMEM((1, D_LRU), F32),
        ],
        compiler_params=_params(("parallel", "arbitrary")),
        name="rg_lru_branch",
    )(z3, conv_w, conv_b, w_gate, ba, bx, lam)


def _nt_dot(a, b):
    return lax.dot_general(a, b, (((1,), (1,)), ((), ())), preferred_element_type=F32)


def _moba_prepare(q_ref, k_ref, v_ref, bias_ref, causal_ref, vt_ref):
    nb, seq = bias_ref.shape
    blk = MOBA_BLOCK
    neg = -jnp.inf
    kmean = jnp.mean(k_ref[0].astype(F32).reshape(nb, blk, HEAD_DIM), axis=1)
    hi = kmean.astype(BF16)
    lo = (kmean - hi.astype(F32)).astype(BF16)
    q_all = q_ref[0]
    gate = _nt_dot(hi, q_all) + _nt_dot(lo, q_all)
    blk_id = lax.broadcasted_iota(jnp.int32, (nb, seq), 0)
    qry_blk = lax.broadcasted_iota(jnp.int32, (nb, seq), 1) // blk
    gate = jnp.where(blk_id < qry_blk, gate, neg)
    bias = jnp.full((nb, seq), neg, F32)
    for _ in range(MOBA_TOPK):
        top = jnp.max(gate, axis=0, keepdims=True)
        first = jnp.min(jnp.where(gate == top, blk_id, nb), axis=0, keepdims=True)
        pick = (blk_id == first) & (top > neg)
        bias = jnp.where(pick, 0.0, bias)
        gate = jnp.where(pick, neg, gate)
    bias_ref[...] = bias
    key_pos = lax.broadcasted_iota(jnp.int32, (blk, blk), 0)
    qry_pos = lax.broadcasted_iota(jnp.int32, (blk, blk), 1)
    causal_ref[...] = jnp.where(key_pos <= qry_pos, 0.0, neg)
    for n in range(nb):
        cols = slice(n * blk, (n + 1) * blk)
        vt_ref[0:HEAD_DIM, cols] = v_ref[0, cols, :].astype(F32).T.astype(BF16)
    vt_ref[HEAD_DIM:, :] = jnp.ones((vt_ref.shape[0] - HEAD_DIM, seq), BF16)


def _moba_scores(j, slot, q_ref, k_ref, bias_ref, causal_ref, s_ref, m_ref, shift_ref):
    blk = MOBA_BLOCK
    lo, hi = j * blk, (j + 1) * blk
    q = q_ref[0, lo:hi, :]
    s_own = _nt_dot(k_ref[0, lo:hi, :], q) + causal_ref[...]
    s_ref[slot, lo:hi, :] = s_own
    m = jnp.max(s_own, axis=0, keepdims=True)
    if j > 0:
        s_past = _nt_dot(k_ref[0, 0:lo, :], q)
        s_ref[slot, 0:lo, :] = s_past
        bias = bias_ref[0:j, lo:hi]
        blk_max = jnp.max(s_past.reshape(j, blk, blk), axis=1)
        m = jnp.maximum(m, jnp.max(blk_max + bias, axis=0, keepdims=True))
        shift_ref[slot, 0:j, :] = m - bias
    m_ref[slot] = m


def _moba_output(i, slot, out_row, o_ref, vt_ref, s_ref, m_ref, shift_ref):
    blk = MOBA_BLOCK
    lo, hi = i * blk, (i + 1) * blk
    p_own = jnp.exp2(s_ref[slot, lo:hi, :] - m_ref[slot])
    acc = jnp.dot(vt_ref[:, lo:hi], p_own.astype(BF16), preferred_element_type=F32)
    if i > 0:
        s_past = s_ref[slot, 0:lo, :].reshape(i, blk, blk)
        p_past = jnp.exp2(s_past - shift_ref[slot, 0:i, :][:, None, :]).reshape(lo, blk)
        acc = acc + jnp.dot(vt_ref[:, 0:lo], p_past.astype(BF16), preferred_element_type=F32)
    inv_l = 1.0 / acc[HEAD_DIM:HEAD_DIM + 1, :]
    o_ref[0, out_row:out_row + blk, :] = (acc[0:HEAD_DIM, :] * inv_l).T.astype(BF16)


def _moba_kernel(q_ref, k_ref, v_ref, o_ref, bias_ref, causal_ref, vt_ref, s_ref, m_ref,
                 shift_ref):
    qi = pl.program_id(2)
    nb = bias_ref.shape[0]

    @pl.when(qi == 0)
    def _():
        _moba_prepare(q_ref, k_ref, v_ref, bias_ref, causal_ref, vt_ref)
        _moba_scores(0, 0, q_ref, k_ref, bias_ref, causal_ref, s_ref, m_ref, shift_ref)

    for g in range(nb // MOBA_Q_BLOCKS_PER_STEP):
        @pl.when(qi == g)
        def _(g=g):
            for r in range(MOBA_Q_BLOCKS_PER_STEP):
                i = g * MOBA_Q_BLOCKS_PER_STEP + r
                if i + 1 < nb:
                    _moba_scores(i + 1, (i + 1) % 2, q_ref, k_ref, bias_ref, causal_ref,
                                 s_ref, m_ref, shift_ref)
                _moba_output(i, i % 2, r * MOBA_BLOCK, o_ref, vt_ref, s_ref, m_ref, shift_ref)


def _moba(z3):
    bsz, seq, _ = z3.shape
    nb = seq // MOBA_BLOCK
    q_col = D_LRU // HEAD_DIM
    k_col = (D_LRU + D_ATTN) // HEAD_DIM
    v_col = (D_LRU + 2 * D_ATTN) // HEAD_DIM
    ones_rows = 2 * SUBLANES
    return pl.pallas_call(
        _moba_kernel,
        grid=(bsz, N_HEADS, nb // MOBA_Q_BLOCKS_PER_STEP),
        in_specs=[
            pl.BlockSpec((1, seq, HEAD_DIM), lambda b, h, i: (b, 0, q_col + h)),
            pl.BlockSpec((1, seq, HEAD_DIM), lambda b, h, i: (b, 0, k_col + h)),
            pl.BlockSpec((1, seq, HEAD_DIM), lambda b, h, i: (b, 0, v_col + h)),
        ],
        out_specs=pl.BlockSpec((1, MOBA_Q_BLOCKS_PER_STEP * MOBA_BLOCK, HEAD_DIM),
                               lambda b, h, i: (b, i, h)),
        out_shape=jax.ShapeDtypeStruct((bsz, seq, D_ATTN), BF16),
        scratch_shapes=[
            pltpu.VMEM((nb, seq), F32),
            pltpu.VMEM((MOBA_BLOCK, MOBA_BLOCK), F32),
            pltpu.VMEM((HEAD_DIM + ones_rows, seq), BF16),
            pltpu.VMEM((2, seq, MOBA_BLOCK), F32),
            pltpu.VMEM((2, 1, MOBA_BLOCK), F32),
            pltpu.VMEM((2, nb, MOBA_BLOCK), F32),
        ],
        compiler_params=_params(("parallel", "parallel", "arbitrary")),
        name="moba_attention",
    )(z3, z3, z3)


def _mix_kernel(x_ref, g_ref, h_ref, o_ref, gate_ref, wl_ref, wa_ref, wo_ref, out_ref):
    for r in range(TM // ROWS_MIX):
        rows = slice(r * ROWS_MIX, (r + 1) * ROWS_MIX)
        ya = jnp.dot(h_ref[rows, :], wl_ref[...], preferred_element_type=F32)
        yb = jnp.dot(o_ref[rows, :], wa_ref[...], preferred_element_type=F32)
        mix = (gate_ref[rows, 0:D_MODEL].astype(F32) * ya
               + gate_ref[rows, D_MODEL:2 * D_MODEL].astype(F32) * yb)
        y = jnp.dot(mix.astype(BF16), wo_ref[...], preferred_element_type=F32)
        out_ref[rows, :] = x_ref[rows, :] + y * _rms_scale(y) * g_ref[...]


def _mix(x, g, h, o, z, w_lru_up, w_attn_up, w_out, layer):
    t = x.shape[0]
    gate_blk = (D_LRU + 3 * D_ATTN) // (2 * D_MODEL)
    resident = lambda rows, cols: pl.BlockSpec((None, rows, cols), lambda i: (layer, 0, 0),
                                               pipeline_mode=pl.Buffered(1))
    return pl.pallas_call(
        _mix_kernel,
        grid=(t // TM,),
        in_specs=[
            pl.BlockSpec((TM, D_MODEL), lambda i: (i, 0)),
            pl.BlockSpec((1, D_MODEL), lambda i: (0, 0)),
            pl.BlockSpec((TM, D_LRU), lambda i: (i, 0)),
            pl.BlockSpec((TM, D_ATTN), lambda i: (i, 0)),
            pl.BlockSpec((TM, 2 * D_MODEL), lambda i: (i, gate_blk)),
            resident(D_LRU, D_MODEL),
            resident(D_ATTN, D_MODEL),
            resident(D_MODEL, D_MODEL),
        ],
        out_specs=pl.BlockSpec((TM, D_MODEL), lambda i: (i, 0)),
        out_shape=jax.ShapeDtypeStruct((t, D_MODEL), F32),
        compiler_params=_params(("parallel",)),
        name="mixer_merge_out_projection",
    )(x, g, h, o, z, w_lru_up, w_attn_up, w_out)


def _rope_tables(seq):
    pos = jnp.arange(seq, dtype=F32)
    inv = ROPE_THETA ** (-jnp.arange(0, ROT_DIM, 2, dtype=F32) / ROT_DIM)
    ang = pos[:, None] * inv[None, :]
    cos2 = jnp.tile(jnp.cos(ang), (1, 2))
    sin2 = jnp.tile(jnp.sin(ang), (1, 2))
    cos_groups, sin_groups = [], []
    for m in range(HEADS_PER_PARTNER):
        before = m * ROT_DIM
        after = HEAD_DIM - before - ROT_DIM
        cos_groups += [jnp.ones((seq, before), F32), cos2, jnp.ones((seq, after), F32)]
        sin_groups += [jnp.zeros((seq, before), F32), sin2, jnp.zeros((seq, after), F32)]
    return jnp.concatenate(cos_groups, axis=1), jnp.concatenate(sin_groups, axis=1)


def _qk_weights(w_in):
    depth = w_in.shape[0]
    n_chunks = HEAD_DIM // ROT_DIM
    groups = N_HEADS // HEADS_PER_PARTNER
    w = w_in[:, :, D_LRU:D_LRU + 2 * D_ATTN].reshape(
        depth, D_MODEL, 2, groups, HEADS_PER_PARTNER, n_chunks, ROT_DIM)
    per_m = []
    for m in range(HEADS_PER_PARTNER):
        order = list(range(n_chunks))
        order[0], order[m] = order[m], order[0]
        per_m.append(jnp.concatenate([w[:, :, :, :, m, c:c + 1, :] for c in order], axis=-2))
    main = jnp.stack(per_m, axis=4).reshape(depth, D_MODEL, 2 * D_ATTN)
    rot = w[:, :, :, :, :, 0, :]
    partner = jnp.concatenate([-rot[..., ROT_HALF:], rot[..., :ROT_HALF]], axis=-1)
    partner = partner.reshape(depth, D_MODEL, 2 * PARTNER_COLS)
    return main.astype(BF16), partner.astype(BF16)


def _gate_weights(wa, wx):
    def pair_diag(w):
        depth = w.shape[0]
        w = w.reshape(depth, N_GATE_GROUPS, 2, LRU_BW, LRU_BW)
        z = jnp.zeros_like(w[:, :, 0])
        top = jnp.concatenate([w[:, :, 0], z], axis=-1)
        bot = jnp.concatenate([z, w[:, :, 1]], axis=-1)
        return jnp.concatenate([top, bot], axis=-2)
    return jnp.concatenate([pair_diag(wa), pair_diag(wx)], axis=-1).astype(BF16)


def kernel(x, norm_gains, ffn_w13, ffn_w2, w_in, b_gate, conv_w, conv_b, lru_wa, lru_ba,
           lru_wx, lru_bx, lru_lambda, w_lru_up, w_attn_up, w_out):
    bsz, seq, d = x.shape
    depth = norm_gains.shape[0]
    t = bsz * seq
    assert d == D_MODEL and seq % MOBA_BLOCK == 0 and seq % TM_IN == 0 and seq % TS_LRU == 0

    w13 = ffn_w13.astype(BF16)
    w2 = ffn_w2.astype(BF16)
    w_in_b = w_in.astype(BF16)
    w_qk, w_partner = _qk_weights(w_in)
    w_lru_up_b = w_lru_up.astype(BF16)
    w_attn_up_b = w_attn_up.astype(BF16)
    w_out_b = w_out.astype(BF16)
    w_gate = _gate_weights(lru_wa, lru_wx)
    b_gate3 = b_gate.reshape(depth, 1, 2 * D_MODEL)
    conv_b3 = conv_b.reshape(depth, 1, D_LRU)
    ba3 = lru_ba.reshape(depth, 1, D_LRU)
    bx3 = lru_bx.reshape(depth, 1, D_LRU)
    lam3 = lru_lambda.reshape(depth, 1, D_LRU)
    cos_t, sin_t = _rope_tables(seq)

    xs = x.reshape(t, d)
    for l in range(depth):
        gain = lambda k: norm_gains[l, k].reshape(1, d)
        xs, xn = _ffn(xs, gain(0), gain(1), gain(2), w13, w2, l, 0)
        z = _inproj(xn, w_in_b, w_qk, w_partner, b_gate3, cos_t, sin_t, l, seq)
        z3 = z.reshape(bsz, seq, IN_COLS)
        h = _lru(z3, conv_w, conv_b3, w_gate, ba3, bx3, lam3, l)
        o = _moba(z3)
        xs = _mix(xs, gain(3), h.reshape(t, D_LRU), o.reshape(t, D_ATTN), z,
                  w_lru_up_b, w_attn_up_b, w_out_b, l)
        xs = _ffn(xs, gain(4), gain(5), None, w13, w2, l, 1)
    return xs.reshape(bsz, seq, d)
```

```python
import functools

import jax
import jax.numpy as jnp
from jax import lax
from jax.experimental import pallas as pl
from jax.experimental.pallas import tpu as pltpu

F32 = jnp.float32
BF16 = jnp.bfloat16

D_MODEL = 2048
D_LRU = 1024
LRU_BLOCKS = 16
LRU_BW = D_LRU // LRU_BLOCKS
CONV_WIDTH = 4
LRU_C = 8.0
N_HEADS = 8
HEAD_DIM = 128
D_ATTN = N_HEADS * HEAD_DIM
MOBA_BLOCK = 256
MOBA_TOPK = 3
ROPE_THETA = 500000.0
ROT_DIM = HEAD_DIM // 4
ROT_HALF = ROT_DIM // 2
D_FF = 5632
NORM_EPS = 1e-6
IN_COLS = D_LRU + 3 * D_ATTN + 2 * D_MODEL

SUBLANES = 8
VMEM_LIMIT_BYTES = 62 * 1024 * 1024

TM = 512
TM_FFN = 1024
ROWS_FFN = 512
ROWS_FFN_LAST = 256
TF = 512
TM_IN = 1024
TN_IN = 1024
ROWS_MIX = 256
TS_LRU = 512
MOBA_Q_BLOCKS_PER_STEP = 8
GATE_GROUP = 2 * LRU_BW
N_GATE_GROUPS = D_LRU // GATE_GROUP


def _rms_scale(x):
    return lax.rsqrt(jnp.mean(x * x, axis=-1, keepdims=True) + NORM_EPS)


def _params(semantics):
    return pltpu.CompilerParams(dimension_semantics=semantics,
                                vmem_limit_bytes=VMEM_LIMIT_BYTES)


def _ffn_kernel(x_ref, gin_ref, gout_ref, gnext_ref, wa_ref, wb_ref, w2_ref, o_ref, *rest):
    next_ref = rest[0] if len(rest) == 2 else None
    xn_ref = rest[-1]
    j = pl.program_id(1)
    last = pl.num_programs(1) - 1

    def step(is_first, is_last):
        chunk = ROWS_FFN_LAST if is_last else ROWS_FFN
        for r in range(TM_FFN // chunk):
            rows = slice(r * chunk, (r + 1) * chunk)
            if is_first:
                x = x_ref[rows, :]
                xn = (x * _rms_scale(x) * gin_ref[...]).astype(BF16)
                xn_ref[rows, :] = xn
            else:
                xn = xn_ref[rows, :]
            a = jnp.dot(xn, wa_ref[...], preferred_element_type=F32)
            b = jnp.dot(xn, wb_ref[...], preferred_element_type=F32)
            h = (a * jax.nn.sigmoid(a) * b).astype(BF16)
            y = jnp.dot(h, w2_ref[...], preferred_element_type=F32)
            if not is_first:
                y = y + o_ref[rows, :]
            if is_last:
                y = x_ref[rows, :] + 0.5 * (y * _rms_scale(y) * gout_ref[...])
                if next_ref is not None:
                    next_ref[rows, :] = (y * _rms_scale(y) * gnext_ref[...]).astype(BF16)
            o_ref[rows, :] = y

    pl.when(j == 0)(functools.partial(step, True, False))
    pl.when((j > 0) & (j < last))(functools.partial(step, False, False))
    pl.when(j == last)(functools.partial(step, False, True))


def _ffn(x, gin, gout, gnext, w13, w2, layer, half):
    t = x.shape[0]
    nf = D_FF // TF
    row_block = pl.BlockSpec((TM_FFN, D_MODEL), lambda i, j: (i, 0))
    gain_block = pl.BlockSpec((1, D_MODEL), lambda i, j: (0, 0))
    out_specs, out_shape = row_block, jax.ShapeDtypeStruct((t, D_MODEL), F32)
    if gnext is not None:
        out_specs = [row_block, row_block]
        out_shape = [out_shape, jax.ShapeDtypeStruct((t, D_MODEL), BF16)]
    return pl.pallas_call(
        _ffn_kernel,
        grid=(t // TM_FFN, nf),
        in_specs=[
            row_block, gain_block, gain_block, gain_block,
            pl.BlockSpec((None, None, D_MODEL, TF), lambda i, j: (layer, half, 0, j)),
            pl.BlockSpec((None, None, D_MODEL, TF), lambda i, j: (layer, half, 0, j + nf)),
            pl.BlockSpec((None, None, TF, D_MODEL), lambda i, j: (layer, half, j, 0)),
        ],
        out_specs=out_specs,
        out_shape=out_shape,
        scratch_shapes=[pltpu.VMEM((TM_FFN, D_MODEL), BF16)],
        compiler_params=_params(("parallel", "arbitrary")),
        name="ffn_half_step",
    )(x, gin, gout, gout if gnext is None else gnext, w13, w13, w2)


Q_SCALE = HEAD_DIM ** -0.5 * 1.4426950408889634
HEADS_PER_PARTNER = HEAD_DIM // ROT_DIM
PARTNER_COLS = N_HEADS * ROT_DIM


def _rotary(acc, partner, cos_ref, sin_ref):
    heads = []
    for hd in range(N_HEADS):
        m = hd % HEADS_PER_PARTNER
        c = hd // HEADS_PER_PARTNER
        xh = acc[:, hd * HEAD_DIM:(hd + 1) * HEAD_DIM]
        ph = partner[:, c * HEAD_DIM:(c + 1) * HEAD_DIM]
        lanes = slice(m * HEAD_DIM, (m + 1) * HEAD_DIM)
        heads.append(xh * cos_ref[:, lanes] + ph * sin_ref[:, lanes])
    return jnp.concatenate(heads, axis=1)


def _inproj_kernel(xn_ref, w_ref, wqk_ref, wp_ref, bg_ref, cos_ref, sin_ref, z_ref):
    j = pl.program_id(1)
    q_blk = D_LRU // TN_IN
    k_blk = (D_LRU + D_ATTN) // TN_IN
    v_blk = (D_LRU + 2 * D_ATTN) // TN_IN
    gate_blk = (D_LRU + 3 * D_ATTN) // TN_IN

    def project(w):
        return jnp.dot(xn_ref[...], w[...], preferred_element_type=F32)

    @pl.when((j < q_blk) | (j == v_blk))
    def _():
        z_ref[...] = project(w_ref).astype(BF16)

    @pl.when(j == q_blk)
    def _():
        rot = _rotary(project(wqk_ref), project(wp_ref), cos_ref, sin_ref)
        z_ref[...] = (rot * Q_SCALE).astype(BF16)

    @pl.when(j == k_blk)
    def _():
        z_ref[...] = _rotary(project(wqk_ref), project(wp_ref), cos_ref, sin_ref).astype(BF16)

    @pl.when(j >= gate_blk)
    def _():
        z_ref[...] = jax.nn.sigmoid(project(w_ref) + bg_ref[...]).astype(BF16)


def _inproj(xn, w_in, w_qk, w_partner, b_gate, cos_t, sin_t, layer, seq):
    t = xn.shape[0]
    q_blk = D_LRU // TN_IN
    k_blk = (D_LRU + D_ATTN) // TN_IN
    gate_blk = (D_LRU + 3 * D_ATTN) // TN_IN
    seq_tiles = seq // TM_IN
    is_k = lambda j: (j == k_blk).astype(jnp.int32)
    main_blk = lambda j: jnp.where((j == q_blk) | (j == k_blk), q_blk - 1, j)
    table = lambda: pl.BlockSpec((TM_IN, HEADS_PER_PARTNER * HEAD_DIM),
                                 lambda i, j: (i % seq_tiles, 0))
    return pl.pallas_call(
        _inproj_kernel,
        grid=(t // TM_IN, IN_COLS // TN_IN),
        in_specs=[
            pl.BlockSpec((TM_IN, D_MODEL), lambda i, j: (i, 0)),
            pl.BlockSpec((None, D_MODEL, TN_IN), lambda i, j: (layer, 0, main_blk(j))),
            pl.BlockSpec((None, D_MODEL, D_ATTN), lambda i, j: (layer, 0, is_k(j))),
            pl.BlockSpec((None, D_MODEL, PARTNER_COLS), lambda i, j: (layer, 0, is_k(j))),
            pl.BlockSpec((None, 1, TN_IN), lambda i, j: (layer, 0, jnp.maximum(j - gate_blk, 0))),
            table(), table(),
        ],
        out_specs=pl.BlockSpec((TM_IN, TN_IN), lambda i, j: (i, j)),
        out_shape=jax.ShapeDtypeStruct((t, IN_COLS), BF16),
        compiler_params=_params(("parallel", "arbitrary")),
        name="mixer_in_projection",
    )(xn, w_in, w_qk, w_partner, b_gate, cos_t, sin_t)


def _lru_kernel(u_ref, cw_ref, cb_ref, wg_ref, ba_ref, bx_ref, lam_ref, h_ref,
                ubuf_ref, a_ref, b_ref, hout_ref, carry_ref):
    ts = TS_LRU
    pad = SUBLANES

    @pl.when(pl.program_id(1) == 0)
    def _():
        ubuf_ref[0:pad, :] = jnp.zeros((pad, D_LRU), F32)
        carry_ref[...] = jnp.zeros_like(carry_ref)

    ubuf_ref[pad:pad + ts, :] = u_ref[0].astype(F32)
    xc = cb_ref[...]
    for tap in range(CONV_WIDTH):
        off = pad - (CONV_WIDTH - 1) + tap
        xc = xc + ubuf_ref[off:off + ts, :] * cw_ref[tap:tap + 1, :]
    ubuf_ref[0:pad, :] = ubuf_ref[ts:ts + pad, :]

    neg_c_softplus = -LRU_C * jax.nn.softplus(-lam_ref[...])
    for g in range(N_GATE_GROUPS):
        cols = slice(g * GATE_GROUP, (g + 1) * GATE_GROUP)
        xg = xc[:, cols]
        lin = jnp.dot(xg.astype(BF16), wg_ref[g], preferred_element_type=F32)
        r = jax.nn.sigmoid(lin[:, :GATE_GROUP] + ba_ref[:, cols])
        i = jax.nn.sigmoid(lin[:, GATE_GROUP:] + bx_ref[:, cols])
        log_a = r * neg_c_softplus[:, cols]
        a = jnp.exp(log_a)
        a_ref[:, cols] = a
        b_ref[:, cols] = jnp.sqrt(1.0 - a * a) * (i * xg)

    def step(t, h):
        h = a_ref[pl.ds(t, 1), :] * h + b_ref[pl.ds(t, 1), :]
        hout_ref[pl.ds(t, 1), :] = h
        return h

    carry_ref[...] = lax.fori_loop(0, ts, step, carry_ref[...], unroll=8)
    h_ref[0] = hout_ref[...].astype(BF16)


def _lru(z3, conv_w, conv_b, w_gate, ba, bx, lam, layer):
    bsz, seq, _ = z3.shape
    vec = lambda: pl.BlockSpec((None, 1, D_LRU), lambda b, t: (layer, 0, 0))
    return pl.pallas_call(
        _lru_kernel,
        grid=(bsz, seq // TS_LRU),
        in_specs=[
            pl.BlockSpec((1, TS_LRU, D_LRU), lambda b, t: (b, t, 0)),
            pl.BlockSpec((None, CONV_WIDTH, D_LRU), lambda b, t: (layer, 0, 0)),
            vec(),
            pl.BlockSpec((None, N_GATE_GROUPS, GATE_GROUP, 2 * GATE_GROUP),
                         lambda b, t: (layer, 0, 0, 0)),
            vec(), vec(), vec(),
        ],
        out_specs=pl.BlockSpec((1, TS_LRU, D_LRU), lambda b, t: (b, t, 0)),
        out_shape=jax.ShapeDtypeStruct((bsz, seq, D_LRU), BF16),
        scratch_shapes=[
            pltpu.VMEM((SUBLANES + TS_LRU, D_LRU), F32),
            pltpu.VMEM((TS_LRU, D_LRU), F32),
            pltpu.VMEM((TS_LRU, D_LRU), F32),
            pltpu.VMEM((TS_LRU, D_LRU), F32),
            pltpu.VMEM((1, D_LRU), F32),
        ],
        compiler_params=_params(("parallel", "arbitrary")),
        name="rg_lru_branch",
    )(z3, conv_w, conv_b, w_gate, ba, bx, lam)


def _nt_dot(a, b):
    return lax.dot_general(a, b, (((1,), (1,)), ((), ())), preferred_element_type=F32)


def _moba_prepare(q_ref, k_ref, v_ref, bias_ref, causal_ref, vt_ref):
    nb, seq = bias_ref.shape
    blk = MOBA_BLOCK
    neg = -jnp.inf
    kmean = jnp.mean(k_ref[0].astype(F32).reshape(nb, blk, HEAD_DIM), axis=1)
    hi = kmean.astype(BF16)
    lo = (kmean - hi.astype(F32)).astype(BF16)
    q_all = q_ref[0]
    gate = _nt_dot(hi, q_all) + _nt_dot(lo, q_all)
    blk_id = lax.broadcasted_iota(jnp.int32, (nb, seq), 0)
    qry_blk = lax.broadcasted_iota(jnp.int32, (nb, seq), 1) // blk
    gate = jnp.where(blk_id < qry_blk, gate, neg)
    bias = jnp.full((nb, seq), neg, F32)
    for _ in range(MOBA_TOPK):
        top = jnp.max(gate, axis=0, keepdims=True)
        first = jnp.min(jnp.where(gate == top, blk_id, nb), axis=0, keepdims=True)
        pick = (blk_id == first) & (top > neg)
        bias = jnp.where(pick, 0.0, bias)
        gate = jnp.where(pick, neg, gate)
    bias_ref[...] = bias
    key_pos = lax.broadcasted_iota(jnp.int32, (blk, blk), 0)
    qry_pos = lax.broadcasted_iota(jnp.int32, (blk, blk), 1)
    causal_ref[...] = jnp.where(key_pos <= qry_pos, 0.0, neg)
    for n in range(nb):
        cols = slice(n * blk, (n + 1) * blk)
        vt_ref[0:HEAD_DIM, cols] = v_ref[0, cols, :].astype(F32).T.astype(BF16)
    vt_ref[HEAD_DIM:, :] = jnp.ones((vt_ref.shape[0] - HEAD_DIM, seq), BF16)


def _moba_scores(j, slot, q_ref, k_ref, bias_ref, causal_ref, s_ref, m_ref, shift_ref):
    blk = MOBA_BLOCK
    lo, hi = j * blk, (j + 1) * blk
    q = q_ref[0, lo:hi, :]
    s_own = _nt_dot(k_ref[0, lo:hi, :], q) + causal_ref[...]
    s_ref[slot, lo:hi, :] = s_own
    m = jnp.max(s_own, axis=0, keepdims=True)
    if j > 0:
        s_past = _nt_dot(k_ref[0, 0:lo, :], q)
        s_ref[slot, 0:lo, :] = s_past
        bias = bias_ref[0:j, lo:hi]
        blk_max = jnp.max(s_past.reshape(j, blk, blk), axis=1)
        m = jnp.maximum(m, jnp.max(blk_max + bias, axis=0, keepdims=True))
        shift_ref[slot, 0:j, :] = m - bias
    m_ref[slot] = m


def _moba_output(i, slot, out_row, o_ref, vt_ref, s_ref, m_ref, shift_ref):
    blk = MOBA_BLOCK
    lo, hi = i * blk, (i + 1) * blk
    p_own = jnp.exp2(s_ref[slot, lo:hi, :] - m_ref[slot])
    acc = jnp.dot(vt_ref[:, lo:hi], p_own.astype(BF16), preferred_element_type=F32)
    if i > 0:
        s_past = s_ref[slot, 0:lo, :].reshape(i, blk, blk)
        p_past = jnp.exp2(s_past - shift_ref[slot, 0:i, :][:, None, :]).reshape(lo, blk)
        acc = acc + jnp.dot(vt_ref[:, 0:lo], p_past.astype(BF16), preferred_element_type=F32)
    inv_l = 1.0 / acc[HEAD_DIM:HEAD_DIM + 1, :]
    o_ref[0, out_row:out_row + blk, :] = (acc[0:HEAD_DIM, :] * inv_l).T.astype(BF16)


def _moba_kernel(q_ref, k_ref, v_ref, o_ref, bias_ref, causal_ref, vt_ref, s_ref, m_ref,
                 shift_ref):
    qi = pl.program_id(2)
    nb = bias_ref.shape[0]

    @pl.when(qi == 0)
    def _():
        _moba_prepare(q_ref, k_ref, v_ref, bias_ref, causal_ref, vt_ref)
        _moba_scores(0, 0, q_ref, k_ref, bias_ref, causal_ref, s_ref, m_ref, shift_ref)

    for g in range(nb // MOBA_Q_BLOCKS_PER_STEP):
        @pl.when(qi == g)
        def _(g=g):
            for r in range(MOBA_Q_BLOCKS_PER_STEP):
                i = g * MOBA_Q_BLOCKS_PER_STEP + r
                if i + 1 < nb:
                    _moba_scores(i + 1, (i + 1) % 2, q_ref, k_ref, bias_ref, causal_ref,
                                 s_ref, m_ref, shift_ref)
                _moba_output(i, i % 2, r * MOBA_BLOCK, o_ref, vt_ref, s_ref, m_ref, shift_ref)


def _moba(z3):
    bsz, seq, _ = z3.shape
    nb = seq // MOBA_BLOCK
    q_col = D_LRU // HEAD_DIM
    k_col = (D_LRU + D_ATTN) // HEAD_DIM
    v_col = (D_LRU + 2 * D_ATTN) // HEAD_DIM
    ones_rows = 2 * SUBLANES
    return pl.pallas_call(
        _moba_kernel,
        grid=(bsz, N_HEADS, nb // MOBA_Q_BLOCKS_PER_STEP),
        in_specs=[
            pl.BlockSpec((1, seq, HEAD_DIM), lambda b, h, i: (b, 0, q_col + h)),
            pl.BlockSpec((1, seq, HEAD_DIM), lambda b, h, i: (b, 0, k_col + h)),
            pl.BlockSpec((1, seq, HEAD_DIM), lambda b, h, i: (b, 0, v_col + h)),
        ],
        out_specs=pl.BlockSpec((1, MOBA_Q_BLOCKS_PER_STEP * MOBA_BLOCK, HEAD_DIM),
                               lambda b, h, i: (b, i, h)),
        out_shape=jax.ShapeDtypeStruct((bsz, seq, D_ATTN), BF16),
        scratch_shapes=[
            pltpu.VMEM((nb, seq), F32),
            pltpu.VMEM((MOBA_BLOCK, MOBA_BLOCK), F32),
            pltpu.VMEM((HEAD_DIM + ones_rows, seq), BF16),
            pltpu.VMEM((2, seq, MOBA_BLOCK), F32),
            pltpu.VMEM((2, 1, MOBA_BLOCK), F32),
            pltpu.VMEM((2, nb, MOBA_BLOCK), F32),
        ],
        compiler_params=_params(("parallel", "parallel", "arbitrary")),
        name="moba_attention",
    )(z3, z3, z3)


def _mix_kernel(x_ref, g_ref, h_ref, o_ref, gate_ref, wl_ref, wa_ref, wo_ref, out_ref):
    for r in range(TM // ROWS_MIX):
        rows = slice(r * ROWS_MIX, (r + 1) * ROWS_MIX)
        ya = jnp.dot(h_ref[rows, :], wl_ref[...], preferred_element_type=F32)
        yb = jnp.dot(o_ref[rows, :], wa_ref[...], preferred_element_type=F32)
        mix = (gate_ref[rows, 0:D_MODEL].astype(F32) * ya
               + gate_ref[rows, D_MODEL:2 * D_MODEL].astype(F32) * yb)
        y = jnp.dot(mix.astype(BF16), wo_ref[...], preferred_element_type=F32)
        out_ref[rows, :] = x_ref[rows, :] + y * _rms_scale(y) * g_ref[...]


def _mix(x, g, h, o, z, w_lru_up, w_attn_up, w_out, layer):
    t = x.shape[0]
    gate_blk = (D_LRU + 3 * D_ATTN) // (2 * D_MODEL)
    resident = lambda rows, cols: pl.BlockSpec((None, rows, cols), lambda i: (layer, 0, 0),
                                               pipeline_mode=pl.Buffered(1))
    return pl.pallas_call(
        _mix_kernel,
        grid=(t // TM,),
        in_specs=[
            pl.BlockSpec((TM, D_MODEL), lambda i: (i, 0)),
            pl.BlockSpec((1, D_MODEL), lambda i: (0, 0)),
            pl.BlockSpec((TM, D_LRU), lambda i: (i, 0)),
            pl.BlockSpec((TM, D_ATTN), lambda i: (i, 0)),
            pl.BlockSpec((TM, 2 * D_MODEL), lambda i: (i, gate_blk)),
            resident(D_LRU, D_MODEL),
            resident(D_ATTN, D_MODEL),
            resident(D_MODEL, D_MODEL),
        ],
        out_specs=pl.BlockSpec((TM, D_MODEL), lambda i: (i, 0)),
        out_shape=jax.ShapeDtypeStruct((t, D_MODEL), F32),
        compiler_params=_params(("parallel",)),
        name="mixer_merge_out_projection",
    )(x, g, h, o, z, w_lru_up, w_attn_up, w_out)


def _rope_tables(seq):
    pos = jnp.arange(seq, dtype=F32)
    inv = ROPE_THETA ** (-jnp.arange(0, ROT_DIM, 2, dtype=F32) / ROT_DIM)
    ang = pos[:, None] * inv[None, :]
    cos2 = jnp.tile(jnp.cos(ang), (1, 2))
    sin2 = jnp.tile(jnp.sin(ang), (1, 2))
    cos_groups, sin_groups = [], []
    for m in range(HEADS_PER_PARTNER):
        before = m * ROT_DIM
        after = HEAD_DIM - before - ROT_DIM
        cos_groups += [jnp.ones((seq, before), F32), cos2, jnp.ones((seq, after), F32)]
        sin_groups += [jnp.zeros((seq, before), F32), sin2, jnp.zeros((seq, after), F32)]
    return jnp.concatenate(cos_groups, axis=1), jnp.concatenate(sin_groups, axis=1)


def _qk_weights(w_in):
    depth = w_in.shape[0]
    n_chunks = HEAD_DIM // ROT_DIM
    groups = N_HEADS // HEADS_PER_PARTNER
    w = w_in[:, :, D_LRU:D_LRU + 2 * D_ATTN].reshape(
        depth, D_MODEL, 2, groups, HEADS_PER_PARTNER, n_chunks, ROT_DIM)
    per_m = []
    for m in range(HEADS_PER_PARTNER):
        order = list(range(n_chunks))
        order[0], order[m] = order[m], order[0]
        per_m.append(jnp.concatenate([w[:, :, :, :, m, c:c + 1, :] for c in order], axis=-2))
    main = jnp.stack(per_m, axis=4).reshape(depth, D_MODEL, 2 * D_ATTN)
    rot = w[:, :, :, :, :, 0, :]
    partner = jnp.concatenate([-rot[..., ROT_HALF:], rot[..., :ROT_HALF]], axis=-1)
    partner = partner.reshape(depth, D_MODEL, 2 * PARTNER_COLS)
    return main, partner


def _gate_weights(wa, wx):
    def pair_diag(w):
        depth = w.shape[0]
        w = w.reshape(depth, N_GATE_GROUPS, 2, LRU_BW, LRU_BW)
        z = jnp.zeros_like(w[:, :, 0])
        top = jnp.concatenate([w[:, :, 0], z], axis=-1)
        bot = jnp.concatenate([z, w[:, :, 1]], axis=-1)
        return jnp.concatenate([top, bot], axis=-2)
    return jnp.concatenate([pair_diag(wa), pair_diag(wx)], axis=-1).astype(BF16)


def kernel(x, norm_gains, ffn_w13, ffn_w2, w_in, b_gate, conv_w, conv_b, lru_wa, lru_ba,
           lru_wx, lru_bx, lru_lambda, w_lru_up, w_attn_up, w_out):
    bsz, seq, d = x.shape
    depth = norm_gains.shape[0]
    t = bsz * seq
    assert d == D_MODEL and seq % MOBA_BLOCK == 0 and seq % TM_IN == 0 and seq % TS_LRU == 0

    w13 = ffn_w13.astype(BF16)
    w2 = ffn_w2.astype(BF16)
    w_in_b = w_in.astype(BF16)
    w_qk, w_partner = _qk_weights(w_in_b)
    w_lru_up_b = w_lru_up.astype(BF16)
    w_attn_up_b = w_attn_up.astype(BF16)
    w_out_b = w_out.astype(BF16)
    w_gate = _gate_weights(lru_wa, lru_wx)
    b_gate3 = b_gate.reshape(depth, 1, 2 * D_MODEL)
    conv_b3 = conv_b.reshape(depth, 1, D_LRU)
    ba3 = lru_ba.reshape(depth, 1, D_LRU)
    bx3 = lru_bx.reshape(depth, 1, D_LRU)
    lam3 = lru_lambda.reshape(depth, 1, D_LRU)
    cos_t, sin_t = _rope_tables(seq)

    xs = x.reshape(t, d)
    for l in range(depth):
        gain = lambda k: norm_gains[l, k].reshape(1, d)
        xs, xn = _ffn(xs, gain(0), gain(1), gain(2), w13, w2, l, 0)
        z = _inproj(xn, w_in_b, w_qk, w_partner, b_gate3, cos_t, sin_t, l, seq)
        z3 = z.reshape(bsz, seq, IN_COLS)
        h = _lru(z3, conv_w, conv_b3, w_gate, ba3, bx3, lam3, l)
        o = _moba(z3)
        xs = _mix(xs, gain(3), h.reshape(t, D_LRU), o.reshape(t, D_ATTN), z,
                  w_lru_up_b, w_attn_up_b, w_out_b, l)
        xs = _ffn(xs, gain(4), gain(5), None, w13, w2, l, 1)
    return xs.reshape(bsz, seq, d)
```

```python
import functools

import jax
import jax.numpy as jnp
from jax import lax
from jax.experimental import pallas as pl
from jax.experimental.pallas import tpu as pltpu

F32 = jnp.float32
BF16 = jnp.bfloat16

D_MODEL = 2048
D_LRU = 1024
LRU_BLOCKS = 16
LRU_BW = D_LRU // LRU_BLOCKS
CONV_WIDTH = 4
LRU_C = 8.0
N_HEADS = 8
HEAD_DIM = 128
D_ATTN = N_HEADS * HEAD_DIM
MOBA_BLOCK = 256
MOBA_TOPK = 3
ROPE_THETA = 500000.0
ROT_DIM = HEAD_DIM // 4
ROT_HALF = ROT_DIM // 2
D_FF = 5632
NORM_EPS = 1e-6
IN_COLS = D_LRU + 3 * D_ATTN + 2 * D_MODEL

SUBLANES = 8
VMEM_LIMIT_BYTES = 62 * 1024 * 1024

TM = 512
TM_FFN = 1024
ROWS_FFN = 512
TF = 512
TM_IN = 1024
TN_IN = 1024
ROWS_MIX = 256
TS_LRU = 512
MOBA_Q_BLOCKS_PER_STEP = 8
GATE_GROUP = 2 * LRU_BW
N_GATE_GROUPS = D_LRU // GATE_GROUP


def _rms_scale(x):
    return lax.rsqrt(jnp.mean(x * x, axis=-1, keepdims=True) + NORM_EPS)


def _params(semantics):
    return pltpu.CompilerParams(dimension_semantics=semantics,
                                vmem_limit_bytes=VMEM_LIMIT_BYTES)


def _ffn_kernel(x_ref, gin_ref, gout_ref, gnext_ref, wa_ref, wb_ref, w2_ref, o_ref, *rest):
    next_ref = rest[0] if len(rest) == 2 else None
    xn_ref = rest[-1]
    j = pl.program_id(1)
    last = pl.num_programs(1) - 1

    def step(is_first, is_last):
        for r in range(TM_FFN // ROWS_FFN):
            rows = slice(r * ROWS_FFN, (r + 1) * ROWS_FFN)
            if is_first:
                x = x_ref[rows, :]
                xn = (x * _rms_scale(x) * gin_ref[...]).astype(BF16)
                xn_ref[rows, :] = xn
            else:
                xn = xn_ref[rows, :]
            a = jnp.dot(xn, wa_ref[...], preferred_element_type=F32)
            b = jnp.dot(xn, wb_ref[...], preferred_element_type=F32)
            h = (a * jax.nn.sigmoid(a) * b).astype(BF16)
            y = jnp.dot(h, w2_ref[...], preferred_element_type=F32)
            if not is_first:
                y = y + o_ref[rows, :]
            if is_last:
                y = x_ref[rows, :] + 0.5 * (y * _rms_scale(y) * gout_ref[...])
                if next_ref is not None:
                    next_ref[rows, :] = (y * _rms_scale(y) * gnext_ref[...]).astype(BF16)
            o_ref[rows, :] = y

    pl.when(j == 0)(functools.partial(step, True, False))
    pl.when((j > 0) & (j < last))(functools.partial(step, False, False))
    pl.when(j == last)(functools.partial(step, False, True))


def _ffn(x, gin, gout, gnext, w13, w2, layer, half):
    t = x.shape[0]
    nf = D_FF // TF
    row_block = pl.BlockSpec((TM_FFN, D_MODEL), lambda i, j: (i, 0))
    gain_block = pl.BlockSpec((1, D_MODEL), lambda i, j: (0, 0))
    out_specs, out_shape = row_block, jax.ShapeDtypeStruct((t, D_MODEL), F32)
    if gnext is not None:
        out_specs = [row_block, row_block]
        out_shape = [out_shape, jax.ShapeDtypeStruct((t, D_MODEL), BF16)]
    return pl.pallas_call(
        _ffn_kernel,
        grid=(t // TM_FFN, nf),
        in_specs=[
            row_block, gain_block, gain_block, gain_block,
            pl.BlockSpec((None, None, D_MODEL, TF), lambda i, j: (layer, half, 0, j)),
            pl.BlockSpec((None, None, D_MODEL, TF), lambda i, j: (layer, half, 0, j + nf)),
            pl.BlockSpec((None, None, TF, D_MODEL), lambda i, j: (layer, half, j, 0)),
        ],
        out_specs=out_specs,
        out_shape=out_shape,
        scratch_shapes=[pltpu.VMEM((TM_FFN, D_MODEL), BF16)],
        compiler_params=_params(("parallel", "arbitrary")),
        name="ffn_half_step",
    )(x, gin, gout, gout if gnext is None else gnext, w13, w13, w2)


LOG2_E = 1.4426950408889634
Q_SCALE = HEAD_DIM ** -0.5 * LOG2_E
HEADS_PER_PARTNER = HEAD_DIM // ROT_DIM
PARTNER_COLS = N_HEADS * ROT_DIM


def _rotary(acc, partner, cos_ref, sin_ref):
    heads = []
    for hd in range(N_HEADS):
        m = hd % HEADS_PER_PARTNER
        c = hd // HEADS_PER_PARTNER
        xh = acc[:, hd * HEAD_DIM:(hd + 1) * HEAD_DIM]
        ph = partner[:, c * HEAD_DIM:(c + 1) * HEAD_DIM]
        lanes = slice(m * HEAD_DIM, (m + 1) * HEAD_DIM)
        heads.append(xh * cos_ref[:, lanes] + ph * sin_ref[:, lanes])
    return jnp.concatenate(heads, axis=1)


def _inproj_kernel(xn_ref, w_ref, wqk_ref, wp_ref, bg_ref, cos_ref, sin_ref, z_ref):
    j = pl.program_id(1)
    q_blk = D_LRU // TN_IN
    k_blk = (D_LRU + D_ATTN) // TN_IN
    v_blk = (D_LRU + 2 * D_ATTN) // TN_IN
    gate_blk = (D_LRU + 3 * D_ATTN) // TN_IN

    def project(w):
        return jnp.dot(xn_ref[...], w[...], preferred_element_type=F32)

    @pl.when((j < q_blk) | (j == v_blk))
    def _():
        z_ref[...] = project(w_ref).astype(BF16)

    @pl.when(j == q_blk)
    def _():
        rot = _rotary(project(wqk_ref), project(wp_ref), cos_ref, sin_ref)
        z_ref[...] = (rot * Q_SCALE).astype(BF16)

    @pl.when(j == k_blk)
    def _():
        z_ref[...] = _rotary(project(wqk_ref), project(wp_ref), cos_ref, sin_ref).astype(BF16)

    @pl.when(j >= gate_blk)
    def _():
        z_ref[...] = jax.nn.sigmoid(project(w_ref) + bg_ref[...]).astype(BF16)


def _inproj(xn, w_in, w_qk, w_partner, b_gate, cos_t, sin_t, layer, seq):
    t = xn.shape[0]
    q_blk = D_LRU // TN_IN
    k_blk = (D_LRU + D_ATTN) // TN_IN
    gate_blk = (D_LRU + 3 * D_ATTN) // TN_IN
    seq_tiles = seq // TM_IN
    is_k = lambda j: (j == k_blk).astype(jnp.int32)
    main_blk = lambda j: jnp.where((j == q_blk) | (j == k_blk), q_blk - 1, j)
    table = lambda: pl.BlockSpec((TM_IN, HEADS_PER_PARTNER * HEAD_DIM),
                                 lambda i, j: (i % seq_tiles, 0))
    return pl.pallas_call(
        _inproj_kernel,
        grid=(t // TM_IN, IN_COLS // TN_IN),
        in_specs=[
            pl.BlockSpec((TM_IN, D_MODEL), lambda i, j: (i, 0)),
            pl.BlockSpec((None, D_MODEL, TN_IN), lambda i, j: (layer, 0, main_blk(j))),
            pl.BlockSpec((None, D_MODEL, D_ATTN), lambda i, j: (layer, 0, is_k(j))),
            pl.BlockSpec((None, D_MODEL, PARTNER_COLS), lambda i, j: (layer, 0, is_k(j))),
            pl.BlockSpec((None, 1, TN_IN), lambda i, j: (layer, 0, jnp.maximum(j - gate_blk, 0))),
            table(), table(),
        ],
        out_specs=pl.BlockSpec((TM_IN, TN_IN), lambda i, j: (i, j)),
        out_shape=jax.ShapeDtypeStruct((t, IN_COLS), BF16),
        compiler_params=_params(("parallel", "arbitrary")),
        name="mixer_in_projection",
    )(xn, w_in, w_qk, w_partner, b_gate, cos_t, sin_t)


def _lru_kernel(u_ref, cw_ref, cb_ref, wg_ref, ba_ref, bx_ref, lam_ref, h_ref,
                ubuf_ref, a_ref, b_ref, hout_ref, carry_ref):
    ts = TS_LRU
    pad = SUBLANES

    @pl.when(pl.program_id(1) == 0)
    def _():
        ubuf_ref[0:pad, :] = jnp.zeros((pad, D_LRU), F32)
        carry_ref[...] = jnp.zeros_like(carry_ref)

    ubuf_ref[pad:pad + ts, :] = u_ref[0].astype(F32)
    xc = cb_ref[...]
    for tap in range(CONV_WIDTH):
        off = pad - (CONV_WIDTH - 1) + tap
        xc = xc + ubuf_ref[off:off + ts, :] * cw_ref[tap:tap + 1, :]
    ubuf_ref[0:pad, :] = ubuf_ref[ts:ts + pad, :]

    neg_c_softplus = -LRU_C * jax.nn.softplus(-lam_ref[...])
    for g in range(N_GATE_GROUPS):
        cols = slice(g * GATE_GROUP, (g + 1) * GATE_GROUP)
        xg = xc[:, cols]
        lin = jnp.dot(xg.astype(BF16), wg_ref[g], preferred_element_type=F32)
        r = jax.nn.sigmoid(lin[:, :GATE_GROUP] + ba_ref[:, cols])
        i = jax.nn.sigmoid(lin[:, GATE_GROUP:] + bx_ref[:, cols])
        log_a = r * neg_c_softplus[:, cols]
        a = jnp.exp(log_a)
        a_ref[:, cols] = a
        b_ref[:, cols] = jnp.sqrt(1.0 - a * a) * (i * xg)

    def step(t, h):
        h = a_ref[pl.ds(t, 1), :] * h + b_ref[pl.ds(t, 1), :]
        hout_ref[pl.ds(t, 1), :] = h
        return h

    carry_ref[...] = lax.fori_loop(0, ts, step, carry_ref[...], unroll=8)
    h_ref[0] = hout_ref[...].astype(BF16)


def _lru(z3, conv_w, conv_b, w_gate, ba, bx, lam, layer):
    bsz, seq, _ = z3.shape
    vec = lambda: pl.BlockSpec((None, 1, D_LRU), lambda b, t: (layer, 0, 0))
    return pl.pallas_call(
        _lru_kernel,
        grid=(bsz, seq // TS_LRU),
        in_specs=[
            pl.BlockSpec((1, TS_LRU, D_LRU), lambda b, t: (b, t, 0)),
            pl.BlockSpec((None, CONV_WIDTH, D_LRU), lambda b, t: (layer, 0, 0)),
            vec(),
            pl.BlockSpec((None, N_GATE_GROUPS, GATE_GROUP, 2 * GATE_GROUP),
                         lambda b, t: (layer, 0, 0, 0)),
            vec(), vec(), vec(),
        ],
        out_specs=pl.BlockSpec((1, TS_LRU, D_LRU), lambda b, t: (b, t, 0)),
        out_shape=jax.ShapeDtypeStruct((bsz, seq, D_LRU), BF16),
        scratch_shapes=[
            pltpu.VMEM((SUBLANES + TS_LRU, D_LRU), F32),
            pltpu.VMEM((TS_LRU, D_LRU), F32),
            pltpu.VMEM((TS_LRU, D_LRU), F32),
            pltpu.VMEM((TS_LRU, D_LRU), F32),
            pltpu.VMEM((1, D_LRU), F32),
        ],
        compiler_params=_params(("parallel", "arbitrary")),
        name="rg_lru_branch",
    )(z3, conv_w, conv_b, w_gate, ba, bx, lam)


def _nt_dot(a, b):
    return lax.dot_general(a, b, (((1,), (1,)), ((), ())), preferred_element_type=F32)


def _moba_prepare(q_ref, k_ref, v_ref, bias_ref, causal_ref, vt_ref):
    nb, seq = bias_ref.shape
    blk = MOBA_BLOCK
    neg = -jnp.inf
    kmean = jnp.mean(k_ref[0].astype(F32).reshape(nb, blk, HEAD_DIM), axis=1)
    hi = kmean.astype(BF16)
    lo = (kmean - hi.astype(F32)).astype(BF16)
    q_all = q_ref[0]
    gate = _nt_dot(hi, q_all) + _nt_dot(lo, q_all)
    blk_id = lax.broadcasted_iota(jnp.int32, (nb, seq), 0)
    qry_blk = lax.broadcasted_iota(jnp.int32, (nb, seq), 1) // blk
    gate = jnp.where(blk_id < qry_blk, gate, neg)
    bias = jnp.full((nb, seq), neg, F32)
    for _ in range(MOBA_TOPK):
        top = jnp.max(gate, axis=0, keepdims=True)
        first = jnp.min(jnp.where(gate == top, blk_id, nb), axis=0, keepdims=True)
        pick = (blk_id == first) & (top > neg)
        bias = jnp.where(pick, 0.0, bias)
        gate = jnp.where(pick, neg, gate)
    bias_ref[...] = bias
    key_pos = lax.broadcasted_iota(jnp.int32, (blk, blk), 0)
    qry_pos = lax.broadcasted_iota(jnp.int32, (blk, blk), 1)
    causal_ref[...] = jnp.where(key_pos <= qry_pos, 0.0, neg)
    for n in range(nb):
        cols = slice(n * blk, (n + 1) * blk)
        vt_ref[0:HEAD_DIM, cols] = v_ref[0, cols, :].astype(F32).T.astype(BF16)
    vt_ref[HEAD_DIM:, :] = jnp.ones((vt_ref.shape[0] - HEAD_DIM, seq), BF16)


def _moba_scores(j, slot, q_ref, k_ref, bias_ref, causal_ref, s_ref, m_ref, shift_ref):
    blk = MOBA_BLOCK
    lo, hi = j * blk, (j + 1) * blk
    q = q_ref[0, lo:hi, :]
    s_own = _nt_dot(k_ref[0, lo:hi, :], q) + causal_ref[...]
    s_ref[slot, lo:hi, :] = s_own
    m = jnp.max(s_own, axis=0, keepdims=True)
    if j > 0:
        s_past = _nt_dot(k_ref[0, 0:lo, :], q)
        s_ref[slot, 0:lo, :] = s_past
        bias = bias_ref[0:j, lo:hi]
        blk_max = jnp.max(s_past.reshape(j, blk, blk), axis=1)
        m = jnp.maximum(m, jnp.max(blk_max + bias, axis=0, keepdims=True))
        shift_ref[slot, 0:j, :] = m - bias
    m_ref[slot] = m


def _moba_output(i, slot, out_row, o_ref, vt_ref, s_ref, m_ref, shift_ref):
    blk = MOBA_BLOCK
    lo, hi = i * blk, (i + 1) * blk
    p_own = jnp.exp2(s_ref[slot, lo:hi, :] - m_ref[slot])
    acc = jnp.dot(vt_ref[:, lo:hi], p_own.astype(BF16), preferred_element_type=F32)
    if i > 0:
        s_past = s_ref[slot, 0:lo, :].reshape(i, blk, blk)
        p_past = jnp.exp2(s_past - shift_ref[slot, 0:i, :][:, None, :]).reshape(lo, blk)
        acc = acc + jnp.dot(vt_ref[:, 0:lo], p_past.astype(BF16), preferred_element_type=F32)
    inv_l = 1.0 / acc[HEAD_DIM:HEAD_DIM + 1, :]
    o_ref[0, out_row:out_row + blk, :] = (acc[0:HEAD_DIM, :] * inv_l).T.astype(BF16)


def _moba_kernel(q_ref, k_ref, v_ref, o_ref, bias_ref, causal_ref, vt_ref, s_ref, m_ref,
                 shift_ref):
    qi = pl.program_id(2)
    nb = bias_ref.shape[0]

    @pl.when(qi == 0)
    def _():
        _moba_prepare(q_ref, k_ref, v_ref, bias_ref, causal_ref, vt_ref)
        _moba_scores(0, 0, q_ref, k_ref, bias_ref, causal_ref, s_ref, m_ref, shift_ref)

    for g in range(nb // MOBA_Q_BLOCKS_PER_STEP):
        @pl.when(qi == g)
        def _(g=g):
            for r in range(MOBA_Q_BLOCKS_PER_STEP):
                i = g * MOBA_Q_BLOCKS_PER_STEP + r
                if i + 1 < nb:
                    _moba_scores(i + 1, (i + 1) % 2, q_ref, k_ref, bias_ref, causal_ref,
                                 s_ref, m_ref, shift_ref)
                _moba_output(i, i % 2, r * MOBA_BLOCK, o_ref, vt_ref, s_ref, m_ref, shift_ref)


def _moba(z3):
    bsz, seq, _ = z3.shape
    nb = seq // MOBA_BLOCK
    q_col = D_LRU // HEAD_DIM
    k_col = (D_LRU + D_ATTN) // HEAD_DIM
    v_col = (D_LRU + 2 * D_ATTN) // HEAD_DIM
    ones_rows = 2 * SUBLANES
    return pl.pallas_call(
        _moba_kernel,
        grid=(bsz, N_HEADS, nb // MOBA_Q_BLOCKS_PER_STEP),
        in_specs=[
            pl.BlockSpec((1, seq, HEAD_DIM), lambda b, h, i: (b, 0, q_col + h)),
            pl.BlockSpec((1, seq, HEAD_DIM), lambda b, h, i: (b, 0, k_col + h)),
            pl.BlockSpec((1, seq, HEAD_DIM), lambda b, h, i: (b, 0, v_col + h)),
        ],
        out_specs=pl.BlockSpec((1, MOBA_Q_BLOCKS_PER_STEP * MOBA_BLOCK, HEAD_DIM),
                               lambda b, h, i: (b, i, h)),
        out_shape=jax.ShapeDtypeStruct((bsz, seq, D_ATTN), BF16),
        scratch_shapes=[
            pltpu.VMEM((nb, seq), F32),
            pltpu.VMEM((MOBA_BLOCK, MOBA_BLOCK), F32),
            pltpu.VMEM((HEAD_DIM + ones_rows, seq), BF16),
            pltpu.VMEM((2, seq, MOBA_BLOCK), F32),
            pltpu.VMEM((2, 1, MOBA_BLOCK), F32),
            pltpu.VMEM((2, nb, MOBA_BLOCK), F32),
        ],
        compiler_params=_params(("parallel", "parallel", "arbitrary")),
        name="moba_attention",
    )(z3, z3, z3)


def _mix_kernel(x_ref, g_ref, h_ref, o_ref, gate_ref, wl_ref, wa_ref, wo_ref, out_ref):
    for r in range(TM // ROWS_MIX):
        rows = slice(r * ROWS_MIX, (r + 1) * ROWS_MIX)
        ya = jnp.dot(h_ref[rows, :], wl_ref[...], preferred_element_type=F32)
        yb = jnp.dot(o_ref[rows, :], wa_ref[...], preferred_element_type=F32)
        mix = (gate_ref[rows, 0:D_MODEL].astype(F32) * ya
               + gate_ref[rows, D_MODEL:2 * D_MODEL].astype(F32) * yb)
        y = jnp.dot(mix.astype(BF16), wo_ref[...], preferred_element_type=F32)
        out_ref[rows, :] = x_ref[rows, :] + y * _rms_scale(y) * g_ref[...]


def _mix(x, g, h, o, z, w_lru_up, w_attn_up, w_out, layer):
    t = x.shape[0]
    gate_blk = (D_LRU + 3 * D_ATTN) // (2 * D_MODEL)
    resident = lambda rows, cols: pl.BlockSpec((None, rows, cols), lambda i: (layer, 0, 0),
                                               pipeline_mode=pl.Buffered(1))
    return pl.pallas_call(
        _mix_kernel,
        grid=(t // TM,),
        in_specs=[
            pl.BlockSpec((TM, D_MODEL), lambda i: (i, 0)),
            pl.BlockSpec((1, D_MODEL), lambda i: (0, 0)),
            pl.BlockSpec((TM, D_LRU), lambda i: (i, 0)),
            pl.BlockSpec((TM, D_ATTN), lambda i: (i, 0)),
            pl.BlockSpec((TM, 2 * D_MODEL), lambda i: (i, gate_blk)),
            resident(D_LRU, D_MODEL),
            resident(D_ATTN, D_MODEL),
            resident(D_MODEL, D_MODEL),
        ],
        out_specs=pl.BlockSpec((TM, D_MODEL), lambda i: (i, 0)),
        out_shape=jax.ShapeDtypeStruct((t, D_MODEL), F32),
        compiler_params=_params(("parallel",)),
        name="mixer_merge_out_projection",
    )(x, g, h, o, z, w_lru_up, w_attn_up, w_out)


def _rope_tables(seq):
    pos = jnp.arange(seq, dtype=F32)
    inv = ROPE_THETA ** (-jnp.arange(0, ROT_DIM, 2, dtype=F32) / ROT_DIM)
    ang = pos[:, None] * inv[None, :]
    cos2 = jnp.tile(jnp.cos(ang), (1, 2))
    sin2 = jnp.tile(jnp.sin(ang), (1, 2))
    cos_groups, sin_groups = [], []
    for m in range(HEADS_PER_PARTNER):
        before = m * ROT_DIM
        after = HEAD_DIM - before - ROT_DIM
        cos_groups += [jnp.ones((seq, before), F32), cos2, jnp.ones((seq, after), F32)]
        sin_groups += [jnp.zeros((seq, before), F32), sin2, jnp.zeros((seq, after), F32)]
    return jnp.concatenate(cos_groups, axis=1), jnp.concatenate(sin_groups, axis=1)


def _qk_weights(w_in):
    depth = w_in.shape[0]
    n_chunks = HEAD_DIM // ROT_DIM
    groups = N_HEADS // HEADS_PER_PARTNER
    w = w_in[:, :, D_LRU:D_LRU + 2 * D_ATTN].reshape(
        depth, D_MODEL, 2, groups, HEADS_PER_PARTNER, n_chunks, ROT_DIM)
    per_m = []
    for m in range(HEADS_PER_PARTNER):
        order = list(range(n_chunks))
        order[0], order[m] = order[m], order[0]
        per_m.append(jnp.concatenate([w[:, :, :, :, m, c:c + 1, :] for c in order], axis=-2))
    main = jnp.stack(per_m, axis=4).reshape(depth, D_MODEL, 2 * D_ATTN)
    rot = w[:, :, :, :, :, 0, :]
    partner = jnp.concatenate([-rot[..., ROT_HALF:], rot[..., :ROT_HALF]], axis=-1)
    partner = partner.reshape(depth, D_MODEL, 2 * PARTNER_COLS)
    return main, partner


def _gate_weights(wa, wx):
    def pair_diag(w):
        depth = w.shape[0]
        w = w.reshape(depth, N_GATE_GROUPS, 2, LRU_BW, LRU_BW)
        z = jnp.zeros_like(w[:, :, 0])
        top = jnp.concatenate([w[:, :, 0], z], axis=-1)
        bot = jnp.concatenate([z, w[:, :, 1]], axis=-1)
        return jnp.concatenate([top, bot], axis=-2)
    return jnp.concatenate([pair_diag(wa), pair_diag(wx)], axis=-1).astype(BF16)


def kernel(x, norm_gains, ffn_w13, ffn_w2, w_in, b_gate, conv_w, conv_b, lru_wa, lru_ba,
           lru_wx, lru_bx, lru_lambda, w_lru_up, w_attn_up, w_out):
    bsz, seq, d = x.shape
    depth = norm_gains.shape[0]
    t = bsz * seq
    assert d == D_MODEL and seq % MOBA_BLOCK == 0 and seq % TM_IN == 0 and seq % TS_LRU == 0

    w13 = ffn_w13.astype(BF16)
    w2 = ffn_w2.astype(BF16)
    w_in_b = w_in.astype(BF16)
    w_qk, w_partner = _qk_weights(w_in_b)
    w_lru_up_b = w_lru_up.astype(BF16)
    w_attn_up_b = w_attn_up.astype(BF16)
    w_out_b = w_out.astype(BF16)
    w_gate = _gate_weights(lru_wa, lru_wx)
    b_gate3 = b_gate.reshape(depth, 1, 2 * D_MODEL)
    conv_b3 = conv_b.reshape(depth, 1, D_LRU)
    ba3 = lru_ba.reshape(depth, 1, D_LRU)
    bx3 = lru_bx.reshape(depth, 1, D_LRU)
    lam3 = lru_lambda.reshape(depth, 1, D_LRU)
    cos_t, sin_t = _rope_tables(seq)

    xs = x.reshape(t, d)
    for l in range(depth):
        gain = lambda k: norm_gains[l, k].reshape(1, d)
        xs, xn = _ffn(xs, gain(0), gain(1), gain(2), w13, w2, l, 0)
        z = _inproj(xn, w_in_b, w_qk, w_partner, b_gate3, cos_t, sin_t, l, seq)
        z3 = z.reshape(bsz, seq, IN_COLS)
        h = _lru(z3, conv_w, conv_b3, w_gate, ba3, bx3, lam3, l)
        o = _moba(z3)
        xs = _mix(xs, gain(3), h.reshape(t, D_LRU), o.reshape(t, D_ATTN), z,
                  w_lru_up_b, w_attn_up_b, w_out_b, l)
        xs = _ffn(xs, gain(4), gain(5), None, w13, w2, l, 1)
    return xs.reshape(bsz, seq, d)
```

```python
import functools

import jax
import jax.numpy as jnp
from jax import lax
from jax.experimental import pallas as pl
from jax.experimental.pallas import tpu as pltpu

F32 = jnp.float32
BF16 = jnp.bfloat16

D_MODEL = 2048
D_LRU = 1024
LRU_BLOCKS = 16
LRU_BW = D_LRU // LRU_BLOCKS
CONV_WIDTH = 4
LRU_C = 8.0
N_HEADS = 8
HEAD_DIM = 128
D_ATTN = N_HEADS * HEAD_DIM
MOBA_BLOCK = 256
MOBA_TOPK = 3
ROPE_THETA = 500000.0
ROT_DIM = HEAD_DIM // 4
ROT_HALF = ROT_DIM // 2
D_FF = 5632
NORM_EPS = 1e-6
IN_COLS = D_LRU + 3 * D_ATTN + 2 * D_MODEL

SUBLANES = 8
VMEM_LIMIT_BYTES = 62 * 1024 * 1024

TM = 512
TM_FFN = 1024
ROWS_FFN = 512
TF = 512
TM_IN = 1024
TN_IN = 1024
N_PIECES = 4
PIECE_COLS = TN_IN // N_PIECES
ROWS_MIX = 256
TS_LRU = 512
MOBA_Q_BLOCKS_PER_STEP = 8
GATE_GROUP = 2 * LRU_BW
N_GATE_GROUPS = D_LRU // GATE_GROUP


def _rms_scale(x):
    return lax.rsqrt(jnp.mean(x * x, axis=-1, keepdims=True) + NORM_EPS)


def _params(semantics):
    return pltpu.CompilerParams(dimension_semantics=semantics,
                                vmem_limit_bytes=VMEM_LIMIT_BYTES)


def _ffn_kernel(x_ref, gin_ref, gout_ref, gnext_ref, wa_ref, wb_ref, w2_ref, o_ref, *rest):
    next_ref = rest[0] if len(rest) == 2 else None
    xn_ref = rest[-1]
    j = pl.program_id(1)
    last = pl.num_programs(1) - 1

    def step(is_first, is_last):
        for r in range(TM_FFN // ROWS_FFN):
            rows = slice(r * ROWS_FFN, (r + 1) * ROWS_FFN)
            if is_first:
                x = x_ref[rows, :]
                xn = (x * _rms_scale(x) * gin_ref[...]).astype(BF16)
                xn_ref[rows, :] = xn
            else:
                xn = xn_ref[rows, :]
            a = jnp.dot(xn, wa_ref[...], preferred_element_type=F32)
            b = jnp.dot(xn, wb_ref[...], preferred_element_type=F32)
            h = (a * jax.nn.sigmoid(a) * b).astype(BF16)
            y = jnp.dot(h, w2_ref[...], preferred_element_type=F32)
            if not is_first:
                y = y + o_ref[rows, :]
            if is_last:
                y = x_ref[rows, :] + 0.5 * (y * _rms_scale(y) * gout_ref[...])
                if next_ref is not None:
                    next_ref[rows, :] = (y * _rms_scale(y) * gnext_ref[...]).astype(BF16)
            o_ref[rows, :] = y

    pl.when(j == 0)(functools.partial(step, True, False))
    pl.when((j > 0) & (j < last))(functools.partial(step, False, False))
    pl.when(j == last)(functools.partial(step, False, True))


def _ffn(x, gin, gout, gnext, w13, w2, layer, half):
    t = x.shape[0]
    nf = D_FF // TF
    row_block = pl.BlockSpec((TM_FFN, D_MODEL), lambda i, j: (i, 0))
    gain_block = pl.BlockSpec((1, D_MODEL), lambda i, j: (0, 0))
    out_specs, out_shape = row_block, jax.ShapeDtypeStruct((t, D_MODEL), F32)
    if gnext is not None:
        out_specs = [row_block, row_block]
        out_shape = [out_shape, jax.ShapeDtypeStruct((t, D_MODEL), BF16)]
    return pl.pallas_call(
        _ffn_kernel,
        grid=(t // TM_FFN, nf),
        in_specs=[
            row_block, gain_block, gain_block, gain_block,
            pl.BlockSpec((None, None, D_MODEL, TF), lambda i, j: (layer, half, 0, j)),
            pl.BlockSpec((None, None, D_MODEL, TF), lambda i, j: (layer, half, 0, j + nf)),
            pl.BlockSpec((None, None, TF, D_MODEL), lambda i, j: (layer, half, j, 0)),
        ],
        out_specs=out_specs,
        out_shape=out_shape,
        scratch_shapes=[pltpu.VMEM((TM_FFN, D_MODEL), BF16)],
        compiler_params=_params(("parallel", "arbitrary")),
        name="ffn_half_step",
    )(x, gin, gout, gout if gnext is None else gnext, w13, w13, w2)


LOG2_E = 1.4426950408889634
Q_SCALE = HEAD_DIM ** -0.5 * LOG2_E
HEADS_PER_PARTNER = HEAD_DIM // ROT_DIM
PARTNER_COLS = N_HEADS * ROT_DIM


def _lru_conv(chunk, piece, ubuf_ref, cw_ref, cb_ref, xc_ref):
    ts = TS_LRU
    base = SUBLANES + chunk * ts
    cols = slice(piece * PIECE_COLS, (piece + 1) * PIECE_COLS)
    xc = cb_ref[:, cols]
    for tap in range(CONV_WIDTH):
        off = base - (CONV_WIDTH - 1) + tap
        xc = xc + ubuf_ref[off:off + ts, cols] * cw_ref[tap:tap + 1, cols]
    xc_ref[:, cols] = xc


def _lru_gates(piece, xc_ref, wg_ref, ba_ref, bx_ref, lam_ref, a_ref, b_ref):
    groups_per_piece = PIECE_COLS // GATE_GROUP
    for g in range(piece * groups_per_piece, (piece + 1) * groups_per_piece):
        cols = slice(g * GATE_GROUP, (g + 1) * GATE_GROUP)
        xg = xc_ref[:, cols]
        lin = jnp.dot(xg.astype(BF16), wg_ref[g], preferred_element_type=F32)
        r = jax.nn.sigmoid(lin[:, :GATE_GROUP] + ba_ref[:, cols])
        i = jax.nn.sigmoid(lin[:, GATE_GROUP:] + bx_ref[:, cols])
        a = jnp.exp(r * (-LRU_C * jax.nn.softplus(-lam_ref[:, cols])))
        a_ref[:, cols] = a
        b_ref[:, cols] = jnp.sqrt(1.0 - a * a) * (i * xg)


def _lru_scan(chunk, piece, a_ref, b_ref, carry_ref, h_ref):
    steps = TS_LRU // N_PIECES
    h = carry_ref[...]
    for t in range(piece * steps, (piece + 1) * steps):
        h = a_ref[t:t + 1, :] * h + b_ref[t:t + 1, :]
        b_ref[t:t + 1, :] = h
    carry_ref[...] = h
    rows = slice(piece * steps, (piece + 1) * steps)
    out0 = chunk * TS_LRU + piece * steps
    h_ref[out0:out0 + steps, :] = b_ref[rows, :].astype(BF16)


def _inproj_kernel(tiles_per_seq, xn_ref, w_ref, wqk_ref, wp_ref, bg_ref, cos_ref, sin_ref,
                   cw_ref, cb_ref, wg_ref, ba_ref, bx_ref, lam_ref, z_ref, h_ref,
                   ubuf_ref, xc_ref, a_ref, b_ref, carry_ref):
    i = pl.program_id(0)
    j = pl.program_id(1)
    q_blk = D_LRU // TN_IN
    k_blk = (D_LRU + D_ATTN) // TN_IN
    gate_blk = (D_LRU + 3 * D_ATTN) // TN_IN
    n_chunks = TM_IN // TS_LRU
    pad = SUBLANES
    conv = functools.partial(_lru_conv, ubuf_ref=ubuf_ref, cw_ref=cw_ref, cb_ref=cb_ref,
                             xc_ref=xc_ref)
    gates = functools.partial(_lru_gates, xc_ref=xc_ref, wg_ref=wg_ref, ba_ref=ba_ref,
                              bx_ref=bx_ref, lam_ref=lam_ref, a_ref=a_ref, b_ref=b_ref)
    scan = functools.partial(_lru_scan, a_ref=a_ref, b_ref=b_ref, carry_ref=carry_ref,
                             h_ref=h_ref)
    lru_work = {}
    for c in range(n_chunks):
        lru_work.setdefault(q_blk + 2 * c, []).append(functools.partial(conv, c))
        lru_work.setdefault(q_blk + 2 * c + 1, []).append(gates)
        lru_work.setdefault(q_blk + 2 * c + 2, []).append(functools.partial(scan, c))
    assert max(lru_work) < IN_COLS // TN_IN and q_blk == 1

    def project(w, cols=slice(None)):
        return jnp.dot(xn_ref[...], w[:, cols], preferred_element_type=F32)

    def step_body(step, lru_stages):
        rotary = step in (q_blk, k_blk)
        partner = project(wp_ref) if rotary else None
        for p in range(N_PIECES):
            cols = slice(p * PIECE_COLS, (p + 1) * PIECE_COLS)
            acc = project(wqk_ref if rotary else w_ref, cols)
            if rotary:
                heads = []
                for hh in range(PIECE_COLS // HEAD_DIM):
                    hd = p * (PIECE_COLS // HEAD_DIM) + hh
                    m = hd % HEADS_PER_PARTNER
                    c = hd // HEADS_PER_PARTNER
                    lanes = slice(m * HEAD_DIM, (m + 1) * HEAD_DIM)
                    heads.append(acc[:, hh * HEAD_DIM:(hh + 1) * HEAD_DIM] * cos_ref[:, lanes]
                                 + partner[:, c * HEAD_DIM:(c + 1) * HEAD_DIM] * sin_ref[:, lanes])
                acc = jnp.concatenate(heads, axis=1)
                if step == q_blk:
                    acc = acc * Q_SCALE
            elif step >= gate_blk:
                acc = jax.nn.sigmoid(acc + bg_ref[:, cols])
            z_ref[:, cols] = acc.astype(BF16)
            for stage in lru_stages:
                stage(p)

    @pl.when(j == 0)
    def _():
        ubuf_ref[0:pad, :] = ubuf_ref[TM_IN:TM_IN + pad, :]

        @pl.when(i % tiles_per_seq == 0)
        def _():
            ubuf_ref[0:pad, :] = jnp.zeros((pad, D_LRU), F32)
            carry_ref[...] = jnp.zeros_like(carry_ref)

        u = project(w_ref)
        ubuf_ref[pad:pad + TM_IN, :] = u
        z_ref[...] = u.astype(BF16)

    special = sorted(set(lru_work) | {q_blk, k_blk, gate_blk - 1, gate_blk})
    for step in special:
        pl.when(j == step)(functools.partial(step_body, step, lru_work.get(step, [])))
    pl.when(j > max(special))(functools.partial(step_body, gate_blk, []))


def _inproj(xn, w_in, w_qk, w_partner, b_gate, cos_t, sin_t, conv_w, conv_b, w_gate, ba, bx,
            lam, layer, seq):
    t = xn.shape[0]
    q_blk = D_LRU // TN_IN
    k_blk = (D_LRU + D_ATTN) // TN_IN
    gate_blk = (D_LRU + 3 * D_ATTN) // TN_IN
    seq_tiles = seq // TM_IN
    is_k = lambda j: (j == k_blk).astype(jnp.int32)
    main_blk = lambda j: jnp.where((j == q_blk) | (j == k_blk), q_blk - 1, j)
    table = lambda: pl.BlockSpec((TM_IN, HEADS_PER_PARTNER * HEAD_DIM),
                                 lambda i, j: (i % seq_tiles, 0))
    vec = lambda: pl.BlockSpec((None, 1, D_LRU), lambda i, j: (layer, 0, 0))
    return pl.pallas_call(
        functools.partial(_inproj_kernel, seq_tiles),
        grid=(t // TM_IN, IN_COLS // TN_IN),
        in_specs=[
            pl.BlockSpec((TM_IN, D_MODEL), lambda i, j: (i, 0)),
            pl.BlockSpec((None, D_MODEL, TN_IN), lambda i, j: (layer, 0, main_blk(j))),
            pl.BlockSpec((None, D_MODEL, D_ATTN), lambda i, j: (layer, 0, is_k(j))),
            pl.BlockSpec((None, D_MODEL, PARTNER_COLS), lambda i, j: (layer, 0, is_k(j))),
            pl.BlockSpec((None, 1, TN_IN), lambda i, j: (layer, 0, jnp.maximum(j - gate_blk, 0))),
            table(), table(),
            pl.BlockSpec((None, CONV_WIDTH, D_LRU), lambda i, j: (layer, 0, 0)),
            vec(),
            pl.BlockSpec((None, N_GATE_GROUPS, GATE_GROUP, 2 * GATE_GROUP),
                         lambda i, j: (layer, 0, 0, 0)),
            vec(), vec(), vec(),
        ],
        out_specs=[
            pl.BlockSpec((TM_IN, TN_IN), lambda i, j: (i, j)),
            pl.BlockSpec((TM_IN, D_LRU), lambda i, j: (i, 0)),
        ],
        out_shape=[jax.ShapeDtypeStruct((t, IN_COLS), BF16),
                   jax.ShapeDtypeStruct((t, D_LRU), BF16)],
        scratch_shapes=[
            pltpu.VMEM((SUBLANES + TM_IN, D_LRU), F32),
            pltpu.VMEM((TS_LRU, D_LRU), F32),
            pltpu.VMEM((TS_LRU, D_LRU), F32),
            pltpu.VMEM((TS_LRU, D_LRU), F32),
            pltpu.VMEM((1, D_LRU), F32),
        ],
        compiler_params=_params(("arbitrary", "arbitrary")),
        name="mixer_in_projection_and_rg_lru",
    )(xn, w_in, w_qk, w_partner, b_gate, cos_t, sin_t, conv_w, conv_b, w_gate, ba, bx, lam)


def _nt_dot(a, b):
    return lax.dot_general(a, b, (((1,), (1,)), ((), ())), preferred_element_type=F32)


def _moba_prepare(q_ref, k_ref, v_ref, bias_ref, causal_ref, vt_ref):
    nb, seq = bias_ref.shape
    blk = MOBA_BLOCK
    neg = -jnp.inf
    kmean = jnp.mean(k_ref[0].astype(F32).reshape(nb, blk, HEAD_DIM), axis=1)
    hi = kmean.astype(BF16)
    lo = (kmean - hi.astype(F32)).astype(BF16)
    q_all = q_ref[0]
    gate = _nt_dot(hi, q_all) + _nt_dot(lo, q_all)
    blk_id = lax.broadcasted_iota(jnp.int32, (nb, seq), 0)
    qry_blk = lax.broadcasted_iota(jnp.int32, (nb, seq), 1) // blk
    gate = jnp.where(blk_id < qry_blk, gate, neg)
    bias = jnp.full((nb, seq), neg, F32)
    for _ in range(MOBA_TOPK):
        top = jnp.max(gate, axis=0, keepdims=True)
        first = jnp.min(jnp.where(gate == top, blk_id, nb), axis=0, keepdims=True)
        pick = (blk_id == first) & (top > neg)
        bias = jnp.where(pick, 0.0, bias)
        gate = jnp.where(pick, neg, gate)
    bias_ref[...] = bias
    key_pos = lax.broadcasted_iota(jnp.int32, (blk, blk), 0)
    qry_pos = lax.broadcasted_iota(jnp.int32, (blk, blk), 1)
    causal_ref[...] = jnp.where(key_pos <= qry_pos, 0.0, neg)
    for n in range(nb):
        cols = slice(n * blk, (n + 1) * blk)
        vt_ref[0:HEAD_DIM, cols] = v_ref[0, cols, :].astype(F32).T.astype(BF16)
    vt_ref[HEAD_DIM:, :] = jnp.ones((vt_ref.shape[0] - HEAD_DIM, seq), BF16)


def _moba_scores(j, slot, q_ref, k_ref, bias_ref, causal_ref, s_ref, m_ref, shift_ref):
    blk = MOBA_BLOCK
    lo, hi = j * blk, (j + 1) * blk
    q = q_ref[0, lo:hi, :]
    s_own = _nt_dot(k_ref[0, lo:hi, :], q) + causal_ref[...]
    s_ref[slot, lo:hi, :] = s_own
    m = jnp.max(s_own, axis=0, keepdims=True)
    if j > 0:
        s_past = _nt_dot(k_ref[0, 0:lo, :], q)
        s_ref[slot, 0:lo, :] = s_past
        bias = bias_ref[0:j, lo:hi]
        blk_max = jnp.max(s_past.reshape(j, blk, blk), axis=1)
        m = jnp.maximum(m, jnp.max(blk_max + bias, axis=0, keepdims=True))
        shift_ref[slot, 0:j, :] = m - bias
    m_ref[slot] = m


def _moba_output(i, slot, out_row, o_ref, vt_ref, s_ref, m_ref, shift_ref):
    blk = MOBA_BLOCK
    lo, hi = i * blk, (i + 1) * blk
    p_own = jnp.exp2(s_ref[slot, lo:hi, :] - m_ref[slot])
    acc = jnp.dot(vt_ref[:, lo:hi], p_own.astype(BF16), preferred_element_type=F32)
    if i > 0:
        s_past = s_ref[slot, 0:lo, :].reshape(i, blk, blk)
        p_past = jnp.exp2(s_past - shift_ref[slot, 0:i, :][:, None, :]).reshape(lo, blk)
        acc = acc + jnp.dot(vt_ref[:, 0:lo], p_past.astype(BF16), preferred_element_type=F32)
    inv_l = 1.0 / acc[HEAD_DIM:HEAD_DIM + 1, :]
    o_ref[0, out_row:out_row + blk, :] = (acc[0:HEAD_DIM, :] * inv_l).T.astype(BF16)


def _moba_kernel(q_ref, k_ref, v_ref, o_ref, bias_ref, causal_ref, vt_ref, s_ref, m_ref,
                 shift_ref):
    qi = pl.program_id(2)
    nb = bias_ref.shape[0]

    @pl.when(qi == 0)
    def _():
        _moba_prepare(q_ref, k_ref, v_ref, bias_ref, causal_ref, vt_ref)
        _moba_scores(0, 0, q_ref, k_ref, bias_ref, causal_ref, s_ref, m_ref, shift_ref)

    for g in range(nb // MOBA_Q_BLOCKS_PER_STEP):
        @pl.when(qi == g)
        def _(g=g):
            for r in range(MOBA_Q_BLOCKS_PER_STEP):
                i = g * MOBA_Q_BLOCKS_PER_STEP + r
                if i + 1 < nb:
                    _moba_scores(i + 1, (i + 1) % 2, q_ref, k_ref, bias_ref, causal_ref,
                                 s_ref, m_ref, shift_ref)
                _moba_output(i, i % 2, r * MOBA_BLOCK, o_ref, vt_ref, s_ref, m_ref, shift_ref)


def _moba(z3):
    bsz, seq, _ = z3.shape
    nb = seq // MOBA_BLOCK
    q_col = D_LRU // HEAD_DIM
    k_col = (D_LRU + D_ATTN) // HEAD_DIM
    v_col = (D_LRU + 2 * D_ATTN) // HEAD_DIM
    ones_rows = 2 * SUBLANES
    return pl.pallas_call(
        _moba_kernel,
        grid=(bsz, N_HEADS, nb // MOBA_Q_BLOCKS_PER_STEP),
        in_specs=[
            pl.BlockSpec((1, seq, HEAD_DIM), lambda b, h, i: (b, 0, q_col + h)),
            pl.BlockSpec((1, seq, HEAD_DIM), lambda b, h, i: (b, 0, k_col + h)),
            pl.BlockSpec((1, seq, HEAD_DIM), lambda b, h, i: (b, 0, v_col + h)),
        ],
        out_specs=pl.BlockSpec((1, MOBA_Q_BLOCKS_PER_STEP * MOBA_BLOCK, HEAD_DIM),
                               lambda b, h, i: (b, i, h)),
        out_shape=jax.ShapeDtypeStruct((bsz, seq, D_ATTN), BF16),
        scratch_shapes=[
            pltpu.VMEM((nb, seq), F32),
            pltpu.VMEM((MOBA_BLOCK, MOBA_BLOCK), F32),
            pltpu.VMEM((HEAD_DIM + ones_rows, seq), BF16),
            pltpu.VMEM((2, seq, MOBA_BLOCK), F32),
            pltpu.VMEM((2, 1, MOBA_BLOCK), F32),
            pltpu.VMEM((2, nb, MOBA_BLOCK), F32),
        ],
        compiler_params=_params(("parallel", "parallel", "arbitrary")),
        name="moba_attention",
    )(z3, z3, z3)


def _mix_kernel(x_ref, g_ref, h_ref, o_ref, gate_ref, wl_ref, wa_ref, wo_ref, out_ref):
    for r in range(TM // ROWS_MIX):
        rows = slice(r * ROWS_MIX, (r + 1) * ROWS_MIX)
        ya = jnp.dot(h_ref[rows, :], wl_ref[...], preferred_element_type=F32)
        yb = jnp.dot(o_ref[rows, :], wa_ref[...], preferred_element_type=F32)
        mix = (gate_ref[rows, 0:D_MODEL].astype(F32) * ya
               + gate_ref[rows, D_MODEL:2 * D_MODEL].astype(F32) * yb)
        y = jnp.dot(mix.astype(BF16), wo_ref[...], preferred_element_type=F32)
        out_ref[rows, :] = x_ref[rows, :] + y * _rms_scale(y) * g_ref[...]


def _mix(x, g, h, o, z, w_lru_up, w_attn_up, w_out, layer):
    t = x.shape[0]
    gate_blk = (D_LRU + 3 * D_ATTN) // (2 * D_MODEL)
    resident = lambda rows, cols: pl.BlockSpec((None, rows, cols), lambda i: (layer, 0, 0),
                                               pipeline_mode=pl.Buffered(1))
    return pl.pallas_call(
        _mix_kernel,
        grid=(t // TM,),
        in_specs=[
            pl.BlockSpec((TM, D_MODEL), lambda i: (i, 0)),
            pl.BlockSpec((1, D_MODEL), lambda i: (0, 0)),
            pl.BlockSpec((TM, D_LRU), lambda i: (i, 0)),
            pl.BlockSpec((TM, D_ATTN), lambda i: (i, 0)),
            pl.BlockSpec((TM, 2 * D_MODEL), lambda i: (i, gate_blk)),
            resident(D_LRU, D_MODEL),
            resident(D_ATTN, D_MODEL),
            resident(D_MODEL, D_MODEL),
        ],
        out_specs=pl.BlockSpec((TM, D_MODEL), lambda i: (i, 0)),
        out_shape=jax.ShapeDtypeStruct((t, D_MODEL), F32),
        compiler_params=_params(("parallel",)),
        name="mixer_merge_out_projection",
    )(x, g, h, o, z, w_lru_up, w_attn_up, w_out)


def _rope_tables(seq):
    pos = jnp.arange(seq, dtype=F32)
    inv = ROPE_THETA ** (-jnp.arange(0, ROT_DIM, 2, dtype=F32) / ROT_DIM)
    ang = pos[:, None] * inv[None, :]
    cos2 = jnp.tile(jnp.cos(ang), (1, 2))
    sin2 = jnp.tile(jnp.sin(ang), (1, 2))
    cos_groups, sin_groups = [], []
    for m in range(HEADS_PER_PARTNER):
        before = m * ROT_DIM
        after = HEAD_DIM - before - ROT_DIM
        cos_groups += [jnp.ones((seq, before), F32), cos2, jnp.ones((seq, after), F32)]
        sin_groups += [jnp.zeros((seq, before), F32), sin2, jnp.zeros((seq, after), F32)]
    return jnp.concatenate(cos_groups, axis=1), jnp.concatenate(sin_groups, axis=1)


def _qk_weights(w_in):
    depth = w_in.shape[0]
    n_chunks = HEAD_DIM // ROT_DIM
    groups = N_HEADS // HEADS_PER_PARTNER
    w = w_in[:, :, D_LRU:D_LRU + 2 * D_ATTN].reshape(
        depth, D_MODEL, 2, groups, HEADS_PER_PARTNER, n_chunks, ROT_DIM)
    per_m = []
    for m in range(HEADS_PER_PARTNER):
        order = list(range(n_chunks))
        order[0], order[m] = order[m], order[0]
        per_m.append(jnp.concatenate([w[:, :, :, :, m, c:c + 1, :] for c in order], axis=-2))
    main = jnp.stack(per_m, axis=4).reshape(depth, D_MODEL, 2 * D_ATTN)
    rot = w[:, :, :, :, :, 0, :]
    partner = jnp.concatenate([-rot[..., ROT_HALF:], rot[..., :ROT_HALF]], axis=-1)
    partner = partner.reshape(depth, D_MODEL, 2 * PARTNER_COLS)
    return main, partner


def _gate_weights(wa, wx):
    def pair_diag(w):
        depth = w.shape[0]
        w = w.reshape(depth, N_GATE_GROUPS, 2, LRU_BW, LRU_BW)
        z = jnp.zeros_like(w[:, :, 0])
        top = jnp.concatenate([w[:, :, 0], z], axis=-1)
        bot = jnp.concatenate([z, w[:, :, 1]], axis=-1)
        return jnp.concatenate([top, bot], axis=-2)
    return jnp.concatenate([pair_diag(wa), pair_diag(wx)], axis=-1).astype(BF16)


def kernel(x, norm_gains, ffn_w13, ffn_w2, w_in, b_gate, conv_w, conv_b, lru_wa, lru_ba,
           lru_wx, lru_bx, lru_lambda, w_lru_up, w_attn_up, w_out):
    bsz, seq, d = x.shape
    depth = norm_gains.shape[0]
    t = bsz * seq
    assert d == D_MODEL and seq % MOBA_BLOCK == 0 and seq % TM_IN == 0 and seq % TS_LRU == 0

    w13 = ffn_w13.astype(BF16)
    w2 = ffn_w2.astype(BF16)
    w_in_b = w_in.astype(BF16)
    w_qk, w_partner = _qk_weights(w_in_b)
    w_lru_up_b = w_lru_up.astype(BF16)
    w_attn_up_b = w_attn_up.astype(BF16)
    w_out_b = w_out.astype(BF16)
    w_gate = _gate_weights(lru_wa, lru_wx)
    b_gate3 = b_gate.reshape(depth, 1, 2 * D_MODEL)
    conv_b3 = conv_b.reshape(depth, 1, D_LRU)
    ba3 = lru_ba.reshape(depth, 1, D_LRU)
    bx3 = lru_bx.reshape(depth, 1, D_LRU)
    lam3 = lru_lambda.reshape(depth, 1, D_LRU)
    cos_t, sin_t = _rope_tables(seq)

    xs = x.reshape(t, d)
    for l in range(depth):
        gain = lambda k: norm_gains[l, k].reshape(1, d)
        xs, xn = _ffn(xs, gain(0), gain(1), gain(2), w13, w2, l, 0)
        z, h = _inproj(xn, w_in_b, w_qk, w_partner, b_gate3, cos_t, sin_t,
                       conv_w, conv_b3, w_gate, ba3, bx3, lam3, l, seq)
        o = _moba(z.reshape(bsz, seq, IN_COLS))
        xs = _mix(xs, gain(3), h, o.reshape(t, D_ATTN), z,
                  w_lru_up_b, w_attn_up_b, w_out_b, l)
        xs = _ffn(xs, gain(4), gain(5), None, w13, w2, l, 1)
    return xs.reshape(bsz, seq, d)
```

```python
import functools

import jax
import jax.numpy as jnp
from jax import lax
from jax.experimental import pallas as pl
from jax.experimental.pallas import tpu as pltpu

F32 = jnp.float32
BF16 = jnp.bfloat16

D_MODEL = 2048
D_LRU = 1024
LRU_BLOCKS = 16
LRU_BW = D_LRU // LRU_BLOCKS
CONV_WIDTH = 4
LRU_C = 8.0
N_HEADS = 8
HEAD_DIM = 128
D_ATTN = N_HEADS * HEAD_DIM
MOBA_BLOCK = 256
MOBA_TOPK = 3
ROPE_THETA = 500000.0
ROT_DIM = HEAD_DIM // 4
ROT_HALF = ROT_DIM // 2
D_FF = 5632
NORM_EPS = 1e-6
IN_COLS = D_LRU + 3 * D_ATTN + 2 * D_MODEL

SUBLANES = 8
VMEM_LIMIT_BYTES = 62 * 1024 * 1024

TM = 512
TM_FFN = 1024
ROWS_FFN = 512
TF = 512
TM_IN = 1024
TN_IN = 1024
N_PIECES = 4
PIECE_COLS = TN_IN // N_PIECES
ROWS_MIX = 256
TS_LRU = 512
MOBA_Q_BLOCKS_PER_STEP = 8
GATE_GROUP = 2 * LRU_BW
N_GATE_GROUPS = D_LRU // GATE_GROUP


def _rms_scale(x):
    return lax.rsqrt(jnp.mean(x * x, axis=-1, keepdims=True) + NORM_EPS)


def _params(semantics):
    return pltpu.CompilerParams(dimension_semantics=semantics,
                                vmem_limit_bytes=VMEM_LIMIT_BYTES)


def _ffn_kernel(x_ref, gin_ref, gout_ref, gnext_ref, wa_ref, wb_ref, w2_ref, o_ref, *rest):
    next_ref = rest[0] if len(rest) == 2 else None
    xn_ref = rest[-1]
    j = pl.program_id(1)
    last = pl.num_programs(1) - 1

    def step(is_first, is_last):
        for r in range(TM_FFN // ROWS_FFN):
            rows = slice(r * ROWS_FFN, (r + 1) * ROWS_FFN)
            if is_first:
                x = x_ref[rows, :]
                xn = (x * _rms_scale(x) * gin_ref[...]).astype(BF16)
                xn_ref[rows, :] = xn
            else:
                xn = xn_ref[rows, :]
            a = jnp.dot(xn, wa_ref[...], preferred_element_type=F32)
            b = jnp.dot(xn, wb_ref[...], preferred_element_type=F32)
            h = (a * jax.nn.sigmoid(a) * b).astype(BF16)
            y = jnp.dot(h, w2_ref[...], preferred_element_type=F32)
            if not is_first:
                y = y + o_ref[rows, :]
            if is_last:
                y = x_ref[rows, :] + 0.5 * (y * _rms_scale(y) * gout_ref[...])
                if next_ref is not None:
                    next_ref[rows, :] = (y * _rms_scale(y) * gnext_ref[...]).astype(BF16)
            o_ref[rows, :] = y

    pl.when(j == 0)(functools.partial(step, True, False))
    pl.when((j > 0) & (j < last))(functools.partial(step, False, False))
    pl.when(j == last)(functools.partial(step, False, True))


def _ffn(x, gin, gout, gnext, w13, w2, layer, half):
    t = x.shape[0]
    nf = D_FF // TF
    row_block = pl.BlockSpec((TM_FFN, D_MODEL), lambda i, j: (i, 0))
    gain_block = pl.BlockSpec((1, D_MODEL), lambda i, j: (0, 0))
    out_specs, out_shape = row_block, jax.ShapeDtypeStruct((t, D_MODEL), F32)
    if gnext is not None:
        out_specs = [row_block, row_block]
        out_shape = [out_shape, jax.ShapeDtypeStruct((t, D_MODEL), BF16)]
    return pl.pallas_call(
        _ffn_kernel,
        grid=(t // TM_FFN, nf),
        in_specs=[
            row_block, gain_block, gain_block, gain_block,
            pl.BlockSpec((None, None, D_MODEL, TF), lambda i, j: (layer, half, 0, j)),
            pl.BlockSpec((None, None, D_MODEL, TF), lambda i, j: (layer, half, 0, j + nf)),
            pl.BlockSpec((None, None, TF, D_MODEL), lambda i, j: (layer, half, j, 0)),
        ],
        out_specs=out_specs,
        out_shape=out_shape,
        scratch_shapes=[pltpu.VMEM((TM_FFN, D_MODEL), BF16)],
        compiler_params=_params(("parallel", "arbitrary")),
        name="ffn_half_step",
    )(x, gin, gout, gout if gnext is None else gnext, w13, w13, w2)


LOG2_E = 1.4426950408889634
Q_SCALE = HEAD_DIM ** -0.5 * LOG2_E


def _lru_conv(chunk, piece, ubuf_ref, cw_ref, cb_ref, xc_ref):
    ts = TS_LRU
    base = SUBLANES + chunk * ts
    cols = slice(piece * PIECE_COLS, (piece + 1) * PIECE_COLS)
    xc = cb_ref[:, cols]
    for tap in range(CONV_WIDTH):
        off = base - (CONV_WIDTH - 1) + tap
        xc = xc + ubuf_ref[off:off + ts, cols] * cw_ref[tap:tap + 1, cols]
    xc_ref[:, cols] = xc


def _lru_gates(piece, xc_ref, wg_ref, ba_ref, bx_ref, lam_ref, a_ref, b_ref):
    groups_per_piece = PIECE_COLS // GATE_GROUP
    for g in range(piece * groups_per_piece, (piece + 1) * groups_per_piece):
        cols = slice(g * GATE_GROUP, (g + 1) * GATE_GROUP)
        xg = xc_ref[:, cols]
        lin = jnp.dot(xg.astype(BF16), wg_ref[g], preferred_element_type=F32)
        r = jax.nn.sigmoid(lin[:, :GATE_GROUP] + ba_ref[:, cols])
        i = jax.nn.sigmoid(lin[:, GATE_GROUP:] + bx_ref[:, cols])
        a = jnp.exp(r * (-LRU_C * jax.nn.softplus(-lam_ref[:, cols])))
        a_ref[:, cols] = a
        b_ref[:, cols] = jnp.sqrt(1.0 - a * a) * (i * xg)


def _lru_scan(chunk, piece, a_ref, b_ref, carry_ref, h_ref):
    steps = TS_LRU // N_PIECES
    h = carry_ref[...]
    for t in range(piece * steps, (piece + 1) * steps):
        h = a_ref[t:t + 1, :] * h + b_ref[t:t + 1, :]
        b_ref[t:t + 1, :] = h
    carry_ref[...] = h
    rows = slice(piece * steps, (piece + 1) * steps)
    out0 = chunk * TS_LRU + piece * steps
    h_ref[out0:out0 + steps, :] = b_ref[rows, :].astype(BF16)


def _inproj_kernel(tiles_per_seq, xn_ref, w_ref, bg_ref, cos_ref, sin_lo_ref, sin_hi_ref,
                   cw_ref, cb_ref, wg_ref, ba_ref, bx_ref, lam_ref, z_ref, h_ref,
                   ubuf_ref, xc_ref, a_ref, b_ref, carry_ref):
    i = pl.program_id(0)
    j = pl.program_id(1)
    q_blk = D_LRU // TN_IN
    k_blk = (D_LRU + D_ATTN) // TN_IN
    gate_blk = (D_LRU + 3 * D_ATTN) // TN_IN
    n_chunks = TM_IN // TS_LRU
    pad = SUBLANES
    conv = functools.partial(_lru_conv, ubuf_ref=ubuf_ref, cw_ref=cw_ref, cb_ref=cb_ref,
                             xc_ref=xc_ref)
    gates = functools.partial(_lru_gates, xc_ref=xc_ref, wg_ref=wg_ref, ba_ref=ba_ref,
                              bx_ref=bx_ref, lam_ref=lam_ref, a_ref=a_ref, b_ref=b_ref)
    scan = functools.partial(_lru_scan, a_ref=a_ref, b_ref=b_ref, carry_ref=carry_ref,
                             h_ref=h_ref)
    lru_work = {}
    for c in range(n_chunks):
        lru_work.setdefault(q_blk + 2 * c, []).append(functools.partial(conv, c))
        lru_work.setdefault(q_blk + 2 * c + 1, []).append(gates)
        lru_work.setdefault(q_blk + 2 * c + 2, []).append(functools.partial(scan, c))
    assert max(lru_work) < IN_COLS // TN_IN and q_blk == 1

    def project(w, cols=slice(None)):
        return jnp.dot(xn_ref[...], w[:, cols], preferred_element_type=F32)

    def step_body(step, lru_stages):
        for p in range(N_PIECES):
            cols = slice(p * PIECE_COLS, (p + 1) * PIECE_COLS)
            acc = project(w_ref, cols)
            if step in (q_blk, k_blk):
                heads = []
                for hh in range(PIECE_COLS // HEAD_DIM):
                    xh = acc[:, hh * HEAD_DIM:(hh + 1) * HEAD_DIM]
                    up = pltpu.roll(xh, HEAD_DIM - ROT_HALF, 1)
                    down = pltpu.roll(xh, ROT_HALF, 1)
                    heads.append(xh * cos_ref[...] + up * sin_lo_ref[...] + down * sin_hi_ref[...])
                acc = jnp.concatenate(heads, axis=1)
                if step == q_blk:
                    acc = acc * Q_SCALE
            elif step >= gate_blk:
                acc = jax.nn.sigmoid(acc + bg_ref[:, cols])
            z_ref[:, cols] = acc.astype(BF16)
            for stage in lru_stages:
                stage(p)

    @pl.when(j == 0)
    def _():
        ubuf_ref[0:pad, :] = ubuf_ref[TM_IN:TM_IN + pad, :]

        @pl.when(i % tiles_per_seq == 0)
        def _():
            ubuf_ref[0:pad, :] = jnp.zeros((pad, D_LRU), F32)
            carry_ref[...] = jnp.zeros_like(carry_ref)

        u = project(w_ref)
        ubuf_ref[pad:pad + TM_IN, :] = u
        z_ref[...] = u.astype(BF16)

    special = sorted(set(lru_work) | {q_blk, k_blk, gate_blk - 1, gate_blk})
    for step in special:
        pl.when(j == step)(functools.partial(step_body, step, lru_work.get(step, [])))
    pl.when(j > max(special))(functools.partial(step_body, gate_blk, []))


def _inproj(xn, w_in, b_gate, rope_tables, conv_w, conv_b, w_gate, ba, bx, lam, layer, seq):
    t = xn.shape[0]
    gate_blk = (D_LRU + 3 * D_ATTN) // TN_IN
    seq_tiles = seq // TM_IN
    table = lambda: pl.BlockSpec((TM_IN, HEAD_DIM), lambda i, j: (i % seq_tiles, 0))
    vec = lambda: pl.BlockSpec((None, 1, D_LRU), lambda i, j: (layer, 0, 0))
    return pl.pallas_call(
        functools.partial(_inproj_kernel, seq_tiles),
        grid=(t // TM_IN, IN_COLS // TN_IN),
        in_specs=[
            pl.BlockSpec((TM_IN, D_MODEL), lambda i, j: (i, 0)),
            pl.BlockSpec((None, D_MODEL, TN_IN), lambda i, j: (layer, 0, j)),
            pl.BlockSpec((None, 1, TN_IN), lambda i, j: (layer, 0, jnp.maximum(j - gate_blk, 0))),
            table(), table(), table(),
            pl.BlockSpec((None, CONV_WIDTH, D_LRU), lambda i, j: (layer, 0, 0)),
            vec(),
            pl.BlockSpec((None, N_GATE_GROUPS, GATE_GROUP, 2 * GATE_GROUP),
                         lambda i, j: (layer, 0, 0, 0)),
            vec(), vec(), vec(),
        ],
        out_specs=[
            pl.BlockSpec((TM_IN, TN_IN), lambda i, j: (i, j)),
            pl.BlockSpec((TM_IN, D_LRU), lambda i, j: (i, 0)),
        ],
        out_shape=[jax.ShapeDtypeStruct((t, IN_COLS), BF16),
                   jax.ShapeDtypeStruct((t, D_LRU), BF16)],
        scratch_shapes=[
            pltpu.VMEM((SUBLANES + TM_IN, D_LRU), F32),
            pltpu.VMEM((TS_LRU, D_LRU), F32),
            pltpu.VMEM((TS_LRU, D_LRU), F32),
            pltpu.VMEM((TS_LRU, D_LRU), F32),
            pltpu.VMEM((1, D_LRU), F32),
        ],
        compiler_params=_params(("arbitrary", "arbitrary")),
        name="mixer_in_projection_and_rg_lru",
    )(xn, w_in, b_gate, *rope_tables, conv_w, conv_b, w_gate, ba, bx, lam)


def _nt_dot(a, b):
    return lax.dot_general(a, b, (((1,), (1,)), ((), ())), preferred_element_type=F32)


def _moba_prepare(q_ref, k_ref, v_ref, bias_ref, causal_ref, vt_ref):
    nb, seq = bias_ref.shape
    blk = MOBA_BLOCK
    neg = -jnp.inf
    kmean = jnp.mean(k_ref[0].astype(F32).reshape(nb, blk, HEAD_DIM), axis=1)
    hi = kmean.astype(BF16)
    lo = (kmean - hi.astype(F32)).astype(BF16)
    q_all = q_ref[0]
    gate = _nt_dot(hi, q_all) + _nt_dot(lo, q_all)
    blk_id = lax.broadcasted_iota(jnp.int32, (nb, seq), 0)
    qry_blk = lax.broadcasted_iota(jnp.int32, (nb, seq), 1) // blk
    gate = jnp.where(blk_id < qry_blk, gate, neg)
    bias = jnp.full((nb, seq), neg, F32)
    for _ in range(MOBA_TOPK):
        top = jnp.max(gate, axis=0, keepdims=True)
        first = jnp.min(jnp.where(gate == top, blk_id, nb), axis=0, keepdims=True)
        pick = (blk_id == first) & (top > neg)
        bias = jnp.where(pick, 0.0, bias)
        gate = jnp.where(pick, neg, gate)
    bias_ref[...] = bias
    key_pos = lax.broadcasted_iota(jnp.int32, (blk, blk), 0)
    qry_pos = lax.broadcasted_iota(jnp.int32, (blk, blk), 1)
    causal_ref[...] = jnp.where(key_pos <= qry_pos, 0.0, neg)
    for n in range(nb):
        cols = slice(n * blk, (n + 1) * blk)
        vt_ref[0:HEAD_DIM, cols] = v_ref[0, cols, :].astype(F32).T.astype(BF16)
    vt_ref[HEAD_DIM:, :] = jnp.ones((vt_ref.shape[0] - HEAD_DIM, seq), BF16)


def _moba_scores(j, slot, q_ref, k_ref, bias_ref, causal_ref, s_ref, m_ref, shift_ref):
    blk = MOBA_BLOCK
    lo, hi = j * blk, (j + 1) * blk
    q = q_ref[0, lo:hi, :]
    s_own = _nt_dot(k_ref[0, lo:hi, :], q) + causal_ref[...]
    s_ref[slot, lo:hi, :] = s_own
    m = jnp.max(s_own, axis=0, keepdims=True)
    if j > 0:
        s_past = _nt_dot(k_ref[0, 0:lo, :], q)
        s_ref[slot, 0:lo, :] = s_past
        bias = bias_ref[0:j, lo:hi]
        blk_max = jnp.max(s_past.reshape(j, blk, blk), axis=1)
        m = jnp.maximum(m, jnp.max(blk_max + bias, axis=0, keepdims=True))
        shift_ref[slot, 0:j, :] = m - bias
    m_ref[slot] = m


def _moba_output(i, slot, out_row, o_ref, vt_ref, s_ref, m_ref, shift_ref):
    blk = MOBA_BLOCK
    lo, hi = i * blk, (i + 1) * blk
    p_own = jnp.exp2(s_ref[slot, lo:hi, :] - m_ref[slot])
    acc = jnp.dot(vt_ref[:, lo:hi], p_own.astype(BF16), preferred_element_type=F32)
    if i > 0:
        s_past = s_ref[slot, 0:lo, :].reshape(i, blk, blk)
        p_past = jnp.exp2(s_past - shift_ref[slot, 0:i, :][:, None, :]).reshape(lo, blk)
        acc = acc + jnp.dot(vt_ref[:, 0:lo], p_past.astype(BF16), preferred_element_type=F32)
    inv_l = 1.0 / acc[HEAD_DIM:HEAD_DIM + 1, :]
    o_ref[0, out_row:out_row + blk, :] = (acc[0:HEAD_DIM, :] * inv_l).T.astype(BF16)


def _moba_kernel(q_ref, k_ref, v_ref, o_ref, bias_ref, causal_ref, vt_ref, s_ref, m_ref,
                 shift_ref):
    qi = pl.program_id(2)
    nb = bias_ref.shape[0]

    @pl.when(qi == 0)
    def _():
        _moba_prepare(q_ref, k_ref, v_ref, bias_ref, causal_ref, vt_ref)
        _moba_scores(0, 0, q_ref, k_ref, bias_ref, causal_ref, s_ref, m_ref, shift_ref)

    for g in range(nb // MOBA_Q_BLOCKS_PER_STEP):
        @pl.when(qi == g)
        def _(g=g):
            for r in range(MOBA_Q_BLOCKS_PER_STEP):
                i = g * MOBA_Q_BLOCKS_PER_STEP + r
                if i + 1 < nb:
                    _moba_scores(i + 1, (i + 1) % 2, q_ref, k_ref, bias_ref, causal_ref,
                                 s_ref, m_ref, shift_ref)
                _moba_output(i, i % 2, r * MOBA_BLOCK, o_ref, vt_ref, s_ref, m_ref, shift_ref)


def _moba(z3):
    bsz, seq, _ = z3.shape
    nb = seq // MOBA_BLOCK
    q_col = D_LRU // HEAD_DIM
    k_col = (D_LRU + D_ATTN) // HEAD_DIM
    v_col = (D_LRU + 2 * D_ATTN) // HEAD_DIM
    ones_rows = 2 * SUBLANES
    return pl.pallas_call(
        _moba_kernel,
        grid=(bsz, N_HEADS, nb // MOBA_Q_BLOCKS_PER_STEP),
        in_specs=[
            pl.BlockSpec((1, seq, HEAD_DIM), lambda b, h, i: (b, 0, q_col + h)),
            pl.BlockSpec((1, seq, HEAD_DIM), lambda b, h, i: (b, 0, k_col + h)),
            pl.BlockSpec((1, seq, HEAD_DIM), lambda b, h, i: (b, 0, v_col + h)),
        ],
        out_specs=pl.BlockSpec((1, MOBA_Q_BLOCKS_PER_STEP * MOBA_BLOCK, HEAD_DIM),
                               lambda b, h, i: (b, i, h)),
        out_shape=jax.ShapeDtypeStruct((bsz, seq, D_ATTN), BF16),
        scratch_shapes=[
            pltpu.VMEM((nb, seq), F32),
            pltpu.VMEM((MOBA_BLOCK, MOBA_BLOCK), F32),
            pltpu.VMEM((HEAD_DIM + ones_rows, seq), BF16),
            pltpu.VMEM((2, seq, MOBA_BLOCK), F32),
            pltpu.VMEM((2, 1, MOBA_BLOCK), F32),
            pltpu.VMEM((2, nb, MOBA_BLOCK), F32),
        ],
        compiler_params=_params(("parallel", "parallel", "arbitrary")),
        name="moba_attention",
    )(z3, z3, z3)


def _mix_kernel(x_ref, g_ref, h_ref, o_ref, gate_ref, wl_ref, wa_ref, wo_ref, out_ref):
    for r in range(TM // ROWS_MIX):
        rows = slice(r * ROWS_MIX, (r + 1) * ROWS_MIX)
        ya = jnp.dot(h_ref[rows, :], wl_ref[...], preferred_element_type=F32)
        yb = jnp.dot(o_ref[rows, :], wa_ref[...], preferred_element_type=F32)
        mix = (gate_ref[rows, 0:D_MODEL].astype(F32) * ya
               + gate_ref[rows, D_MODEL:2 * D_MODEL].astype(F32) * yb)
        y = jnp.dot(mix.astype(BF16), wo_ref[...], preferred_element_type=F32)
        out_ref[rows, :] = x_ref[rows, :] + y * _rms_scale(y) * g_ref[...]


def _mix(x, g, h, o, z, w_lru_up, w_attn_up, w_out, layer):
    t = x.shape[0]
    gate_blk = (D_LRU + 3 * D_ATTN) // (2 * D_MODEL)
    resident = lambda rows, cols: pl.BlockSpec((None, rows, cols), lambda i: (layer, 0, 0),
                                               pipeline_mode=pl.Buffered(1))
    return pl.pallas_call(
        _mix_kernel,
        grid=(t // TM,),
        in_specs=[
            pl.BlockSpec((TM, D_MODEL), lambda i: (i, 0)),
            pl.BlockSpec((1, D_MODEL), lambda i: (0, 0)),
            pl.BlockSpec((TM, D_LRU), lambda i: (i, 0)),
            pl.BlockSpec((TM, D_ATTN), lambda i: (i, 0)),
            pl.BlockSpec((TM, 2 * D_MODEL), lambda i: (i, gate_blk)),
            resident(D_LRU, D_MODEL),
            resident(D_ATTN, D_MODEL),
            resident(D_MODEL, D_MODEL),
        ],
        out_specs=pl.BlockSpec((TM, D_MODEL), lambda i: (i, 0)),
        out_shape=jax.ShapeDtypeStruct((t, D_MODEL), F32),
        compiler_params=_params(("parallel",)),
        name="mixer_merge_out_projection",
    )(x, g, h, o, z, w_lru_up, w_attn_up, w_out)


def _rope_tables(seq):
    pos = jnp.arange(seq, dtype=F32)
    inv = ROPE_THETA ** (-jnp.arange(0, ROT_DIM, 2, dtype=F32) / ROT_DIM)
    ang = pos[:, None] * inv[None, :]
    cos, sin = jnp.cos(ang), jnp.sin(ang)
    rest = HEAD_DIM - ROT_DIM
    zeros = jnp.zeros((seq, ROT_HALF), F32)
    tail0 = jnp.zeros((seq, rest), F32)
    cos_t = jnp.concatenate([cos, cos, jnp.ones((seq, rest), F32)], axis=1)
    sin_lo_t = jnp.concatenate([-sin, zeros, tail0], axis=1)
    sin_hi_t = jnp.concatenate([zeros, sin, tail0], axis=1)
    return cos_t, sin_lo_t, sin_hi_t


def _gate_weights(wa, wx):
    def pair_diag(w):
        depth = w.shape[0]
        w = w.reshape(depth, N_GATE_GROUPS, 2, LRU_BW, LRU_BW)
        z = jnp.zeros_like(w[:, :, 0])
        top = jnp.concatenate([w[:, :, 0], z], axis=-1)
        bot = jnp.concatenate([z, w[:, :, 1]], axis=-1)
        return jnp.concatenate([top, bot], axis=-2)
    return jnp.concatenate([pair_diag(wa), pair_diag(wx)], axis=-1).astype(BF16)


def kernel(x, norm_gains, ffn_w13, ffn_w2, w_in, b_gate, conv_w, conv_b, lru_wa, lru_ba,
           lru_wx, lru_bx, lru_lambda, w_lru_up, w_attn_up, w_out):
    bsz, seq, d = x.shape
    depth = norm_gains.shape[0]
    t = bsz * seq
    assert d == D_MODEL and seq % MOBA_BLOCK == 0 and seq % TM_IN == 0 and seq % TS_LRU == 0

    w13 = ffn_w13.astype(BF16)
    w2 = ffn_w2.astype(BF16)
    w_in_b = w_in.astype(BF16)
    w_lru_up_b = w_lru_up.astype(BF16)
    w_attn_up_b = w_attn_up.astype(BF16)
    w_out_b = w_out.astype(BF16)
    w_gate = _gate_weights(lru_wa, lru_wx)
    b_gate3 = b_gate.reshape(depth, 1, 2 * D_MODEL)
    conv_b3 = conv_b.reshape(depth, 1, D_LRU)
    ba3 = lru_ba.reshape(depth, 1, D_LRU)
    bx3 = lru_bx.reshape(depth, 1, D_LRU)
    lam3 = lru_lambda.reshape(depth, 1, D_LRU)
    rope_tables = _rope_tables(seq)

    xs = x.reshape(t, d)
    for l in range(depth):
        gain = lambda k: norm_gains[l, k].reshape(1, d)
        xs, xn = _ffn(xs, gain(0), gain(1), gain(2), w13, w2, l, 0)
        z, h = _inproj(xn, w_in_b, b_gate3, rope_tables, conv_w, conv_b3, w_gate, ba3, bx3,
                       lam3, l, seq)
        o = _moba(z.reshape(bsz, seq, IN_COLS))
        xs = _mix(xs, gain(3), h, o.reshape(t, D_ATTN), z,
                  w_lru_up_b, w_attn_up_b, w_out_b, l)
        xs = _ffn(xs, gain(4), gain(5), None, w13, w2, l, 1)
    return xs.reshape(bsz, seq, d)
```

```python
import functools

import jax
import jax.numpy as jnp
from jax import lax
from jax.experimental import pallas as pl
from jax.experimental.pallas import tpu as pltpu

F32 = jnp.float32
BF16 = jnp.bfloat16

D_MODEL = 2048
D_LRU = 1024
LRU_BLOCKS = 16
LRU_BW = D_LRU // LRU_BLOCKS
CONV_WIDTH = 4
LRU_C = 8.0
N_HEADS = 8
HEAD_DIM = 128
D_ATTN = N_HEADS * HEAD_DIM
MOBA_BLOCK = 256
MOBA_TOPK = 3
ROPE_THETA = 500000.0
ROT_DIM = HEAD_DIM // 4
ROT_HALF = ROT_DIM // 2
D_FF = 5632
NORM_EPS = 1e-6
IN_COLS = D_LRU + 3 * D_ATTN + 2 * D_MODEL

SUBLANES = 8
VMEM_LIMIT_BYTES = 62 * 1024 * 1024

TM = 512
TM_FFN = 1024
ROWS_FFN = 512
TF = 512
TM_IN = 1024
TN_IN = 1024
N_PIECES = 4
PIECE_COLS = TN_IN // N_PIECES
ROWS_MIX = 256
TS_LRU = 512
MOBA_Q_BLOCKS_PER_STEP = 8
GATE_GROUP = 2 * LRU_BW
N_GATE_GROUPS = D_LRU // GATE_GROUP


def _rms_scale(x):
    return lax.rsqrt(jnp.mean(x * x, axis=-1, keepdims=True) + NORM_EPS)


def _params(semantics):
    return pltpu.CompilerParams(dimension_semantics=semantics,
                                vmem_limit_bytes=VMEM_LIMIT_BYTES)


def _ffn_kernel(x_ref, gin_ref, gout_ref, gnext_ref, wa_ref, wb_ref, w2_ref, o_ref, *rest):
    next_ref = rest[0] if len(rest) == 2 else None
    xn_ref = rest[-1]
    j = pl.program_id(1)
    last = pl.num_programs(1) - 1

    def step(is_first, is_last):
        for r in range(TM_FFN // ROWS_FFN):
            rows = slice(r * ROWS_FFN, (r + 1) * ROWS_FFN)
            if is_first:
                x = x_ref[rows, :]
                xn = (x * _rms_scale(x) * gin_ref[...]).astype(BF16)
                xn_ref[rows, :] = xn
            else:
                xn = xn_ref[rows, :]
            a = jnp.dot(xn, wa_ref[...], preferred_element_type=F32)
            b = jnp.dot(xn, wb_ref[...], preferred_element_type=F32)
            h = (a * jax.nn.sigmoid(a) * b).astype(BF16)
            y = jnp.dot(h, w2_ref[...], preferred_element_type=F32)
            if not is_first:
                y = y + o_ref[rows, :]
            if is_last:
                y = x_ref[rows, :] + 0.5 * (y * _rms_scale(y) * gout_ref[...])
                if next_ref is not None:
                    next_ref[rows, :] = (y * _rms_scale(y) * gnext_ref[...]).astype(BF16)
            o_ref[rows, :] = y

    pl.when(j == 0)(functools.partial(step, True, False))
    pl.when((j > 0) & (j < last))(functools.partial(step, False, False))
    pl.when(j == last)(functools.partial(step, False, True))


def _ffn(x, gin, gout, gnext, w13, w2):
    t = x.shape[0]
    nf = D_FF // TF
    row_block = pl.BlockSpec((TM_FFN, D_MODEL), lambda i, j: (i, 0))
    gain_block = pl.BlockSpec((1, D_MODEL), lambda i, j: (0, 0))
    out_specs, out_shape = row_block, jax.ShapeDtypeStruct((t, D_MODEL), F32)
    if gnext is not None:
        out_specs = [row_block, row_block]
        out_shape = [out_shape, jax.ShapeDtypeStruct((t, D_MODEL), BF16)]
    return pl.pallas_call(
        _ffn_kernel,
        grid=(t // TM_FFN, nf),
        in_specs=[
            row_block, gain_block, gain_block, gain_block,
            pl.BlockSpec((D_MODEL, TF), lambda i, j: (0, j)),
            pl.BlockSpec((D_MODEL, TF), lambda i, j: (0, j + nf)),
            pl.BlockSpec((TF, D_MODEL), lambda i, j: (j, 0)),
        ],
        out_specs=out_specs,
        out_shape=out_shape,
        scratch_shapes=[pltpu.VMEM((TM_FFN, D_MODEL), BF16)],
        compiler_params=_params(("parallel", "arbitrary")),
        name="ffn_half_step",
    )(x, gin, gout, gout if gnext is None else gnext, w13, w13, w2)


LOG2_E = 1.4426950408889634
Q_SCALE = HEAD_DIM ** -0.5 * LOG2_E


def _lru_conv(chunk, piece, ubuf_ref, cw_ref, cb_ref, xc_ref):
    ts = TS_LRU
    base = SUBLANES + chunk * ts
    cols = slice(piece * PIECE_COLS, (piece + 1) * PIECE_COLS)
    xc = cb_ref[:, cols]
    for tap in range(CONV_WIDTH):
        off = base - (CONV_WIDTH - 1) + tap
        xc = xc + ubuf_ref[off:off + ts, cols] * cw_ref[tap:tap + 1, cols]
    xc_ref[:, cols] = xc


def _lru_gates(piece, xc_ref, wg_ref, ba_ref, bx_ref, lam_ref, a_ref, b_ref):
    groups_per_piece = PIECE_COLS // GATE_GROUP
    for g in range(piece * groups_per_piece, (piece + 1) * groups_per_piece):
        cols = slice(g * GATE_GROUP, (g + 1) * GATE_GROUP)
        xg = xc_ref[:, cols]
        lin = jnp.dot(xg.astype(BF16), wg_ref[g], preferred_element_type=F32)
        r = jax.nn.sigmoid(lin[:, :GATE_GROUP] + ba_ref[:, cols])
        i = jax.nn.sigmoid(lin[:, GATE_GROUP:] + bx_ref[:, cols])
        a = jnp.exp(r * (-LRU_C * jax.nn.softplus(-lam_ref[:, cols])))
        a_ref[:, cols] = a
        b_ref[:, cols] = jnp.sqrt(1.0 - a * a) * (i * xg)


def _lru_scan(chunk, piece, a_ref, b_ref, carry_ref, h_ref):
    steps = TS_LRU // N_PIECES
    h = carry_ref[...]
    for t in range(piece * steps, (piece + 1) * steps):
        h = a_ref[t:t + 1, :] * h + b_ref[t:t + 1, :]
        b_ref[t:t + 1, :] = h
    carry_ref[...] = h
    rows = slice(piece * steps, (piece + 1) * steps)
    out0 = chunk * TS_LRU + piece * steps
    h_ref[out0:out0 + steps, :] = b_ref[rows, :].astype(BF16)


def _inproj_kernel(tiles_per_seq, n_cast, xn_ref, w_ref, bg_ref, cos_ref, sin_lo_ref,
                   sin_hi_ref, cw_ref, cb_ref, wg_ref, ba_ref, bx_ref, lam_ref, *rest):
    cast_in = rest[:n_cast]
    z_ref, h_ref = rest[n_cast:n_cast + 2]
    cast_out = rest[n_cast + 2:2 * n_cast + 2]
    ubuf_ref, xc_ref, a_ref, b_ref, carry_ref = rest[2 * n_cast + 2:]
    i = pl.program_id(0)
    j = pl.program_id(1)
    for src_ref, dst_ref in zip(cast_in, cast_out):
        dst_ref[...] = src_ref[...].astype(BF16)
    q_blk = D_LRU // TN_IN
    k_blk = (D_LRU + D_ATTN) // TN_IN
    gate_blk = (D_LRU + 3 * D_ATTN) // TN_IN
    n_chunks = TM_IN // TS_LRU
    pad = SUBLANES
    conv = functools.partial(_lru_conv, ubuf_ref=ubuf_ref, cw_ref=cw_ref, cb_ref=cb_ref,
                             xc_ref=xc_ref)
    gates = functools.partial(_lru_gates, xc_ref=xc_ref, wg_ref=wg_ref, ba_ref=ba_ref,
                              bx_ref=bx_ref, lam_ref=lam_ref, a_ref=a_ref, b_ref=b_ref)
    scan = functools.partial(_lru_scan, a_ref=a_ref, b_ref=b_ref, carry_ref=carry_ref,
                             h_ref=h_ref)
    lru_work = {}
    for c in range(n_chunks):
        lru_work.setdefault(q_blk + 2 * c, []).append(functools.partial(conv, c))
        lru_work.setdefault(q_blk + 2 * c + 1, []).append(gates)
        lru_work.setdefault(q_blk + 2 * c + 2, []).append(functools.partial(scan, c))
    assert max(lru_work) < IN_COLS // TN_IN and q_blk == 1

    def project(w, cols=slice(None)):
        return jnp.dot(xn_ref[...], w[:, cols], preferred_element_type=F32)

    def step_body(step, lru_stages):
        for p in range(N_PIECES):
            cols = slice(p * PIECE_COLS, (p + 1) * PIECE_COLS)
            acc = project(w_ref, cols)
            if step in (q_blk, k_blk):
                heads = []
                for hh in range(PIECE_COLS // HEAD_DIM):
                    xh = acc[:, hh * HEAD_DIM:(hh + 1) * HEAD_DIM]
                    up = pltpu.roll(xh, HEAD_DIM - ROT_HALF, 1)
                    down = pltpu.roll(xh, ROT_HALF, 1)
                    heads.append(xh * cos_ref[...] + up * sin_lo_ref[...] + down * sin_hi_ref[...])
                acc = jnp.concatenate(heads, axis=1)
                if step == q_blk:
                    acc = acc * Q_SCALE
            elif step >= gate_blk:
                acc = jax.nn.sigmoid(acc + bg_ref[:, cols])
            z_ref[:, cols] = acc.astype(BF16)
            for stage in lru_stages:
                stage(p)

    @pl.when(j == 0)
    def _():
        ubuf_ref[0:pad, :] = ubuf_ref[TM_IN:TM_IN + pad, :]

        @pl.when(i % tiles_per_seq == 0)
        def _():
            ubuf_ref[0:pad, :] = jnp.zeros((pad, D_LRU), F32)
            carry_ref[...] = jnp.zeros_like(carry_ref)

        u = project(w_ref)
        ubuf_ref[pad:pad + TM_IN, :] = u
        z_ref[...] = u.astype(BF16)

    special = sorted(set(lru_work) | {q_blk, k_blk, gate_blk - 1, gate_blk})
    for step in special:
        pl.when(j == step)(functools.partial(step_body, step, lru_work.get(step, [])))
    pl.when(j > max(special))(functools.partial(step_body, gate_blk, []))


def _inproj(xn, w_in, b_gate, rope_tables, conv_w, conv_b, w_gate, ba, bx, lam, layer, seq,
            ffn_w13, ffn_w2_rows, cast_jobs):
    t = xn.shape[0]
    n_steps = (t // TM_IN) * (IN_COLS // TN_IN)
    slab = D_MODEL // n_steps
    assert slab * n_steps == D_MODEL and slab % (2 * SUBLANES) == 0
    step_id = lambda i, j: i * (IN_COLS // TN_IN) + j
    cast_args, cast_in_specs, cast_out_specs, cast_out_shapes = [], [], [], []
    for (cl, ch) in cast_jobs:
        for arr in (ffn_w13, ffn_w2_rows):
            cols = arr.shape[-1]
            cast_args.append(arr)
            cast_in_specs.append(pl.BlockSpec((None, None, slab, cols),
                                              lambda i, j, cl=cl, ch=ch: (cl, ch, step_id(i, j), 0)))
            cast_out_specs.append(pl.BlockSpec((slab, cols), lambda i, j: (step_id(i, j), 0)))
            cast_out_shapes.append(jax.ShapeDtypeStruct((D_MODEL, cols), BF16))
    gate_blk = (D_LRU + 3 * D_ATTN) // TN_IN
    seq_tiles = seq // TM_IN
    table = lambda: pl.BlockSpec((TM_IN, HEAD_DIM), lambda i, j: (i % seq_tiles, 0))
    vec = lambda: pl.BlockSpec((None, 1, D_LRU), lambda i, j: (layer, 0, 0))
    return pl.pallas_call(
        functools.partial(_inproj_kernel, seq_tiles, len(cast_args)),
        grid=(t // TM_IN, IN_COLS // TN_IN),
        in_specs=[
            pl.BlockSpec((TM_IN, D_MODEL), lambda i, j: (i, 0)),
            pl.BlockSpec((None, D_MODEL, TN_IN), lambda i, j: (layer, 0, j)),
            pl.BlockSpec((None, 1, TN_IN), lambda i, j: (layer, 0, jnp.maximum(j - gate_blk, 0))),
            table(), table(), table(),
            pl.BlockSpec((None, CONV_WIDTH, D_LRU), lambda i, j: (layer, 0, 0)),
            vec(),
            pl.BlockSpec((None, N_GATE_GROUPS, GATE_GROUP, 2 * GATE_GROUP),
                         lambda i, j: (layer, 0, 0, 0)),
            vec(), vec(), vec(),
        ] + cast_in_specs,
        out_specs=[
            pl.BlockSpec((TM_IN, TN_IN), lambda i, j: (i, j)),
            pl.BlockSpec((TM_IN, D_LRU), lambda i, j: (i, 0)),
        ] + cast_out_specs,
        out_shape=[jax.ShapeDtypeStruct((t, IN_COLS), BF16),
                   jax.ShapeDtypeStruct((t, D_LRU), BF16)] + cast_out_shapes,
        scratch_shapes=[
            pltpu.VMEM((SUBLANES + TM_IN, D_LRU), F32),
            pltpu.VMEM((TS_LRU, D_LRU), F32),
            pltpu.VMEM((TS_LRU, D_LRU), F32),
            pltpu.VMEM((TS_LRU, D_LRU), F32),
            pltpu.VMEM((1, D_LRU), F32),
        ],
        compiler_params=_params(("arbitrary", "arbitrary")),
        name="mixer_in_projection_and_rg_lru",
    )(xn, w_in, b_gate, *rope_tables, conv_w, conv_b, w_gate, ba, bx, lam, *cast_args)


def _nt_dot(a, b):
    return lax.dot_general(a, b, (((1,), (1,)), ((), ())), preferred_element_type=F32)


def _moba_prepare(q_ref, k_ref, v_ref, bias_ref, causal_ref, vt_ref):
    nb, seq = bias_ref.shape
    blk = MOBA_BLOCK
    neg = -jnp.inf
    kmean = jnp.mean(k_ref[0].astype(F32).reshape(nb, blk, HEAD_DIM), axis=1)
    hi = kmean.astype(BF16)
    lo = (kmean - hi.astype(F32)).astype(BF16)
    q_all = q_ref[0]
    gate = _nt_dot(hi, q_all) + _nt_dot(lo, q_all)
    blk_id = lax.broadcasted_iota(jnp.int32, (nb, seq), 0)
    qry_blk = lax.broadcasted_iota(jnp.int32, (nb, seq), 1) // blk
    gate = jnp.where(blk_id < qry_blk, gate, neg)
    bias = jnp.full((nb, seq), neg, F32)
    for _ in range(MOBA_TOPK):
        top = jnp.max(gate, axis=0, keepdims=True)
        first = jnp.min(jnp.where(gate == top, blk_id, nb), axis=0, keepdims=True)
        pick = (blk_id == first) & (top > neg)
        bias = jnp.where(pick, 0.0, bias)
        gate = jnp.where(pick, neg, gate)
    bias_ref[...] = bias
    key_pos = lax.broadcasted_iota(jnp.int32, (blk, blk), 0)
    qry_pos = lax.broadcasted_iota(jnp.int32, (blk, blk), 1)
    causal_ref[...] = jnp.where(key_pos <= qry_pos, 0.0, neg)
    for n in range(nb):
        cols = slice(n * blk, (n + 1) * blk)
        vt_ref[0:HEAD_DIM, cols] = v_ref[0, cols, :].astype(F32).T.astype(BF16)
    vt_ref[HEAD_DIM:, :] = jnp.ones((vt_ref.shape[0] - HEAD_DIM, seq), BF16)


def _moba_scores(j, slot, q_ref, k_ref, bias_ref, causal_ref, s_ref, m_ref, shift_ref):
    blk = MOBA_BLOCK
    lo, hi = j * blk, (j + 1) * blk
    q = q_ref[0, lo:hi, :]
    s_own = _nt_dot(k_ref[0, lo:hi, :], q) + causal_ref[...]
    s_ref[slot, lo:hi, :] = s_own
    m = jnp.max(s_own, axis=0, keepdims=True)
    if j > 0:
        s_past = _nt_dot(k_ref[0, 0:lo, :], q)
        s_ref[slot, 0:lo, :] = s_past
        bias = bias_ref[0:j, lo:hi]
        blk_max = jnp.max(s_past.reshape(j, blk, blk), axis=1)
        m = jnp.maximum(m, jnp.max(blk_max + bias, axis=0, keepdims=True))
        shift_ref[slot, 0:j, :] = m - bias
    m_ref[slot] = m


def _moba_output(i, slot, out_row, o_ref, vt_ref, s_ref, m_ref, shift_ref):
    blk = MOBA_BLOCK
    lo, hi = i * blk, (i + 1) * blk
    p_own = jnp.exp2(s_ref[slot, lo:hi, :] - m_ref[slot])
    acc = jnp.dot(vt_ref[:, lo:hi], p_own.astype(BF16), preferred_element_type=F32)
    if i > 0:
        s_past = s_ref[slot, 0:lo, :].reshape(i, blk, blk)
        p_past = jnp.exp2(s_past - shift_ref[slot, 0:i, :][:, None, :]).reshape(lo, blk)
        acc = acc + jnp.dot(vt_ref[:, 0:lo], p_past.astype(BF16), preferred_element_type=F32)
    inv_l = 1.0 / acc[HEAD_DIM:HEAD_DIM + 1, :]
    o_ref[0, out_row:out_row + blk, :] = (acc[0:HEAD_DIM, :] * inv_l).T.astype(BF16)


def _moba_kernel(q_ref, k_ref, v_ref, o_ref, bias_ref, causal_ref, vt_ref, s_ref, m_ref,
                 shift_ref):
    qi = pl.program_id(2)
    nb = bias_ref.shape[0]

    @pl.when(qi == 0)
    def _():
        _moba_prepare(q_ref, k_ref, v_ref, bias_ref, causal_ref, vt_ref)
        _moba_scores(0, 0, q_ref, k_ref, bias_ref, causal_ref, s_ref, m_ref, shift_ref)

    for g in range(nb // MOBA_Q_BLOCKS_PER_STEP):
        @pl.when(qi == g)
        def _(g=g):
            for r in range(MOBA_Q_BLOCKS_PER_STEP):
                i = g * MOBA_Q_BLOCKS_PER_STEP + r
                if i + 1 < nb:
                    _moba_scores(i + 1, (i + 1) % 2, q_ref, k_ref, bias_ref, causal_ref,
                                 s_ref, m_ref, shift_ref)
                _moba_output(i, i % 2, r * MOBA_BLOCK, o_ref, vt_ref, s_ref, m_ref, shift_ref)


def _moba(z3):
    bsz, seq, _ = z3.shape
    nb = seq // MOBA_BLOCK
    q_col = D_LRU // HEAD_DIM
    k_col = (D_LRU + D_ATTN) // HEAD_DIM
    v_col = (D_LRU + 2 * D_ATTN) // HEAD_DIM
    ones_rows = 2 * SUBLANES
    return pl.pallas_call(
        _moba_kernel,
        grid=(bsz, N_HEADS, nb // MOBA_Q_BLOCKS_PER_STEP),
        in_specs=[
            pl.BlockSpec((1, seq, HEAD_DIM), lambda b, h, i: (b, 0, q_col + h)),
            pl.BlockSpec((1, seq, HEAD_DIM), lambda b, h, i: (b, 0, k_col + h)),
            pl.BlockSpec((1, seq, HEAD_DIM), lambda b, h, i: (b, 0, v_col + h)),
        ],
        out_specs=pl.BlockSpec((1, MOBA_Q_BLOCKS_PER_STEP * MOBA_BLOCK, HEAD_DIM),
                               lambda b, h, i: (b, i, h)),
        out_shape=jax.ShapeDtypeStruct((bsz, seq, D_ATTN), BF16),
        scratch_shapes=[
            pltpu.VMEM((nb, seq), F32),
            pltpu.VMEM((MOBA_BLOCK, MOBA_BLOCK), F32),
            pltpu.VMEM((HEAD_DIM + ones_rows, seq), BF16),
            pltpu.VMEM((2, seq, MOBA_BLOCK), F32),
            pltpu.VMEM((2, 1, MOBA_BLOCK), F32),
            pltpu.VMEM((2, nb, MOBA_BLOCK), F32),
        ],
        compiler_params=_params(("parallel", "parallel", "arbitrary")),
        name="moba_attention",
    )(z3, z3, z3)


def _mix_kernel(x_ref, g_ref, h_ref, o_ref, gate_ref, wl_ref, wa_ref, wo_ref, out_ref):
    for r in range(TM // ROWS_MIX):
        rows = slice(r * ROWS_MIX, (r + 1) * ROWS_MIX)
        ya = jnp.dot(h_ref[rows, :], wl_ref[...], preferred_element_type=F32)
        yb = jnp.dot(o_ref[rows, :], wa_ref[...], preferred_element_type=F32)
        mix = (gate_ref[rows, 0:D_MODEL].astype(F32) * ya
               + gate_ref[rows, D_MODEL:2 * D_MODEL].astype(F32) * yb)
        y = jnp.dot(mix.astype(BF16), wo_ref[...], preferred_element_type=F32)
        out_ref[rows, :] = x_ref[rows, :] + y * _rms_scale(y) * g_ref[...]


def _mix(x, g, h, o, z, w_lru_up, w_attn_up, w_out, layer):
    t = x.shape[0]
    gate_blk = (D_LRU + 3 * D_ATTN) // (2 * D_MODEL)
    resident = lambda rows, cols: pl.BlockSpec((None, rows, cols), lambda i: (layer, 0, 0),
                                               pipeline_mode=pl.Buffered(1))
    return pl.pallas_call(
        _mix_kernel,
        grid=(t // TM,),
        in_specs=[
            pl.BlockSpec((TM, D_MODEL), lambda i: (i, 0)),
            pl.BlockSpec((1, D_MODEL), lambda i: (0, 0)),
            pl.BlockSpec((TM, D_LRU), lambda i: (i, 0)),
            pl.BlockSpec((TM, D_ATTN), lambda i: (i, 0)),
            pl.BlockSpec((TM, 2 * D_MODEL), lambda i: (i, gate_blk)),
            resident(D_LRU, D_MODEL),
            resident(D_ATTN, D_MODEL),
            resident(D_MODEL, D_MODEL),
        ],
        out_specs=pl.BlockSpec((TM, D_MODEL), lambda i: (i, 0)),
        out_shape=jax.ShapeDtypeStruct((t, D_MODEL), F32),
        compiler_params=_params(("parallel",)),
        name="mixer_merge_out_projection",
    )(x, g, h, o, z, w_lru_up, w_attn_up, w_out)


def _rope_tables(seq):
    pos = jnp.arange(seq, dtype=F32)
    inv = ROPE_THETA ** (-jnp.arange(0, ROT_DIM, 2, dtype=F32) / ROT_DIM)
    ang = pos[:, None] * inv[None, :]
    cos, sin = jnp.cos(ang), jnp.sin(ang)
    rest = HEAD_DIM - ROT_DIM
    zeros = jnp.zeros((seq, ROT_HALF), F32)
    tail0 = jnp.zeros((seq, rest), F32)
    cos_t = jnp.concatenate([cos, cos, jnp.ones((seq, rest), F32)], axis=1)
    sin_lo_t = jnp.concatenate([-sin, zeros, tail0], axis=1)
    sin_hi_t = jnp.concatenate([zeros, sin, tail0], axis=1)
    return cos_t, sin_lo_t, sin_hi_t


def _gate_weights(wa, wx):
    def pair_diag(w):
        depth = w.shape[0]
        w = w.reshape(depth, N_GATE_GROUPS, 2, LRU_BW, LRU_BW)
        z = jnp.zeros_like(w[:, :, 0])
        top = jnp.concatenate([w[:, :, 0], z], axis=-1)
        bot = jnp.concatenate([z, w[:, :, 1]], axis=-1)
        return jnp.concatenate([top, bot], axis=-2)
    return jnp.concatenate([pair_diag(wa), pair_diag(wx)], axis=-1).astype(BF16)


def kernel(x, norm_gains, ffn_w13, ffn_w2, w_in, b_gate, conv_w, conv_b, lru_wa, lru_ba,
           lru_wx, lru_bx, lru_lambda, w_lru_up, w_attn_up, w_out):
    bsz, seq, d = x.shape
    depth = norm_gains.shape[0]
    t = bsz * seq
    assert d == D_MODEL and seq % MOBA_BLOCK == 0 and seq % TM_IN == 0 and seq % TS_LRU == 0

    ffn_w2_rows = ffn_w2.reshape(depth, 2, D_MODEL, D_FF)
    ffn_weights = {(0, 0): (ffn_w13[0, 0].astype(BF16), ffn_w2[0, 0].astype(BF16))}
    w_in_b = w_in.astype(BF16)
    w_lru_up_b = w_lru_up.astype(BF16)
    w_attn_up_b = w_attn_up.astype(BF16)
    w_out_b = w_out.astype(BF16)
    w_gate = _gate_weights(lru_wa, lru_wx)
    b_gate3 = b_gate.reshape(depth, 1, 2 * D_MODEL)
    conv_b3 = conv_b.reshape(depth, 1, D_LRU)
    ba3 = lru_ba.reshape(depth, 1, D_LRU)
    bx3 = lru_bx.reshape(depth, 1, D_LRU)
    lam3 = lru_lambda.reshape(depth, 1, D_LRU)
    rope_tables = _rope_tables(seq)

    xs = x.reshape(t, d)
    for l in range(depth):
        gain = lambda k: norm_gains[l, k].reshape(1, d)
        xs, xn = _ffn(xs, gain(0), gain(1), gain(2), *ffn_weights[l, 0])
        cast_jobs = [(l, 1)] + ([(l + 1, 0)] if l + 1 < depth else [])
        z, h, *casts = _inproj(xn, w_in_b, b_gate3, rope_tables, conv_w, conv_b3, w_gate, ba3,
                               bx3, lam3, l, seq, ffn_w13, ffn_w2_rows, cast_jobs)
        for n, job in enumerate(cast_jobs):
            ffn_weights[job] = (casts[2 * n], casts[2 * n + 1].reshape(D_FF, D_MODEL))
        o = _moba(z.reshape(bsz, seq, IN_COLS))
        xs = _mix(xs, gain(3), h, o.reshape(t, D_ATTN), z,
                  w_lru_up_b, w_attn_up_b, w_out_b, l)
        xs = _ffn(xs, gain(4), gain(5), None, *ffn_weights[l, 1])
    return xs.reshape(bsz, seq, d)
```

```python
import functools

import jax
import jax.numpy as jnp
from jax import lax
from jax.experimental import pallas as pl
from jax.experimental.pallas import tpu as pltpu

F32 = jnp.float32
BF16 = jnp.bfloat16

D_MODEL = 2048
D_LRU = 1024
LRU_BLOCKS = 16
LRU_BW = D_LRU // LRU_BLOCKS
CONV_WIDTH = 4
LRU_C = 8.0
N_HEADS = 8
HEAD_DIM = 128
D_ATTN = N_HEADS * HEAD_DIM
MOBA_BLOCK = 256
MOBA_TOPK = 3
ROPE_THETA = 500000.0
ROT_DIM = HEAD_DIM // 4
ROT_HALF = ROT_DIM // 2
D_FF = 5632
NORM_EPS = 1e-6
IN_COLS = D_LRU + 3 * D_ATTN + 2 * D_MODEL

SUBLANES = 8
VMEM_LIMIT_BYTES = 62 * 1024 * 1024

TM = 512
TM_FFN = 1024
ROWS_FFN = 512
TF = 512
TM_IN = 1024
TN_IN = 1024
N_PIECES = 4
PIECE_COLS = TN_IN // N_PIECES
ROWS_MIX = 256
TS_LRU = 512
MOBA_Q_BLOCKS_PER_STEP = 8
GATE_GROUP = 2 * LRU_BW
N_GATE_GROUPS = D_LRU // GATE_GROUP


def _rms_scale(x):
    return lax.rsqrt(jnp.mean(x * x, axis=-1, keepdims=True) + NORM_EPS)


def _params(semantics):
    return pltpu.CompilerParams(dimension_semantics=semantics,
                                vmem_limit_bytes=VMEM_LIMIT_BYTES)


def _ffn_kernel(x_ref, gin_ref, gout_ref, gnext_ref, wa_ref, wb_ref, w2_ref, o_ref, *rest):
    next_ref = rest[0] if len(rest) == 2 else None
    xn_ref = rest[-1]
    j = pl.program_id(1)
    last = pl.num_programs(1) - 1

    def step(is_first, is_last):
        for r in range(TM_FFN // ROWS_FFN):
            rows = slice(r * ROWS_FFN, (r + 1) * ROWS_FFN)
            if is_first:
                x = x_ref[rows, :]
                xn = (x * _rms_scale(x) * gin_ref[...]).astype(BF16)
                xn_ref[rows, :] = xn
            else:
                xn = xn_ref[rows, :]
            a = jnp.dot(xn, wa_ref[...], preferred_element_type=F32)
            b = jnp.dot(xn, wb_ref[...], preferred_element_type=F32)
            h = (a * jax.nn.sigmoid(a) * b).astype(BF16)
            y = jnp.dot(h, w2_ref[...], preferred_element_type=F32)
            if not is_first:
                y = y + o_ref[rows, :]
            if is_last:
                y = x_ref[rows, :] + 0.5 * (y * _rms_scale(y) * gout_ref[...])
                if next_ref is not None:
                    next_ref[rows, :] = (y * _rms_scale(y) * gnext_ref[...]).astype(BF16)
            o_ref[rows, :] = y

    pl.when(j == 0)(functools.partial(step, True, False))
    pl.when((j > 0) & (j < last))(functools.partial(step, False, False))
    pl.when(j == last)(functools.partial(step, False, True))


def _ffn(x, gin, gout, gnext, w13, w2):
    t = x.shape[0]
    nf = D_FF // TF
    row_block = pl.BlockSpec((TM_FFN, D_MODEL), lambda i, j: (i, 0))
    gain_block = pl.BlockSpec((1, D_MODEL), lambda i, j: (0, 0))
    out_specs, out_shape = row_block, jax.ShapeDtypeStruct((t, D_MODEL), F32)
    if gnext is not None:
        out_specs = [row_block, row_block]
        out_shape = [out_shape, jax.ShapeDtypeStruct((t, D_MODEL), BF16)]
    return pl.pallas_call(
        _ffn_kernel,
        grid=(t // TM_FFN, nf),
        in_specs=[
            row_block, gain_block, gain_block, gain_block,
            pl.BlockSpec((D_MODEL, TF), lambda i, j: (0, j)),
            pl.BlockSpec((D_MODEL, TF), lambda i, j: (0, j + nf)),
            pl.BlockSpec((TF, D_MODEL), lambda i, j: (j, 0)),
        ],
        out_specs=out_specs,
        out_shape=out_shape,
        scratch_shapes=[pltpu.VMEM((TM_FFN, D_MODEL), BF16)],
        compiler_params=_params(("parallel", "arbitrary")),
        name="ffn_half_step",
    )(x, gin, gout, gout if gnext is None else gnext, w13, w13, w2)


LOG2_E = 1.4426950408889634
Q_SCALE = HEAD_DIM ** -0.5 * LOG2_E


def _lru_conv(chunk, piece, ubuf_ref, cw_ref, cb_ref, xc_ref):
    ts = TS_LRU
    base = SUBLANES + chunk * ts
    cols = slice(piece * PIECE_COLS, (piece + 1) * PIECE_COLS)
    xc = cb_ref[:, cols]
    for tap in range(CONV_WIDTH):
        off = base - (CONV_WIDTH - 1) + tap
        xc = xc + ubuf_ref[off:off + ts, cols] * cw_ref[tap:tap + 1, cols]
    xc_ref[:, cols] = xc


def _lru_gates(piece, xc_ref, wg_ref, ba_ref, bx_ref, lam_ref, a_ref, b_ref):
    groups_per_piece = PIECE_COLS // GATE_GROUP
    for g in range(piece * groups_per_piece, (piece + 1) * groups_per_piece):
        cols = slice(g * GATE_GROUP, (g + 1) * GATE_GROUP)
        xg = xc_ref[:, cols]
        lin = jnp.dot(xg.astype(BF16), wg_ref[g], preferred_element_type=F32)
        r = jax.nn.sigmoid(lin[:, :GATE_GROUP] + ba_ref[:, cols])
        i = jax.nn.sigmoid(lin[:, GATE_GROUP:] + bx_ref[:, cols])
        a = jnp.exp(r * (-LRU_C * jax.nn.softplus(-lam_ref[:, cols])))
        a_ref[:, cols] = a
        b_ref[:, cols] = jnp.sqrt(1.0 - a * a) * (i * xg)


def _lru_scan(chunk, piece, a_ref, b_ref, carry_ref, h_ref):
    steps = TS_LRU // N_PIECES
    h = carry_ref[...]
    for t in range(piece * steps, (piece + 1) * steps):
        h = a_ref[t:t + 1, :] * h + b_ref[t:t + 1, :]
        b_ref[t:t + 1, :] = h
    carry_ref[...] = h
    rows = slice(piece * steps, (piece + 1) * steps)
    out0 = chunk * TS_LRU + piece * steps
    h_ref[out0:out0 + steps, :] = b_ref[rows, :].astype(BF16)


def _inproj_kernel(tiles_per_seq, n_cast, xn_ref, w_ref, bg_ref, cos_ref, sin_lo_ref,
                   sin_hi_ref, cw_ref, cb_ref, wg_ref, ba_ref, bx_ref, lam_ref, *rest):
    cast_in = rest[:n_cast]
    z_ref, h_ref = rest[n_cast:n_cast + 2]
    cast_out = rest[n_cast + 2:2 * n_cast + 2]
    ubuf_ref, xc_ref, a_ref, b_ref, carry_ref = rest[2 * n_cast + 2:]
    i = pl.program_id(0)
    j = pl.program_id(1)
    for src_ref, dst_ref in zip(cast_in, cast_out):
        dst_ref[...] = src_ref[...].astype(BF16)
    q_blk = D_LRU // TN_IN
    k_blk = (D_LRU + D_ATTN) // TN_IN
    gate_blk = (D_LRU + 3 * D_ATTN) // TN_IN
    n_chunks = TM_IN // TS_LRU
    pad = SUBLANES
    conv = functools.partial(_lru_conv, ubuf_ref=ubuf_ref, cw_ref=cw_ref, cb_ref=cb_ref,
                             xc_ref=xc_ref)
    gates = functools.partial(_lru_gates, xc_ref=xc_ref, wg_ref=wg_ref, ba_ref=ba_ref,
                              bx_ref=bx_ref, lam_ref=lam_ref, a_ref=a_ref, b_ref=b_ref)
    scan = functools.partial(_lru_scan, a_ref=a_ref, b_ref=b_ref, carry_ref=carry_ref,
                             h_ref=h_ref)
    lru_work = {}
    for c in range(n_chunks):
        lru_work.setdefault(q_blk + 2 * c, []).append(functools.partial(conv, c))
        lru_work.setdefault(q_blk + 2 * c + 1, []).append(gates)
        lru_work.setdefault(q_blk + 2 * c + 2, []).append(functools.partial(scan, c))
    assert max(lru_work) < IN_COLS // TN_IN and q_blk == 1

    def project(w, cols=slice(None)):
        return jnp.dot(xn_ref[...], w[:, cols], preferred_element_type=F32)

    def step_body(step, lru_stages):
        for p in range(N_PIECES):
            cols = slice(p * PIECE_COLS, (p + 1) * PIECE_COLS)
            acc = project(w_ref, cols)
            if step in (q_blk, k_blk):
                heads = []
                for hh in range(PIECE_COLS // HEAD_DIM):
                    xh = acc[:, hh * HEAD_DIM:(hh + 1) * HEAD_DIM]
                    up = pltpu.roll(xh, HEAD_DIM - ROT_HALF, 1)
                    down = pltpu.roll(xh, ROT_HALF, 1)
                    heads.append(xh * cos_ref[...] + up * sin_lo_ref[...] + down * sin_hi_ref[...])
                acc = jnp.concatenate(heads, axis=1)
                if step == q_blk:
                    acc = acc * Q_SCALE
            elif step >= gate_blk:
                acc = jax.nn.sigmoid(acc + bg_ref[:, cols])
            z_ref[:, cols] = acc.astype(BF16)
            for stage in lru_stages:
                stage(p)

    @pl.when(j == 0)
    def _():
        ubuf_ref[0:pad, :] = ubuf_ref[TM_IN:TM_IN + pad, :]

        @pl.when(i % tiles_per_seq == 0)
        def _():
            ubuf_ref[0:pad, :] = jnp.zeros((pad, D_LRU), F32)
            carry_ref[...] = jnp.zeros_like(carry_ref)

        u = project(w_ref)
        ubuf_ref[pad:pad + TM_IN, :] = u
        z_ref[...] = u.astype(BF16)

    special = sorted(set(lru_work) | {q_blk, k_blk, gate_blk - 1, gate_blk})
    for step in special:
        pl.when(j == step)(functools.partial(step_body, step, lru_work.get(step, [])))
    pl.when(j > max(special))(functools.partial(step_body, gate_blk, []))


def _inproj(xn, w_in, b_gate, rope_tables, conv_w, conv_b, w_gate, ba, bx, lam, layer, seq,
            ffn_w13, ffn_w2, cast_jobs):
    t = xn.shape[0]
    n_i, n_j = t // TM_IN, IN_COLS // TN_IN
    slab = D_MODEL // (n_i * n_j)
    w2_rows, w2_cols = D_FF // n_j, D_MODEL // n_i
    assert slab * n_i * n_j == D_MODEL and slab % (2 * SUBLANES) == 0
    assert w2_rows * n_j == D_FF and w2_rows % (2 * SUBLANES) == 0 and w2_cols % HEAD_DIM == 0
    cast_args, cast_in_specs, cast_out_specs, cast_out_shapes = [], [], [], []
    for (cl, ch) in cast_jobs:
        cast_args += [ffn_w13, ffn_w2]
        cast_in_specs += [
            pl.BlockSpec((None, None, slab, 2 * D_FF),
                         lambda i, j, cl=cl, ch=ch: (cl, ch, i * n_j + j, 0)),
            pl.BlockSpec((None, None, w2_rows, w2_cols),
                         lambda i, j, cl=cl, ch=ch: (cl, ch, j, i)),
        ]
        cast_out_specs += [
            pl.BlockSpec((slab, 2 * D_FF), lambda i, j: (i * n_j + j, 0)),
            pl.BlockSpec((w2_rows, w2_cols), lambda i, j: (j, i)),
        ]
        cast_out_shapes += [jax.ShapeDtypeStruct((D_MODEL, 2 * D_FF), BF16),
                            jax.ShapeDtypeStruct((D_FF, D_MODEL), BF16)]
    gate_blk = (D_LRU + 3 * D_ATTN) // TN_IN
    seq_tiles = seq // TM_IN
    table = lambda: pl.BlockSpec((TM_IN, HEAD_DIM), lambda i, j: (i % seq_tiles, 0))
    vec = lambda: pl.BlockSpec((None, 1, D_LRU), lambda i, j: (layer, 0, 0))
    return pl.pallas_call(
        functools.partial(_inproj_kernel, seq_tiles, len(cast_args)),
        grid=(t // TM_IN, IN_COLS // TN_IN),
        in_specs=[
            pl.BlockSpec((TM_IN, D_MODEL), lambda i, j: (i, 0)),
            pl.BlockSpec((None, D_MODEL, TN_IN), lambda i, j: (layer, 0, j)),
            pl.BlockSpec((None, 1, TN_IN), lambda i, j: (layer, 0, jnp.maximum(j - gate_blk, 0))),
            table(), table(), table(),
            pl.BlockSpec((None, CONV_WIDTH, D_LRU), lambda i, j: (layer, 0, 0)),
            vec(),
            pl.BlockSpec((None, N_GATE_GROUPS, GATE_GROUP, 2 * GATE_GROUP),
                         lambda i, j: (layer, 0, 0, 0)),
            vec(), vec(), vec(),
        ] + cast_in_specs,
        out_specs=[
            pl.BlockSpec((TM_IN, TN_IN), lambda i, j: (i, j)),
            pl.BlockSpec((TM_IN, D_LRU), lambda i, j: (i, 0)),
        ] + cast_out_specs,
        out_shape=[jax.ShapeDtypeStruct((t, IN_COLS), BF16),
                   jax.ShapeDtypeStruct((t, D_LRU), BF16)] + cast_out_shapes,
        scratch_shapes=[
            pltpu.VMEM((SUBLANES + TM_IN, D_LRU), F32),
            pltpu.VMEM((TS_LRU, D_LRU), F32),
            pltpu.VMEM((TS_LRU, D_LRU), F32),
            pltpu.VMEM((TS_LRU, D_LRU), F32),
            pltpu.VMEM((1, D_LRU), F32),
        ],
        compiler_params=_params(("arbitrary", "arbitrary")),
        name="mixer_in_projection_and_rg_lru",
    )(xn, w_in, b_gate, *rope_tables, conv_w, conv_b, w_gate, ba, bx, lam, *cast_args)


def _nt_dot(a, b):
    return lax.dot_general(a, b, (((1,), (1,)), ((), ())), preferred_element_type=F32)


def _moba_prepare(q_ref, k_ref, v_ref, bias_ref, causal_ref, vt_ref):
    nb, seq = bias_ref.shape
    blk = MOBA_BLOCK
    neg = -jnp.inf
    kmean = jnp.mean(k_ref[0].astype(F32).reshape(nb, blk, HEAD_DIM), axis=1)
    hi = kmean.astype(BF16)
    lo = (kmean - hi.astype(F32)).astype(BF16)
    q_all = q_ref[0]
    gate = _nt_dot(hi, q_all) + _nt_dot(lo, q_all)
    blk_id = lax.broadcasted_iota(jnp.int32, (nb, seq), 0)
    qry_blk = lax.broadcasted_iota(jnp.int32, (nb, seq), 1) // blk
    gate = jnp.where(blk_id < qry_blk, gate, neg)
    bias = jnp.full((nb, seq), neg, F32)
    for _ in range(MOBA_TOPK):
        top = jnp.max(gate, axis=0, keepdims=True)
        first = jnp.min(jnp.where(gate == top, blk_id, nb), axis=0, keepdims=True)
        pick = (blk_id == first) & (top > neg)
        bias = jnp.where(pick, 0.0, bias)
        gate = jnp.where(pick, neg, gate)
    bias_ref[...] = bias
    key_pos = lax.broadcasted_iota(jnp.int32, (blk, blk), 0)
    qry_pos = lax.broadcasted_iota(jnp.int32, (blk, blk), 1)
    causal_ref[...] = jnp.where(key_pos <= qry_pos, 0.0, neg)
    for n in range(nb):
        cols = slice(n * blk, (n + 1) * blk)
        vt_ref[0:HEAD_DIM, cols] = v_ref[0, cols, :].astype(F32).T.astype(BF16)
    vt_ref[HEAD_DIM:, :] = jnp.ones((vt_ref.shape[0] - HEAD_DIM, seq), BF16)


def _moba_scores(j, slot, q_ref, k_ref, bias_ref, causal_ref, s_ref, m_ref, shift_ref):
    blk = MOBA_BLOCK
    lo, hi = j * blk, (j + 1) * blk
    q = q_ref[0, lo:hi, :]
    s_own = _nt_dot(k_ref[0, lo:hi, :], q) + causal_ref[...]
    s_ref[slot, lo:hi, :] = s_own
    m = jnp.max(s_own, axis=0, keepdims=True)
    if j > 0:
        s_past = _nt_dot(k_ref[0, 0:lo, :], q)
        s_ref[slot, 0:lo, :] = s_past
        bias = bias_ref[0:j, lo:hi]
        blk_max = jnp.max(s_past.reshape(j, blk, blk), axis=1)
        m = jnp.maximum(m, jnp.max(blk_max + bias, axis=0, keepdims=True))
        shift_ref[slot, 0:j, :] = m - bias
    m_ref[slot] = m


def _moba_output(i, slot, out_row, o_ref, vt_ref, s_ref, m_ref, shift_ref):
    blk = MOBA_BLOCK
    lo, hi = i * blk, (i + 1) * blk
    p_own = jnp.exp2(s_ref[slot, lo:hi, :] - m_ref[slot])
    acc = jnp.dot(vt_ref[:, lo:hi], p_own.astype(BF16), preferred_element_type=F32)
    if i > 0:
        s_past = s_ref[slot, 0:lo, :].reshape(i, blk, blk)
        p_past = jnp.exp2(s_past - shift_ref[slot, 0:i, :][:, None, :]).reshape(lo, blk)
        acc = acc + jnp.dot(vt_ref[:, 0:lo], p_past.astype(BF16), preferred_element_type=F32)
    inv_l = 1.0 / acc[HEAD_DIM:HEAD_DIM + 1, :]
    o_ref[0, out_row:out_row + blk, :] = (acc[0:HEAD_DIM, :] * inv_l).T.astype(BF16)


def _moba_kernel(q_ref, k_ref, v_ref, o_ref, bias_ref, causal_ref, vt_ref, s_ref, m_ref,
                 shift_ref):
    qi = pl.program_id(2)
    nb = bias_ref.shape[0]

    @pl.when(qi == 0)
    def _():
        _moba_prepare(q_ref, k_ref, v_ref, bias_ref, causal_ref, vt_ref)
        _moba_scores(0, 0, q_ref, k_ref, bias_ref, causal_ref, s_ref, m_ref, shift_ref)

    for g in range(nb // MOBA_Q_BLOCKS_PER_STEP):
        @pl.when(qi == g)
        def _(g=g):
            for r in range(MOBA_Q_BLOCKS_PER_STEP):
                i = g * MOBA_Q_BLOCKS_PER_STEP + r
                if i + 1 < nb:
                    _moba_scores(i + 1, (i + 1) % 2, q_ref, k_ref, bias_ref, causal_ref,
                                 s_ref, m_ref, shift_ref)
                _moba_output(i, i % 2, r * MOBA_BLOCK, o_ref, vt_ref, s_ref, m_ref, shift_ref)


def _moba(z3):
    bsz, seq, _ = z3.shape
    nb = seq // MOBA_BLOCK
    q_col = D_LRU // HEAD_DIM
    k_col = (D_LRU + D_ATTN) // HEAD_DIM
    v_col = (D_LRU + 2 * D_ATTN) // HEAD_DIM
    ones_rows = 2 * SUBLANES
    return pl.pallas_call(
        _moba_kernel,
        grid=(bsz, N_HEADS, nb // MOBA_Q_BLOCKS_PER_STEP),
        in_specs=[
            pl.BlockSpec((1, seq, HEAD_DIM), lambda b, h, i: (b, 0, q_col + h)),
            pl.BlockSpec((1, seq, HEAD_DIM), lambda b, h, i: (b, 0, k_col + h)),
            pl.BlockSpec((1, seq, HEAD_DIM), lambda b, h, i: (b, 0, v_col + h)),
        ],
        out_specs=pl.BlockSpec((1, MOBA_Q_BLOCKS_PER_STEP * MOBA_BLOCK, HEAD_DIM),
                               lambda b, h, i: (b, i, h)),
        out_shape=jax.ShapeDtypeStruct((bsz, seq, D_ATTN), BF16),
        scratch_shapes=[
            pltpu.VMEM((nb, seq), F32),
            pltpu.VMEM((MOBA_BLOCK, MOBA_BLOCK), F32),
            pltpu.VMEM((HEAD_DIM + ones_rows, seq), BF16),
            pltpu.VMEM((2, seq, MOBA_BLOCK), F32),
            pltpu.VMEM((2, 1, MOBA_BLOCK), F32),
            pltpu.VMEM((2, nb, MOBA_BLOCK), F32),
        ],
        compiler_params=_params(("parallel", "parallel", "arbitrary")),
        name="moba_attention",
    )(z3, z3, z3)


def _mix_kernel(x_ref, g_ref, h_ref, o_ref, gate_ref, wl_ref, wa_ref, wo_ref, out_ref):
    for r in range(TM // ROWS_MIX):
        rows = slice(r * ROWS_MIX, (r + 1) * ROWS_MIX)
        ya = jnp.dot(h_ref[rows, :], wl_ref[...], preferred_element_type=F32)
        yb = jnp.dot(o_ref[rows, :], wa_ref[...], preferred_element_type=F32)
        mix = (gate_ref[rows, 0:D_MODEL].astype(F32) * ya
               + gate_ref[rows, D_MODEL:2 * D_MODEL].astype(F32) * yb)
        y = jnp.dot(mix.astype(BF16), wo_ref[...], preferred_element_type=F32)
        out_ref[rows, :] = x_ref[rows, :] + y * _rms_scale(y) * g_ref[...]


def _mix(x, g, h, o, z, w_lru_up, w_attn_up, w_out, layer):
    t = x.shape[0]
    gate_blk = (D_LRU + 3 * D_ATTN) // (2 * D_MODEL)
    resident = lambda rows, cols: pl.BlockSpec((None, rows, cols), lambda i: (layer, 0, 0),
                                               pipeline_mode=pl.Buffered(1))
    return pl.pallas_call(
        _mix_kernel,
        grid=(t // TM,),
        in_specs=[
            pl.BlockSpec((TM, D_MODEL), lambda i: (i, 0)),
            pl.BlockSpec((1, D_MODEL), lambda i: (0, 0)),
            pl.BlockSpec((TM, D_LRU), lambda i: (i, 0)),
            pl.BlockSpec((TM, D_ATTN), lambda i: (i, 0)),
            pl.BlockSpec((TM, 2 * D_MODEL), lambda i: (i, gate_blk)),
            resident(D_LRU, D_MODEL),
            resident(D_ATTN, D_MODEL),
            resident(D_MODEL, D_MODEL),
        ],
        out_specs=pl.BlockSpec((TM, D_MODEL), lambda i: (i, 0)),
        out_shape=jax.ShapeDtypeStruct((t, D_MODEL), F32),
        compiler_params=_params(("parallel",)),
        name="mixer_merge_out_projection",
    )(x, g, h, o, z, w_lru_up, w_attn_up, w_out)


def _rope_tables(seq):
    pos = jnp.arange(seq, dtype=F32)
    inv = ROPE_THETA ** (-jnp.arange(0, ROT_DIM, 2, dtype=F32) / ROT_DIM)
    ang = pos[:, None] * inv[None, :]
    cos, sin = jnp.cos(ang), jnp.sin(ang)
    rest = HEAD_DIM - ROT_DIM
    zeros = jnp.zeros((seq, ROT_HALF), F32)
    tail0 = jnp.zeros((seq, rest), F32)
    cos_t = jnp.concatenate([cos, cos, jnp.ones((seq, rest), F32)], axis=1)
    sin_lo_t = jnp.concatenate([-sin, zeros, tail0], axis=1)
    sin_hi_t = jnp.concatenate([zeros, sin, tail0], axis=1)
    return cos_t, sin_lo_t, sin_hi_t


def _gate_weights(wa, wx):
    def pair_diag(w):
        depth = w.shape[0]
        w = w.reshape(depth, N_GATE_GROUPS, 2, LRU_BW, LRU_BW)
        z = jnp.zeros_like(w[:, :, 0])
        top = jnp.concatenate([w[:, :, 0], z], axis=-1)
        bot = jnp.concatenate([z, w[:, :, 1]], axis=-1)
        return jnp.concatenate([top, bot], axis=-2)
    return jnp.concatenate([pair_diag(wa), pair_diag(wx)], axis=-1).astype(BF16)


def kernel(x, norm_gains, ffn_w13, ffn_w2, w_in, b_gate, conv_w, conv_b, lru_wa, lru_ba,
           lru_wx, lru_bx, lru_lambda, w_lru_up, w_attn_up, w_out):
    bsz, seq, d = x.shape
    depth = norm_gains.shape[0]
    t = bsz * seq
    assert d == D_MODEL and seq % MOBA_BLOCK == 0 and seq % TM_IN == 0 and seq % TS_LRU == 0

    ffn_weights = {(0, 0): (ffn_w13[0, 0].astype(BF16), ffn_w2[0, 0].astype(BF16))}
    w_in_b = w_in.astype(BF16)
    w_lru_up_b = w_lru_up.astype(BF16)
    w_attn_up_b = w_attn_up.astype(BF16)
    w_out_b = w_out.astype(BF16)
    w_gate = _gate_weights(lru_wa, lru_wx)
    b_gate3 = b_gate.reshape(depth, 1, 2 * D_MODEL)
    conv_b3 = conv_b.reshape(depth, 1, D_LRU)
    ba3 = lru_ba.reshape(depth, 1, D_LRU)
    bx3 = lru_bx.reshape(depth, 1, D_LRU)
    lam3 = lru_lambda.reshape(depth, 1, D_LRU)
    rope_tables = _rope_tables(seq)

    xs = x.reshape(t, d)
    for l in range(depth):
        gain = lambda k: norm_gains[l, k].reshape(1, d)
        xs, xn = _ffn(xs, gain(0), gain(1), gain(2), *ffn_weights[l, 0])
        cast_jobs = [(l, 1)] + ([(l + 1, 0)] if l + 1 < depth else [])
        z, h, *casts = _inproj(xn, w_in_b, b_gate3, rope_tables, conv_w, conv_b3, w_gate, ba3,
                               bx3, lam3, l, seq, ffn_w13, ffn_w2, cast_jobs)
        for n, job in enumerate(cast_jobs):
            ffn_weights[job] = (casts[2 * n], casts[2 * n + 1])
        o = _moba(z.reshape(bsz, seq, IN_COLS))
        xs = _mix(xs, gain(3), h, o.reshape(t, D_ATTN), z,
                  w_lru_up_b, w_attn_up_b, w_out_b, l)
        xs = _ffn(xs, gain(4), gain(5), None, *ffn_weights[l, 1])
    return xs.reshape(bsz, seq, d)
```

```python
import functools

import jax
import jax.numpy as jnp
from jax import lax
from jax.experimental import pallas as pl
from jax.experimental.pallas import tpu as pltpu

F32 = jnp.float32
BF16 = jnp.bfloat16

D_MODEL = 2048
D_LRU = 1024
LRU_BLOCKS = 16
LRU_BW = D_LRU // LRU_BLOCKS
CONV_WIDTH = 4
LRU_C = 8.0
N_HEADS = 8
HEAD_DIM = 128
D_ATTN = N_HEADS * HEAD_DIM
MOBA_BLOCK = 256
MOBA_TOPK = 3
ROPE_THETA = 500000.0
ROT_DIM = HEAD_DIM // 4
ROT_HALF = ROT_DIM // 2
D_FF = 5632
NORM_EPS = 1e-6
IN_COLS = D_LRU + 3 * D_ATTN + 2 * D_MODEL

SUBLANES = 8
VMEM_LIMIT_BYTES = 62 * 1024 * 1024

TM = 512
TM_FFN = 1024
ROWS_FFN = 512
TF = 512
TM_IN = 1024
TN_IN = 1024
N_PIECES = 4
PIECE_COLS = TN_IN // N_PIECES
ROWS_MIX = 256
TS_LRU = 512
MOBA_Q_BLOCKS_PER_STEP = 8
GATE_GROUP = 2 * LRU_BW
N_GATE_GROUPS = D_LRU // GATE_GROUP


def _rms_scale(x):
    return lax.rsqrt(jnp.mean(x * x, axis=-1, keepdims=True) + NORM_EPS)


def _params(semantics):
    return pltpu.CompilerParams(dimension_semantics=semantics,
                                vmem_limit_bytes=VMEM_LIMIT_BYTES)


def _ffn_kernel(x_ref, gin_ref, gout_ref, gnext_ref, wa_ref, wb_ref, w2_ref, o_ref, *rest):
    next_ref = rest[0] if len(rest) == 2 else None
    xn_ref = rest[-1]
    j = pl.program_id(1)
    last = pl.num_programs(1) - 1

    def step(is_first, is_last):
        for r in range(TM_FFN // ROWS_FFN):
            rows = slice(r * ROWS_FFN, (r + 1) * ROWS_FFN)
            if is_first:
                x = x_ref[rows, :]
                xn = (x * _rms_scale(x) * gin_ref[...]).astype(BF16)
                xn_ref[rows, :] = xn
            else:
                xn = xn_ref[rows, :]
            a = jnp.dot(xn, wa_ref[...], preferred_element_type=F32)
            b = jnp.dot(xn, wb_ref[...], preferred_element_type=F32)
            h = (a * jax.nn.sigmoid(a) * b).astype(BF16)
            y = jnp.dot(h, w2_ref[...], preferred_element_type=F32)
            if not is_first:
                y = y + o_ref[rows, :]
            if is_last:
                y = x_ref[rows, :] + 0.5 * (y * _rms_scale(y) * gout_ref[...])
                if next_ref is not None:
                    next_ref[rows, :] = (y * _rms_scale(y) * gnext_ref[...]).astype(BF16)
            o_ref[rows, :] = y

    pl.when(j == 0)(functools.partial(step, True, False))
    pl.when((j > 0) & (j < last))(functools.partial(step, False, False))
    pl.when(j == last)(functools.partial(step, False, True))


def _ffn(x, gin, gout, gnext, w13, w2):
    t = x.shape[0]
    nf = D_FF // TF
    row_block = pl.BlockSpec((TM_FFN, D_MODEL), lambda i, j: (i, 0))
    gain_block = pl.BlockSpec((1, D_MODEL), lambda i, j: (0, 0))
    out_specs, out_shape = row_block, jax.ShapeDtypeStruct((t, D_MODEL), F32)
    if gnext is not None:
        out_specs = [row_block, row_block]
        out_shape = [out_shape, jax.ShapeDtypeStruct((t, D_MODEL), BF16)]
    return pl.pallas_call(
        _ffn_kernel,
        grid=(t // TM_FFN, nf),
        in_specs=[
            row_block, gain_block, gain_block, gain_block,
            pl.BlockSpec((D_MODEL, TF), lambda i, j: (0, j)),
            pl.BlockSpec((D_MODEL, TF), lambda i, j: (0, j + nf)),
            pl.BlockSpec((TF, D_MODEL), lambda i, j: (j, 0)),
        ],
        out_specs=out_specs,
        out_shape=out_shape,
        scratch_shapes=[pltpu.VMEM((TM_FFN, D_MODEL), BF16)],
        compiler_params=_params(("parallel", "arbitrary")),
        name="ffn_half_step",
    )(x, gin, gout, gout if gnext is None else gnext, w13, w13, w2)


LOG2_E = 1.4426950408889634
Q_SCALE = HEAD_DIM ** -0.5 * LOG2_E


def _lru_conv(chunk, piece, ubuf_ref, cw_ref, cb_ref, xc_ref):
    ts = TS_LRU
    base = SUBLANES + chunk * ts
    cols = slice(piece * PIECE_COLS, (piece + 1) * PIECE_COLS)
    xc = cb_ref[:, cols]
    for tap in range(CONV_WIDTH):
        off = base - (CONV_WIDTH - 1) + tap
        xc = xc + ubuf_ref[off:off + ts, cols] * cw_ref[tap:tap + 1, cols]
    xc_ref[:, cols] = xc


def _lru_gates(piece, xc_ref, wg_ref, ba_ref, bx_ref, lam_ref, a_ref, b_ref):
    groups_per_piece = PIECE_COLS // GATE_GROUP
    for g in range(piece * groups_per_piece, (piece + 1) * groups_per_piece):
        cols = slice(g * GATE_GROUP, (g + 1) * GATE_GROUP)
        xg = xc_ref[:, cols]
        lin = jnp.dot(xg.astype(BF16), wg_ref[g], preferred_element_type=F32)
        r = jax.nn.sigmoid(lin[:, :GATE_GROUP] + ba_ref[:, cols])
        i = jax.nn.sigmoid(lin[:, GATE_GROUP:] + bx_ref[:, cols])
        a = jnp.exp(r * (-LRU_C * jax.nn.softplus(-lam_ref[:, cols])))
        a_ref[:, cols] = a
        b_ref[:, cols] = jnp.sqrt(1.0 - a * a) * (i * xg)


def _lru_scan(chunk, piece, a_ref, b_ref, carry_ref, h_ref):
    steps = TS_LRU // N_PIECES
    h = carry_ref[...]
    for t in range(piece * steps, (piece + 1) * steps):
        h = a_ref[t:t + 1, :] * h + b_ref[t:t + 1, :]
        b_ref[t:t + 1, :] = h
    carry_ref[...] = h
    rows = slice(piece * steps, (piece + 1) * steps)
    out0 = chunk * TS_LRU + piece * steps
    h_ref[out0:out0 + steps, :] = b_ref[rows, :].astype(BF16)


def _inproj_kernel(tiles_per_seq, n_cast, xn_ref, w_ref, bg_ref, cos_ref, sin_lo_ref,
                   sin_hi_ref, cw_ref, cb_ref, wg_ref, ba_ref, bx_ref, lam_ref, *rest):
    cast_in = rest[:n_cast]
    z_ref, h_ref = rest[n_cast:n_cast + 2]
    cast_out = rest[n_cast + 2:2 * n_cast + 2]
    ubuf_ref, xc_ref, a_ref, b_ref, carry_ref = rest[2 * n_cast + 2:]
    i = pl.program_id(0)
    j = pl.program_id(1)
    q_blk = D_LRU // TN_IN
    k_blk = (D_LRU + D_ATTN) // TN_IN
    gate_blk = (D_LRU + 3 * D_ATTN) // TN_IN
    n_chunks = TM_IN // TS_LRU
    pad = SUBLANES
    conv = functools.partial(_lru_conv, ubuf_ref=ubuf_ref, cw_ref=cw_ref, cb_ref=cb_ref,
                             xc_ref=xc_ref)
    gates = functools.partial(_lru_gates, xc_ref=xc_ref, wg_ref=wg_ref, ba_ref=ba_ref,
                              bx_ref=bx_ref, lam_ref=lam_ref, a_ref=a_ref, b_ref=b_ref)
    scan = functools.partial(_lru_scan, a_ref=a_ref, b_ref=b_ref, carry_ref=carry_ref,
                             h_ref=h_ref)
    lru_work = {}
    for c in range(n_chunks):
        lru_work.setdefault(q_blk + 2 * c, []).append(functools.partial(conv, c))
        lru_work.setdefault(q_blk + 2 * c + 1, []).append(gates)
        lru_work.setdefault(q_blk + 2 * c + 2, []).append(functools.partial(scan, c))
    assert max(lru_work) < IN_COLS // TN_IN and q_blk == 1

    def project(w, cols=slice(None)):
        return jnp.dot(xn_ref[...], w[:, cols], preferred_element_type=F32)

    def cast_weight_tiles():
        for src_ref, dst_ref in zip(cast_in, cast_out):
            dst_ref[...] = src_ref[...].astype(BF16)

    def step_body(step, lru_stages):
        cast_weight_tiles()
        for p in range(N_PIECES):
            cols = slice(p * PIECE_COLS, (p + 1) * PIECE_COLS)
            acc = project(w_ref, cols)
            if step in (q_blk, k_blk):
                heads = []
                for hh in range(PIECE_COLS // HEAD_DIM):
                    xh = acc[:, hh * HEAD_DIM:(hh + 1) * HEAD_DIM]
                    up = pltpu.roll(xh, HEAD_DIM - ROT_HALF, 1)
                    down = pltpu.roll(xh, ROT_HALF, 1)
                    heads.append(xh * cos_ref[...] + up * sin_lo_ref[...] + down * sin_hi_ref[...])
                acc = jnp.concatenate(heads, axis=1)
                if step == q_blk:
                    acc = acc * Q_SCALE
            elif step >= gate_blk:
                acc = jax.nn.sigmoid(acc + bg_ref[:, cols])
            z_ref[:, cols] = acc.astype(BF16)
            for stage in lru_stages:
                stage(p)

    @pl.when(j == 0)
    def _():
        ubuf_ref[0:pad, :] = ubuf_ref[TM_IN:TM_IN + pad, :]

        @pl.when(i % tiles_per_seq == 0)
        def _():
            ubuf_ref[0:pad, :] = jnp.zeros((pad, D_LRU), F32)
            carry_ref[...] = jnp.zeros_like(carry_ref)

        cast_weight_tiles()
        u = project(w_ref)
        ubuf_ref[pad:pad + TM_IN, :] = u
        z_ref[...] = u.astype(BF16)

    special = sorted(set(lru_work) | {q_blk, k_blk, gate_blk - 1, gate_blk})
    for step in special:
        pl.when(j == step)(functools.partial(step_body, step, lru_work.get(step, [])))
    pl.when(j > max(special))(functools.partial(step_body, gate_blk, []))


def _cast_job_specs(arr, lead, n_i, n_j):
    rows, cols = arr.shape[-2:]
    steps = n_i * n_j
    if rows % steps == 0 and (rows // steps) % (2 * SUBLANES) == 0:
        block, index = (rows // steps, cols), lambda i, j: (i * n_j + j, 0)
    else:
        block, index = (rows // n_j, cols // n_i), lambda i, j: (j, i)
        assert rows % n_j == 0 and block[0] % (2 * SUBLANES) == 0
        assert cols % n_i == 0 and block[1] % HEAD_DIM == 0
    in_spec = pl.BlockSpec((None,) * len(lead) + block, lambda i, j: tuple(lead) + index(i, j))
    return in_spec, pl.BlockSpec(block, index), jax.ShapeDtypeStruct((rows, cols), BF16)


def _inproj(xn, w_in, b_gate, rope_tables, conv_w, conv_b, w_gate, ba, bx, lam, layer, seq,
            cast_jobs):
    t = xn.shape[0]
    n_i, n_j = t // TM_IN, IN_COLS // TN_IN
    cast_args = [arr for arr, _ in cast_jobs]
    cast_specs = [_cast_job_specs(arr, lead, n_i, n_j) for arr, lead in cast_jobs]
    cast_in_specs = [s[0] for s in cast_specs]
    cast_out_specs = [s[1] for s in cast_specs]
    cast_out_shapes = [s[2] for s in cast_specs]
    gate_blk = (D_LRU + 3 * D_ATTN) // TN_IN
    seq_tiles = seq // TM_IN
    table = lambda: pl.BlockSpec((TM_IN, HEAD_DIM), lambda i, j: (i % seq_tiles, 0))
    vec = lambda: pl.BlockSpec((None, 1, D_LRU), lambda i, j: (layer, 0, 0))
    return pl.pallas_call(
        functools.partial(_inproj_kernel, seq_tiles, len(cast_args)),
        grid=(t // TM_IN, IN_COLS // TN_IN),
        in_specs=[
            pl.BlockSpec((TM_IN, D_MODEL), lambda i, j: (i, 0)),
            pl.BlockSpec((D_MODEL, TN_IN), lambda i, j: (0, j)),
            pl.BlockSpec((None, 1, TN_IN), lambda i, j: (layer, 0, jnp.maximum(j - gate_blk, 0))),
            table(), table(), table(),
            pl.BlockSpec((None, CONV_WIDTH, D_LRU), lambda i, j: (layer, 0, 0)),
            vec(),
            pl.BlockSpec((None, N_GATE_GROUPS, GATE_GROUP, 2 * GATE_GROUP),
                         lambda i, j: (layer, 0, 0, 0)),
            vec(), vec(), vec(),
        ] + cast_in_specs,
        out_specs=[
            pl.BlockSpec((TM_IN, TN_IN), lambda i, j: (i, j)),
            pl.BlockSpec((TM_IN, D_LRU), lambda i, j: (i, 0)),
        ] + cast_out_specs,
        out_shape=[jax.ShapeDtypeStruct((t, IN_COLS), BF16),
                   jax.ShapeDtypeStruct((t, D_LRU), BF16)] + cast_out_shapes,
        scratch_shapes=[
            pltpu.VMEM((SUBLANES + TM_IN, D_LRU), F32),
            pltpu.VMEM((TS_LRU, D_LRU), F32),
            pltpu.VMEM((TS_LRU, D_LRU), F32),
            pltpu.VMEM((TS_LRU, D_LRU), F32),
            pltpu.VMEM((1, D_LRU), F32),
        ],
        compiler_params=_params(("arbitrary", "arbitrary")),
        name="mixer_in_projection_and_rg_lru",
    )(xn, w_in, b_gate, *rope_tables, conv_w, conv_b, w_gate, ba, bx, lam, *cast_args)


def _nt_dot(a, b):
    return lax.dot_general(a, b, (((1,), (1,)), ((), ())), preferred_element_type=F32)


def _moba_prepare(q_ref, k_ref, v_ref, bias_ref, causal_ref, vt_ref):
    nb, seq = bias_ref.shape
    blk = MOBA_BLOCK
    neg = -jnp.inf
    kmean = jnp.mean(k_ref[0].astype(F32).reshape(nb, blk, HEAD_DIM), axis=1)
    hi = kmean.astype(BF16)
    lo = (kmean - hi.astype(F32)).astype(BF16)
    q_all = q_ref[0]
    gate = _nt_dot(hi, q_all) + _nt_dot(lo, q_all)
    blk_id = lax.broadcasted_iota(jnp.int32, (nb, seq), 0)
    qry_blk = lax.broadcasted_iota(jnp.int32, (nb, seq), 1) // blk
    gate = jnp.where(blk_id < qry_blk, gate, neg)
    bias = jnp.full((nb, seq), neg, F32)
    for _ in range(MOBA_TOPK):
        top = jnp.max(gate, axis=0, keepdims=True)
        first = jnp.min(jnp.where(gate == top, blk_id, nb), axis=0, keepdims=True)
        pick = (blk_id == first) & (top > neg)
        bias = jnp.where(pick, 0.0, bias)
        gate = jnp.where(pick, neg, gate)
    bias_ref[...] = bias
    key_pos = lax.broadcasted_iota(jnp.int32, (blk, blk), 0)
    qry_pos = lax.broadcasted_iota(jnp.int32, (blk, blk), 1)
    causal_ref[...] = jnp.where(key_pos <= qry_pos, 0.0, neg)
    for n in range(nb):
        cols = slice(n * blk, (n + 1) * blk)
        vt_ref[0:HEAD_DIM, cols] = v_ref[0, cols, :].astype(F32).T.astype(BF16)
    vt_ref[HEAD_DIM:, :] = jnp.ones((vt_ref.shape[0] - HEAD_DIM, seq), BF16)


def _moba_scores(j, slot, q_ref, k_ref, bias_ref, causal_ref, s_ref, m_ref, shift_ref):
    blk = MOBA_BLOCK
    lo, hi = j * blk, (j + 1) * blk
    q = q_ref[0, lo:hi, :]
    s_own = _nt_dot(k_ref[0, lo:hi, :], q) + causal_ref[...]
    s_ref[slot, lo:hi, :] = s_own
    m = jnp.max(s_own, axis=0, keepdims=True)
    if j > 0:
        s_past = _nt_dot(k_ref[0, 0:lo, :], q)
        s_ref[slot, 0:lo, :] = s_past
        bias = bias_ref[0:j, lo:hi]
        blk_max = jnp.max(s_past.reshape(j, blk, blk), axis=1)
        m = jnp.maximum(m, jnp.max(blk_max + bias, axis=0, keepdims=True))
        shift_ref[slot, 0:j, :] = m - bias
    m_ref[slot] = m


def _moba_output(i, slot, out_row, o_ref, vt_ref, s_ref, m_ref, shift_ref):
    blk = MOBA_BLOCK
    lo, hi = i * blk, (i + 1) * blk
    p_own = jnp.exp2(s_ref[slot, lo:hi, :] - m_ref[slot])
    acc = jnp.dot(vt_ref[:, lo:hi], p_own.astype(BF16), preferred_element_type=F32)
    if i > 0:
        s_past = s_ref[slot, 0:lo, :].reshape(i, blk, blk)
        p_past = jnp.exp2(s_past - shift_ref[slot, 0:i, :][:, None, :]).reshape(lo, blk)
        acc = acc + jnp.dot(vt_ref[:, 0:lo], p_past.astype(BF16), preferred_element_type=F32)
    inv_l = 1.0 / acc[HEAD_DIM:HEAD_DIM + 1, :]
    o_ref[0, out_row:out_row + blk, :] = (acc[0:HEAD_DIM, :] * inv_l).T.astype(BF16)


def _moba_kernel(q_ref, k_ref, v_ref, o_ref, bias_ref, causal_ref, vt_ref, s_ref, m_ref,
                 shift_ref):
    qi = pl.program_id(2)
    nb = bias_ref.shape[0]

    @pl.when(qi == 0)
    def _():
        _moba_prepare(q_ref, k_ref, v_ref, bias_ref, causal_ref, vt_ref)
        _moba_scores(0, 0, q_ref, k_ref, bias_ref, causal_ref, s_ref, m_ref, shift_ref)

    for g in range(nb // MOBA_Q_BLOCKS_PER_STEP):
        @pl.when(qi == g)
        def _(g=g):
            for r in range(MOBA_Q_BLOCKS_PER_STEP):
                i = g * MOBA_Q_BLOCKS_PER_STEP + r
                if i + 1 < nb:
                    _moba_scores(i + 1, (i + 1) % 2, q_ref, k_ref, bias_ref, causal_ref,
                                 s_ref, m_ref, shift_ref)
                _moba_output(i, i % 2, r * MOBA_BLOCK, o_ref, vt_ref, s_ref, m_ref, shift_ref)


def _moba(z3):
    bsz, seq, _ = z3.shape
    nb = seq // MOBA_BLOCK
    q_col = D_LRU // HEAD_DIM
    k_col = (D_LRU + D_ATTN) // HEAD_DIM
    v_col = (D_LRU + 2 * D_ATTN) // HEAD_DIM
    ones_rows = 2 * SUBLANES
    return pl.pallas_call(
        _moba_kernel,
        grid=(bsz, N_HEADS, nb // MOBA_Q_BLOCKS_PER_STEP),
        in_specs=[
            pl.BlockSpec((1, seq, HEAD_DIM), lambda b, h, i: (b, 0, q_col + h)),
            pl.BlockSpec((1, seq, HEAD_DIM), lambda b, h, i: (b, 0, k_col + h)),
            pl.BlockSpec((1, seq, HEAD_DIM), lambda b, h, i: (b, 0, v_col + h)),
        ],
        out_specs=pl.BlockSpec((1, MOBA_Q_BLOCKS_PER_STEP * MOBA_BLOCK, HEAD_DIM),
                               lambda b, h, i: (b, i, h)),
        out_shape=jax.ShapeDtypeStruct((bsz, seq, D_ATTN), BF16),
        scratch_shapes=[
            pltpu.VMEM((nb, seq), F32),
            pltpu.VMEM((MOBA_BLOCK, MOBA_BLOCK), F32),
            pltpu.VMEM((HEAD_DIM + ones_rows, seq), BF16),
            pltpu.VMEM((2, seq, MOBA_BLOCK), F32),
            pltpu.VMEM((2, 1, MOBA_BLOCK), F32),
            pltpu.VMEM((2, nb, MOBA_BLOCK), F32),
        ],
        compiler_params=_params(("parallel", "parallel", "arbitrary")),
        name="moba_attention",
    )(z3, z3, z3)


def _mix_kernel(x_ref, g_ref, h_ref, o_ref, gate_ref, wl_ref, wa_ref, wo_ref, out_ref):
    for r in range(TM // ROWS_MIX):
        rows = slice(r * ROWS_MIX, (r + 1) * ROWS_MIX)
        ya = jnp.dot(h_ref[rows, :], wl_ref[...], preferred_element_type=F32)
        yb = jnp.dot(o_ref[rows, :], wa_ref[...], preferred_element_type=F32)
        mix = (gate_ref[rows, 0:D_MODEL].astype(F32) * ya
               + gate_ref[rows, D_MODEL:2 * D_MODEL].astype(F32) * yb)
        y = jnp.dot(mix.astype(BF16), wo_ref[...], preferred_element_type=F32)
        out_ref[rows, :] = x_ref[rows, :] + y * _rms_scale(y) * g_ref[...]


def _mix(x, g, h, o, z, w_lru_up, w_attn_up, w_out):
    t = x.shape[0]
    gate_blk = (D_LRU + 3 * D_ATTN) // (2 * D_MODEL)
    resident = lambda rows, cols: pl.BlockSpec((rows, cols), lambda i: (0, 0),
                                               pipeline_mode=pl.Buffered(1))
    return pl.pallas_call(
        _mix_kernel,
        grid=(t // TM,),
        in_specs=[
            pl.BlockSpec((TM, D_MODEL), lambda i: (i, 0)),
            pl.BlockSpec((1, D_MODEL), lambda i: (0, 0)),
            pl.BlockSpec((TM, D_LRU), lambda i: (i, 0)),
            pl.BlockSpec((TM, D_ATTN), lambda i: (i, 0)),
            pl.BlockSpec((TM, 2 * D_MODEL), lambda i: (i, gate_blk)),
            resident(D_LRU, D_MODEL),
            resident(D_ATTN, D_MODEL),
            resident(D_MODEL, D_MODEL),
        ],
        out_specs=pl.BlockSpec((TM, D_MODEL), lambda i: (i, 0)),
        out_shape=jax.ShapeDtypeStruct((t, D_MODEL), F32),
        compiler_params=_params(("parallel",)),
        name="mixer_merge_out_projection",
    )(x, g, h, o, z, w_lru_up, w_attn_up, w_out)


def _rope_tables(seq):
    pos = jnp.arange(seq, dtype=F32)
    inv = ROPE_THETA ** (-jnp.arange(0, ROT_DIM, 2, dtype=F32) / ROT_DIM)
    ang = pos[:, None] * inv[None, :]
    cos, sin = jnp.cos(ang), jnp.sin(ang)
    rest = HEAD_DIM - ROT_DIM
    zeros = jnp.zeros((seq, ROT_HALF), F32)
    tail0 = jnp.zeros((seq, rest), F32)
    cos_t = jnp.concatenate([cos, cos, jnp.ones((seq, rest), F32)], axis=1)
    sin_lo_t = jnp.concatenate([-sin, zeros, tail0], axis=1)
    sin_hi_t = jnp.concatenate([zeros, sin, tail0], axis=1)
    return cos_t, sin_lo_t, sin_hi_t


def _gate_weights(wa, wx):
    def pair_diag(w):
        depth = w.shape[0]
        w = w.reshape(depth, N_GATE_GROUPS, 2, LRU_BW, LRU_BW)
        z = jnp.zeros_like(w[:, :, 0])
        top = jnp.concatenate([w[:, :, 0], z], axis=-1)
        bot = jnp.concatenate([z, w[:, :, 1]], axis=-1)
        return jnp.concatenate([top, bot], axis=-2)
    return jnp.concatenate([pair_diag(wa), pair_diag(wx)], axis=-1).astype(BF16)


def kernel(x, norm_gains, ffn_w13, ffn_w2, w_in, b_gate, conv_w, conv_b, lru_wa, lru_ba,
           lru_wx, lru_bx, lru_lambda, w_lru_up, w_attn_up, w_out):
    bsz, seq, d = x.shape
    depth = norm_gains.shape[0]
    t = bsz * seq
    assert d == D_MODEL and seq % MOBA_BLOCK == 0 and seq % TM_IN == 0 and seq % TS_LRU == 0

    ffn_weights = {(0, 0): (ffn_w13[0, 0].astype(BF16), ffn_w2[0, 0].astype(BF16))}
    w_in_b = {0: w_in[0].astype(BF16)}
    w_gate = _gate_weights(lru_wa, lru_wx)
    b_gate3 = b_gate.reshape(depth, 1, 2 * D_MODEL)
    conv_b3 = conv_b.reshape(depth, 1, D_LRU)
    ba3 = lru_ba.reshape(depth, 1, D_LRU)
    bx3 = lru_bx.reshape(depth, 1, D_LRU)
    lam3 = lru_lambda.reshape(depth, 1, D_LRU)
    rope_tables = _rope_tables(seq)

    xs = x.reshape(t, d)
    for l in range(depth):
        gain = lambda k: norm_gains[l, k].reshape(1, d)
        xs, xn = _ffn(xs, gain(0), gain(1), gain(2), *ffn_weights[l, 0])
        cast_jobs = [(w_lru_up, (l,)), (w_attn_up, (l,)), (w_out, (l,)),
                     (ffn_w13, (l, 1)), (ffn_w2, (l, 1))]
        if l + 1 < depth:
            cast_jobs += [(ffn_w13, (l + 1, 0)), (ffn_w2, (l + 1, 0)), (w_in, (l + 1,))]
        z, h, *casts = _inproj(xn, w_in_b[l], b_gate3, rope_tables, conv_w, conv_b3, w_gate, ba3,
                               bx3, lam3, l, seq, cast_jobs)
        merge_weights = casts[0:3]
        ffn_weights[l, 1] = casts[3:5]
        if l + 1 < depth:
            ffn_weights[l + 1, 0] = casts[5:7]
            w_in_b[l + 1] = casts[7]
        o = _moba(z.reshape(bsz, seq, IN_COLS))
        xs = _mix(xs, gain(3), h, o.reshape(t, D_ATTN), z, *merge_weights)
        xs = _ffn(xs, gain(4), gain(5), None, *ffn_weights[l, 1])
    return xs.reshape(bsz, seq, d)
```
